```python
import numpy as np
import jax
import jax.numpy as jnp
from jax import lax

D_MODEL = 1024
BATCH = 16
SEQ = 2048
DEPTH = 2

GRID_W = 64
CTX_LEN = 256
EPS = 1e-6
ROPE_BASE = 10000.0
Q_BLOCK = 128
D_FF = 4 * D_MODEL
MIX_W = D_MODEL

MLA_HEADS = D_MODEL // 128
MLA_NOPE = 64
MLA_ROPE = 32
MLA_QK = MLA_NOPE + MLA_ROPE
MLA_V = 64
MLA_Q_LORA = D_MODEL // 4
MLA_KV_LORA = D_MODEL // 8

LRU_W = MIX_W - MLA_HEADS * MLA_V
LRU_BLOCKS = 8
LRU_BS = LRU_W // LRU_BLOCKS
LRU_C = 8.0
CONV_W = 4
CONV_LEFT = CONV_W // 2

GLA_HEADS = D_MODEL // 256
GLA_DK = 64
GLA_DV = 128
GLA_LR = 16
GLA_TAU = 16.0
GLA_CHUNK = 64

NA_HD = 64
NA_HEADS = (MIX_W - GLA_HEADS * GLA_DV) // NA_HD
NA_WIN_ROWS = 8
NA_WIN_COLS = 16

EV_SIZES = (MLA_Q_LORA, MLA_KV_LORA, MLA_ROPE, LRU_W, LRU_W)
OD_SIZES = (GLA_HEADS * GLA_DK, GLA_HEADS * GLA_DK, GLA_HEADS * GLA_DV, GLA_HEADS * GLA_DV, 2 * GLA_LR,
            NA_HEADS * NA_HD, NA_HEADS * NA_HD, NA_HEADS * NA_HD)
EV_IN = sum(EV_SIZES)
OD_IN = sum(OD_SIZES)

kernel_name = 'hybrid_mla_rglru_gla_natten_dit'


def _split(t, sizes):
    return jnp.split(t, np.cumsum(sizes)[:-1].tolist(), axis=-1)


def rms_norm(x, g):
    xf = x.astype(jnp.float32)
    y = xf * lax.rsqrt(jnp.mean(xf * xf, axis=-1, keepdims=True) + EPS)
    return (y * g.astype(jnp.float32)).astype(x.dtype)


def _modulation(cv, ada_w, ada_b):
    return jnp.split(jax.nn.silu(cv) @ ada_w + ada_b, 6, axis=-1)


def _modulate(x, g, shift, scale):
    return rms_norm(x, g) * (1.0 + scale) + shift


def _sq_relu_mlp(h, w1, w2):
    return jnp.square(jax.nn.relu(h @ w1)) @ w2


def axial_rope(x):
    n = x.shape[1]
    pos = jnp.arange(n)
    half = x.shape[-1] // 2
    inv = ROPE_BASE ** (-jnp.arange(0, half, 2, dtype=jnp.float32) / half)

    def rot(xh, p):
        ang = p.astype(jnp.float32)[:, None] * inv[None, :]
        ang = jnp.concatenate([ang, ang], axis=-1)[None, :, None, :]
        xf = xh.astype(jnp.float32)
        x1, x2 = jnp.split(xf, 2, axis=-1)
        return (xf * jnp.cos(ang) + jnp.concatenate([-x2, x1], axis=-1) * jnp.sin(ang)).astype(x.dtype)

    return jnp.concatenate([rot(x[..., :half], pos // GRID_W), rot(x[..., half:], pos % GRID_W)], axis=-1)


def attend(q, k, v, scale):
    s = jnp.einsum('bqhd,bkhd->bhqk', q, k).astype(jnp.float32) * scale
    p = jax.nn.softmax(s, axis=-1).astype(v.dtype)
    return jnp.einsum('bhqk,bkhe->bqhe', p, v)


def block_attend(q, k, v, scale):
    b, n, h, d = q.shape
    qb = jnp.moveaxis(q.reshape(b, n // Q_BLOCK, Q_BLOCK, h, d), 1, 0)
    out = lax.map(lambda qq: attend(qq, k, v, scale), qb)
    return jnp.moveaxis(out, 0, 1).reshape(b, n, h, v.shape[-1])


def _mla(uq, ukv, ukr, uqc, ukvc, ukrc, q_norm, w_uq, kv_norm, w_ukv, q_gain, k_gain, need_ctx):
    def qkv(uq, ukv, ukr, use_rope):
        b, n, _ = uq.shape
        q = (rms_norm(uq, q_norm) @ w_uq).reshape(b, n, MLA_HEADS, MLA_QK)
        kv = (rms_norm(ukv, kv_norm) @ w_ukv).reshape(b, n, MLA_HEADS, MLA_NOPE + MLA_V)
        k_rope = jnp.broadcast_to(ukr[:, :, None, :], (b, n, MLA_HEADS, MLA_ROPE))
        q = rms_norm(q, q_gain)
        k = rms_norm(jnp.concatenate([kv[..., :MLA_NOPE], k_rope], axis=-1), k_gain)
        v = kv[..., MLA_NOPE:]
        if use_rope:
            q = jnp.concatenate([q[..., :MLA_NOPE], axial_rope(q[..., MLA_NOPE:])], axis=-1)
            k = jnp.concatenate([k[..., :MLA_NOPE], axial_rope(k[..., MLA_NOPE:])], axis=-1)
        return q, k, v

    scale = MLA_QK ** -0.5
    q, k, v = qkv(uq, ukv, ukr, True)
    qc, kc, vc = qkv(uqc, ukvc, ukrc, False)
    b, n = q.shape[:2]
    y = block_attend(q, jnp.concatenate([k, kc], axis=1), jnp.concatenate([v, vc], axis=1), scale)
    y = y.reshape(b, n, MLA_HEADS * MLA_V)
    yc = attend(qc, kc, vc, scale).reshape(qc.shape[0], qc.shape[1], MLA_HEADS * MLA_V) if need_ctx else None
    return y, yc


def _dwconv(x, w, b):
    n = x.shape[1]
    xp = jnp.pad(x, ((0, 0), (CONV_LEFT, CONV_W - 1 - CONV_LEFT), (0, 0)))
    y = b + xp[:, 0:n] * w[0]
    for j in range(1, CONV_W):
        y = y + xp[:, j:j + n] * w[j]
    return y


def _lru_coeffs(x, w_a, b_a, w_x, b_x, lam):
    b, n, _ = x.shape
    xb = x.reshape(b, n, LRU_BLOCKS, LRU_BS)
    r = jax.nn.sigmoid(jnp.einsum('bnki,kij->bnkj', xb, w_a).reshape(b, n, LRU_W) + b_a)
    i = jax.nn.sigmoid(jnp.einsum('bnki,kij->bnkj', xb, w_x).reshape(b, n, LRU_W) + b_x)
    log_a = -LRU_C * r * jax.nn.softplus(-lam.astype(jnp.float32))
    return jnp.exp(log_a), jnp.sqrt(-jnp.expm1(2.0 * log_a)) * (i * x)


def _linear_scan(a, bx, h0):
    def comb(l, r):
        return l[0] * r[0], r[0] * l[1] + r[1]
    acum, bcum = lax.associative_scan(comb, (a, bx), axis=1)
    return acum * h0[:, None, :] + bcum


def _rglru(ux, ug, uxc, ugc, conv_w, conv_b, w_a, b_a, w_x, b_x, lam, need_ctx):
    xl = _dwconv(ux, conv_w, conv_b).astype(jnp.float32)
    xc = _dwconv(uxc, conv_w, conv_b).astype(jnp.float32)
    flip = lambda t: jnp.flip(t, axis=1)
    h_sum, hc_sum = 0.0, 0.0
    for d in range(2):
        a, bx = _lru_coeffs(xl, w_a[d], b_a[d], w_x[d], b_x[d], lam[d])
        ac, bc = _lru_coeffs(xc, w_a[d], b_a[d], w_x[d], b_x[d], lam[d])
        if d == 1:
            a, bx, ac, bc = flip(a), flip(bx), flip(ac), flip(bc)
        hc = _linear_scan(ac, bc, jnp.zeros((xc.shape[0], LRU_W), jnp.float32))
        h = _linear_scan(a, bx, hc[:, -1])
        if d == 1:
            h, hc = flip(h), flip(hc)
        h_sum = h_sum + h
        hc_sum = hc_sum + hc
    y = (h_sum * jax.nn.gelu(ug.astype(jnp.float32))).astype(ux.dtype)
    yc = (hc_sum * jax.nn.gelu(ugc.astype(jnp.float32))).astype(ux.dtype) if need_ctx else None
    return y, yc


def _gla_chunk_scan(q, k, v, log_a, s0, with_output=True):
    b, n, h, dk = q.shape
    dv = v.shape[-1]
    nc = n // GLA_CHUNK
    q, k, v, log_a = [t.astype(jnp.float32).reshape(b, nc, GLA_CHUNK, h, t.shape[-1]) for t in (q, k, v, log_a)]
    cum = jnp.cumsum(log_a, axis=2)
    cum_last = cum[:, :, -1]
    d_state = jnp.einsum('bnshd,bnshe->bnhde', k * jnp.exp(cum_last[:, :, None] - cum), v)

    def step(s, inp):
        decay, ds = inp
        return decay[..., None] * s + ds, s

    s_fin, s_in = lax.scan(step, s0.astype(jnp.float32),
                           (jnp.moveaxis(jnp.exp(cum_last), 1, 0), jnp.moveaxis(d_state, 1, 0)))
    if not with_output:
        return None, s_fin
    s_in = jnp.moveaxis(s_in, 0, 1)
    q_dec = q * jnp.exp(cum)
    att = jnp.einsum('bnthd,bnshd->bnhts', q_dec, k * jnp.exp(-cum))
    att = jnp.where(jnp.tril(jnp.ones((GLA_CHUNK, GLA_CHUNK), dtype=bool)), att, 0.0)
    o = jnp.einsum('bnhts,bnshe->bnthe', att, v) + jnp.einsum('bnthd,bnhde->bnthe', q_dec, s_in)
    return o.reshape(b, n, h, dv), s_fin


def _gla(gq, gk, gv, gg, glr, gqc, gkc, gvc, ggc, glrc, w_a, b_a, o_gain, need_ctx):
    def heads(t, d):
        return t.reshape(t.shape[0], t.shape[1], -1, d)

    def log_decay(lr, d):
        z = lr[..., d * GLA_LR:(d + 1) * GLA_LR] @ w_a[d] + b_a[d]
        return heads(jax.nn.log_sigmoid(z.astype(jnp.float32)) / GLA_TAU, GLA_DK)

    sc = GLA_DK ** -0.5
    q, k, v = heads(gq, GLA_DK) * sc, heads(gk, GLA_DK), heads(gv, GLA_DV)
    qc, kc, vc = heads(gqc, GLA_DK) * sc, heads(gkc, GLA_DK), heads(gvc, GLA_DV)
    zero = jnp.zeros((qc.shape[0], GLA_HEADS, GLA_DK, GLA_DV), jnp.float32)
    flip = lambda t: jnp.flip(t, axis=1)
    oc_f, sc_f = _gla_chunk_scan(qc, kc, vc, log_decay(glrc, 0), zero, need_ctx)
    o_f, _ = _gla_chunk_scan(q, k, v, log_decay(glr, 0), sc_f)
    oc_b, sc_b = _gla_chunk_scan(flip(qc), flip(kc), flip(vc), flip(log_decay(glrc, 1)), zero, need_ctx)
    o_b, _ = _gla_chunk_scan(flip(q), flip(k), flip(v), flip(log_decay(glr, 1)), sc_b)

    def finish(o, g):
        return (rms_norm(o, o_gain).reshape(g.shape) * jax.nn.silu(g.astype(jnp.float32))).astype(g.dtype)

    y = finish(o_f + flip(o_b), gg)
    yc = finish(oc_f + flip(oc_b), ggc) if need_ctx else None
    return y, yc


def _natten(nq, nk, nv, nqc, nkc, nvc, q_gain, k_gain, rpb, need_ctx):
    def heads(t, g=None):
        t = t.reshape(t.shape[0], t.shape[1], NA_HEADS, NA_HD)
        return rms_norm(t, g) if g is not None else t

    q, k, v = heads(nq, q_gain), heads(nk, k_gain), heads(nv)
    qc, kc, vc = heads(nqc, q_gain), heads(nkc, k_gain), heads(nvc)
    b, n = q.shape[:2]
    rows = n // GRID_W
    wr = min(NA_WIN_ROWS, rows)
    scale = NA_HD ** -0.5
    qg = q.reshape(b, rows, GRID_W, NA_HEADS, NA_HD)
    kg = k.reshape(b, rows, GRID_W, NA_HEADS, NA_HD)
    vg = v.reshape(b, rows, GRID_W, NA_HEADS, NA_HD)
    cols = jnp.arange(GRID_W)
    c0 = jnp.clip(cols - NA_WIN_COLS // 2, 0, GRID_W - NA_WIN_COLS)
    col_idx = c0[:, None] + jnp.arange(NA_WIN_COLS)[None, :]
    dj = col_idx - cols[:, None] + (NA_WIN_COLS - 1)

    def row_block(args):
        q_r, r = args
        r0 = jnp.clip(r - wr // 2, 0, rows - wr)
        k_win = lax.dynamic_slice_in_dim(kg, r0, wr, axis=1)[:, :, col_idx]
        v_win = lax.dynamic_slice_in_dim(vg, r0, wr, axis=1)[:, :, col_idx]
        di = r0 + jnp.arange(wr) - r + (NA_WIN_ROWS - 1)
        bias = rpb[:, di[:, None, None], dj[None, :, :]]
        s_lat = jnp.einsum('bchd,bicjhd->bhcij', q_r, k_win).astype(jnp.float32) * scale
        s_lat = s_lat + jnp.transpose(bias, (0, 2, 1, 3))[None].astype(jnp.float32)
        s_ctx = jnp.einsum('bchd,bkhd->bhck', q_r, kc).astype(jnp.float32) * scale
        s = jnp.concatenate([s_lat.reshape(b, NA_HEADS, GRID_W, wr * NA_WIN_COLS), s_ctx], axis=-1)
        p = jax.nn.softmax(s, axis=-1).astype(v.dtype)
        p_lat = p[..., :wr * NA_WIN_COLS].reshape(b, NA_HEADS, GRID_W, wr, NA_WIN_COLS)
        return (jnp.einsum('bhcij,bicjhd->bchd', p_lat, v_win)
                + jnp.einsum('bhck,bkhd->bchd', p[..., wr * NA_WIN_COLS:], vc))

    out = lax.map(row_block, (jnp.moveaxis(qg, 1, 0), jnp.arange(rows)))
    y = jnp.moveaxis(out, 0, 1).reshape(b, n, NA_HEADS * NA_HD)
    yc = attend(qc, kc, vc, scale).reshape(qc.shape[0], qc.shape[1], NA_HEADS * NA_HD) if need_ctx else None
    return y, yc


def _even_mixer(h, hc, w_in, w_out, q_norm, w_uq, kv_norm, w_ukv, q_gain, k_gain,
                conv_w, conv_b, w_a, b_a, w_x, b_x, lam, need_ctx):
    uq, ukv, ukr, ux, ug = _split(h @ w_in, EV_SIZES)
    uqc, ukvc, ukrc, uxc, ugc = _split(hc @ w_in, EV_SIZES)
    y_a, yc_a = _mla(uq, ukv, ukr, uqc, ukvc, ukrc, q_norm, w_uq, kv_norm, w_ukv, q_gain, k_gain, need_ctx)
    y_b, yc_b = _rglru(ux, ug, uxc, ugc, conv_w, conv_b, w_a, b_a, w_x, b_x, lam, need_ctx)
    y = jnp.concatenate([y_a, y_b.astype(y_a.dtype)], axis=-1) @ w_out
    yc = jnp.concatenate([yc_a, yc_b.astype(yc_a.dtype)], axis=-1) @ w_out if need_ctx else None
    return y, yc


def _odd_mixer(h, hc, w_in, w_out, gla_w_a, gla_b_a, gla_o_gain, na_q_gain, na_k_gain, na_rpb, need_ctx):
    gq, gk, gv, gg, glr, nq, nk, nv = _split(h @ w_in, OD_SIZES)
    gqc, gkc, gvc, ggc, glrc, nqc, nkc, nvc = _split(hc @ w_in, OD_SIZES)
    y_c, yc_c = _gla(gq, gk, gv, gg, glr, gqc, gkc, gvc, ggc, glrc, gla_w_a, gla_b_a, gla_o_gain, need_ctx)
    y_d, yc_d = _natten(nq, nk, nv, nqc, nkc, nvc, na_q_gain, na_k_gain, na_rpb, need_ctx)
    y = jnp.concatenate([y_c, y_d], axis=-1) @ w_out
    yc = jnp.concatenate([yc_c, yc_d], axis=-1) @ w_out if need_ctx else None
    return y, yc


def setup_inputs(seed: int = 0) -> dict:
    key = jax.random.key(seed)
    ks = iter(jax.random.split(key, 40))
    ne = (DEPTH + 1) // 2
    no = DEPTH // 2
    D = D_MODEL

    def nrm(shape, scale):
        return jax.random.normal(next(ks), shape, jnp.float32) * scale

    def gain(shape):
        return 1.0 + nrm(shape, 0.05)

    a8 = jax.random.uniform(next(ks), (ne, 2, LRU_W), jnp.float32, minval=0.9, maxval=0.999)
    a_base = a8 ** (1.0 / LRU_C)
    return {
        'x': nrm((BATCH, SEQ, D), 1.0),
        'c': nrm((BATCH, D), 1.0),
        'ctx': nrm((BATCH, CTX_LEN, D), 1.0),
        'c_ctx': nrm((D,), 1.0),
        'ada_w': nrm((DEPTH, D, 6 * D), 0.5 * D ** -0.5),
        'ada_b': nrm((DEPTH, 6 * D), 0.02),
        'norm_mix': gain((DEPTH, D)),
        'norm_mlp': gain((DEPTH, D)),
        'w_out': nrm((DEPTH, MIX_W, D), MIX_W ** -0.5),
        'mlp_w1': nrm((DEPTH, D, D_FF), D ** -0.5),
        'mlp_w2': nrm((DEPTH, D_FF, D), D_FF ** -0.5),
        'ev_w_in': nrm((ne, D, EV_IN), D ** -0.5),
        'mla_q_norm': gain((ne, MLA_Q_LORA)),
        'mla_w_uq': nrm((ne, MLA_Q_LORA, MLA_HEADS * MLA_QK), MLA_Q_LORA ** -0.5),
        'mla_kv_norm': gain((ne, MLA_KV_LORA)),
        'mla_w_ukv': nrm((ne, MLA_KV_LORA, MLA_HEADS * (MLA_NOPE + MLA_V)), MLA_KV_LORA ** -0.5),
        'mla_q_gain': gain((ne, MLA_QK)),
        'mla_k_gain': gain((ne, MLA_QK)),
        'lru_conv_w': nrm((ne, CONV_W, LRU_W), CONV_W ** -0.5),
        'lru_conv_b': nrm((ne, LRU_W), 0.02),
        'lru_w_a': nrm((ne, 2, LRU_BLOCKS, LRU_BS, LRU_BS), LRU_BS ** -0.5),
        'lru_b_a': nrm((ne, 2, LRU_W), 0.02),
        'lru_w_x': nrm((ne, 2, LRU_BLOCKS, LRU_BS, LRU_BS), LRU_BS ** -0.5),
        'lru_b_x': nrm((ne, 2, LRU_W), 0.02),
        'lru_lam': jnp.log(a_base) - jnp.log1p(-a_base),
        'od_w_in': nrm((no, D, OD_IN), D ** -0.5),
        'gla_w_a': nrm((no, 2, GLA_LR, GLA_HEADS * GLA_DK), GLA_LR ** -0.5),
        'gla_b_a': nrm((no, 2, GLA_HEADS * GLA_DK), 0.02),
        'gla_o_gain': gain((no, GLA_DV)),
        'na_q_gain': gain((no, NA_HD)),
        'na_k_gain': gain((no, NA_HD)),
        'na_rpb': nrm((no, NA_HEADS, 2 * NA_WIN_ROWS - 1, 2 * NA_WIN_COLS - 1), 0.2),
    }


def reference(x, c, ctx, c_ctx, ada_w, ada_b, norm_mix, norm_mlp, w_out, mlp_w1, mlp_w2,
              ev_w_in, mla_q_norm, mla_w_uq, mla_kv_norm, mla_w_ukv, mla_q_gain, mla_k_gain,
              lru_conv_w, lru_conv_b, lru_w_a, lru_b_a, lru_w_x, lru_b_x, lru_lam,
              od_w_in, gla_w_a, gla_b_a, gla_o_gain, na_q_gain, na_k_gain, na_rpb):
    xc = ctx
    for l in range(DEPTH):
        last = l == DEPTH - 1
        j = l // 2
        m = [t[:, None, :] for t in _modulation(c, ada_w[l], ada_b[l])]
        mc = _modulation(c_ctx, ada_w[l], ada_b[l])
        h = _modulate(x, norm_mix[l], m[0], m[1])
        hc = _modulate(xc, norm_mix[l], mc[0], mc[1])
        if l % 2 == 0:
            y, yc = _even_mixer(h, hc, ev_w_in[j], w_out[l], mla_q_norm[j], mla_w_uq[j], mla_kv_norm[j],
                                mla_w_ukv[j], mla_q_gain[j], mla_k_gain[j], lru_conv_w[j], lru_conv_b[j],
                                lru_w_a[j], lru_b_a[j], lru_w_x[j], lru_b_x[j], lru_lam[j], not last)
        else:
            y, yc = _odd_mixer(h, hc, od_w_in[j], w_out[l], gla_w_a[j], gla_b_a[j], gla_o_gain[j],
                               na_q_gain[j], na_k_gain[j], na_rpb[j], not last)
        x = x + m[2] * y
        x = x + m[5] * _sq_relu_mlp(_modulate(x, norm_mlp[l], m[3], m[4]), mlp_w1[l], mlp_w2[l])
        if not last:
            xc = xc + mc[2] * yc
            xc = xc + mc[5] * _sq_relu_mlp(_modulate(xc, norm_mlp[l], mc[3], mc[4]), mlp_w1[l], mlp_w2[l])
    return x
```

```python
import functools

import numpy as np
import jax
import jax.numpy as jnp
from jax import lax
from jax.experimental import pallas as pl
from jax.experimental.pallas import tpu as pltpu

F32 = jnp.float32
BF16 = jnp.bfloat16

D_MODEL = 1024
GRID_W = 64
EPS = 1e-6
ROPE_BASE = 10000.0
D_FF = 4 * D_MODEL

MLA_HEADS = 8
MLA_NOPE = 64
MLA_ROPE = 32
MLA_QK = MLA_NOPE + MLA_ROPE
MLA_V = 64
MLA_Q_LORA = 256
MLA_KV_LORA = 128

LRU_W = 512
LRU_BLOCKS = 8
LRU_BS = LRU_W // LRU_BLOCKS
LRU_C = 8.0
CONV_W = 4
CONV_LEFT = 2

GLA_HEADS = 4
GLA_DK = 64
GLA_DV = 128
GLA_LR = 16
GLA_TAU = 16.0
GLA_CHUNK = 64

NA_HD = 64
NA_HEADS = 8
NA_WIN_ROWS = 8
NA_WIN_COLS = 16

LANE = 128
HEAD_SLOT = 128

EV_UQ, EV_UKV, EV_UKR, EV_UX, EV_UG = 0, 256, 384, 512, 1024
EV_PAD = 1536
OD_GQ, OD_GK, OD_GV, OD_GG, OD_NQ, OD_NK, OD_NV, OD_GLR = 0, 256, 512, 1024, 1536, 2048, 2560, 3072
OD_PAD = 3200


def _cparams(semantics, vmem_mib):
    return pltpu.CompilerParams(dimension_semantics=semantics, vmem_limit_bytes=vmem_mib << 20)


def _rms(x, g):
    return x * lax.rsqrt(jnp.mean(x * x, axis=-1, keepdims=True) + EPS) * g


def _ada_kernel(c_ref, w_ref, b_ref, o_ref):
    cv = c_ref[...]
    s = cv * jax.nn.sigmoid(cv)
    o_ref[0] = jnp.dot(s.astype(BF16), w_ref[0].astype(BF16), preferred_element_type=F32) + b_ref[0]


def _modulation(cc, ada_w, ada_b):
    L, D, D6 = ada_w.shape
    R = cc.shape[0]
    tn = 1536
    return pl.pallas_call(
        _ada_kernel,
        out_shape=jax.ShapeDtypeStruct((L, R, D6), F32),
        grid=(L, D6 // tn),
        in_specs=[pl.BlockSpec((R, D), lambda l, j: (0, 0)),
                  pl.BlockSpec((1, D, tn), lambda l, j: (l, 0, j)),
                  pl.BlockSpec((1, 1, tn), lambda l, j: (l, 0, j))],
        out_specs=pl.BlockSpec((1, R, tn), lambda l, j: (l, 0, j)),
        compiler_params=_cparams(("arbitrary", "arbitrary"), 40),
        name="adaln_modulation",
    )(cc, ada_w, ada_b.reshape(L, 1, D6))


def _mod_spec(row0, k, ngrid):
    if ngrid == 2:
        return pl.BlockSpec((1, 1, D_MODEL), lambda g, i: (row0 + g, 0, k))
    return pl.BlockSpec((1, 1, D_MODEL), lambda g, i, f: (row0 + g, 0, k))


def _nmm_kernel(x_ref, g_ref, sh_ref, sc_ref, w_ref, o_ref):
    h = _rms(x_ref[0], g_ref[...]) * (1.0 + sc_ref[0]) + sh_ref[0]
    o_ref[0] = jnp.dot(h.astype(BF16), w_ref[...], preferred_element_type=F32).astype(o_ref.dtype)


def _norm_mod_matmul(x, g, mods, row0, w, tm=256):
    G, M, D = x.shape
    Nout = w.shape[1]
    return pl.pallas_call(
        _nmm_kernel,
        out_shape=jax.ShapeDtypeStruct((G, M, Nout), BF16),
        grid=(G, M // tm),
        in_specs=[pl.BlockSpec((1, tm, D), lambda b, i: (b, i, 0)),
                  pl.BlockSpec((1, D), lambda b, i: (0, 0)),
                  _mod_spec(row0, 0, 2), _mod_spec(row0, 1, 2),
                  pl.BlockSpec((D, Nout), lambda b, i: (0, 0))],
        out_specs=pl.BlockSpec((1, tm, Nout), lambda b, i: (b, i, 0)),
        compiler_params=_cparams(("parallel", "arbitrary"), 48),
        name="norm_mod_in_proj",
    )(x, g.reshape(1, D), mods, mods, w)


def _out_mlp_kernel(x_ref, ya_ref, yb_ref, woa_ref, wob_ref, g_ref, m2_ref, m3_ref, m4_ref, m5_ref,
                    w1_ref, w2_ref, o_ref, x1_ref, h_ref, acc_ref):
    f = pl.program_id(2)

    @pl.when(f == 0)
    def _():
        y = (jnp.dot(ya_ref[0], woa_ref[...], preferred_element_type=F32)
             + jnp.dot(yb_ref[0], wob_ref[...], preferred_element_type=F32))
        x1 = x_ref[0] + m2_ref[0] * y
        x1_ref[...] = x1
        h_ref[...] = (_rms(x1, g_ref[...]) * (1.0 + m4_ref[0]) + m3_ref[0]).astype(BF16)
        acc_ref[...] = jnp.zeros_like(acc_ref)

    a = jnp.maximum(jnp.dot(h_ref[...], w1_ref[...], preferred_element_type=F32), 0.0)
    acc_ref[...] += jnp.dot((a * a).astype(BF16), w2_ref[...], preferred_element_type=F32)

    @pl.when(f == pl.num_programs(2) - 1)
    def _():
        o_ref[0] = x1_ref[...] + m5_ref[0] * acc_ref[...]


def _out_proj_mlp(x, ya, yb, woa, wob, g_mlp, mods, row0, w1, w2, tm=512, tf=512):
    G, M, D = x.shape
    Wa, Wb = ya.shape[-1], yb.shape[-1]
    FF = w1.shape[1]
    return pl.pallas_call(
        _out_mlp_kernel,
        out_shape=jax.ShapeDtypeStruct((G, M, D), F32),
        grid=(G, M // tm, FF // tf),
        in_specs=[pl.BlockSpec((1, tm, D), lambda b, i, f: (b, i, 0)),
                  pl.BlockSpec((1, tm, Wa), lambda b, i, f: (b, i, 0)),
                  pl.BlockSpec((1, tm, Wb), lambda b, i, f: (b, i, 0)),
                  pl.BlockSpec((Wa, D), lambda b, i, f: (0, 0)),
                  pl.BlockSpec((Wb, D), lambda b, i, f: (0, 0)),
                  pl.BlockSpec((1, D), lambda b, i, f: (0, 0)),
                  _mod_spec(row0, 2, 3), _mod_spec(row0, 3, 3), _mod_spec(row0, 4, 3), _mod_spec(row0, 5, 3),
                  pl.BlockSpec((D, tf), lambda b, i, f: (0, f)),
                  pl.BlockSpec((tf, D), lambda b, i, f: (f, 0))],
        out_specs=pl.BlockSpec((1, tm, D), lambda b, i, f: (b, i, 0)),
        scratch_shapes=[pltpu.VMEM((tm, D), F32), pltpu.VMEM((tm, D), BF16), pltpu.VMEM((tm, D), F32)],
        compiler_params=_cparams(("parallel", "parallel", "arbitrary"), 48),
        name="out_proj_mlp",
    )(x, ya, yb, woa, wob, g_mlp.reshape(1, D), mods, mods, mods, mods, w1, w2)


def _mla_prep_kernel(uq_ref, ukv_ref, ukr_ref, cos_ref, sa_ref, sb_ref, qn_ref, kvn_ref, wuq_ref, wuk_ref,
                     wuv_ref, qg_ref, kg_ref, q_out, k_out, v_out, *, use_rope):
    qn = _rms(uq_ref[0].astype(F32), qn_ref[...]).astype(BF16)
    kvn = _rms(ukv_ref[0].astype(F32), kvn_ref[...]).astype(BF16)
    q_all = jnp.dot(qn, wuq_ref[...], preferred_element_type=F32)
    k_all = jnp.dot(kvn, wuk_ref[...], preferred_element_type=F32)
    v_out[0] = jnp.dot(kvn, wuv_ref[...], preferred_element_type=F32).astype(BF16)
    ukr = ukr_ref[0].astype(F32)
    scale = MLA_QK ** -0.5

    def head_norm(t, gain):
        ms = jnp.sum(t * t, axis=-1, keepdims=True) * (1.0 / MLA_QK)
        return t * lax.rsqrt(ms + EPS) * gain

    def rope(t):
        return (t * cos_ref[...] + pltpu.roll(t, HEAD_SLOT - MLA_ROPE // 4, 1) * sa_ref[...]
                + pltpu.roll(t, MLA_ROPE // 4, 1) * sb_ref[...])

    for h in range(MLA_HEADS):
        sl = slice(h * HEAD_SLOT, (h + 1) * HEAD_SLOT)
        qh = head_norm(q_all[:, sl], qg_ref[...])
        kh = head_norm(k_all[:, sl] + ukr, kg_ref[...])
        if use_rope:
            qh, kh = rope(qh), rope(kh)
        q_out[0, :, sl] = (qh * scale).astype(BF16)
        k_out[0, :, sl] = kh.astype(BF16)


def _rope_tables(n):
    pos = jnp.arange(n)
    quarter = MLA_ROPE // 4
    inv = ROPE_BASE ** (-jnp.arange(0, MLA_ROPE // 2, 2, dtype=F32) / (MLA_ROPE // 2))
    ang_r = (pos // GRID_W).astype(F32)[:, None] * inv[None, :]
    ang_c = (pos % GRID_W).astype(F32)[:, None] * inv[None, :]
    z = jnp.zeros((n, quarter), F32)
    ones_nope = jnp.ones((n, MLA_NOPE), F32)
    pad = jnp.zeros((n, HEAD_SLOT - MLA_QK), F32)
    cos = jnp.concatenate([ones_nope, jnp.cos(ang_r), jnp.cos(ang_r), jnp.cos(ang_c), jnp.cos(ang_c), pad], -1)
    zn = jnp.zeros((n, MLA_NOPE), F32)
    sin_a = jnp.concatenate([zn, -jnp.sin(ang_r), z, -jnp.sin(ang_c), z, pad], -1)
    sin_b = jnp.concatenate([zn, z, jnp.sin(ang_r), z, jnp.sin(ang_c), pad], -1)
    return cos, sin_a, sin_b


def _mla_prep(u, tables, qn, kvn, wuq, wuk, wuv, qg, kg, use_rope, tm=256):
    G, M, _ = u.shape
    HS = MLA_HEADS * HEAD_SLOT
    HV = MLA_HEADS * MLA_V
    full = lambda shape: pl.BlockSpec(shape, lambda b, i: (0,) * len(shape))
    tab = pl.BlockSpec((tm, HEAD_SLOT), lambda b, i: (i, 0))
    return pl.pallas_call(
        functools.partial(_mla_prep_kernel, use_rope=use_rope),
        out_shape=(jax.ShapeDtypeStruct((G, M, HS), BF16), jax.ShapeDtypeStruct((G, M, HS), BF16),
                   jax.ShapeDtypeStruct((G, M, HV), BF16)),
        grid=(G, M // tm),
        in_specs=[pl.BlockSpec((1, tm, MLA_Q_LORA), lambda b, i: (b, i, EV_UQ // MLA_Q_LORA)),
                  pl.BlockSpec((1, tm, MLA_KV_LORA), lambda b, i: (b, i, EV_UKV // MLA_KV_LORA)),
                  pl.BlockSpec((1, tm, HEAD_SLOT), lambda b, i: (b, i, EV_UKR // HEAD_SLOT)),
                  tab, tab, tab,
                  full((1, MLA_Q_LORA)), full((1, MLA_KV_LORA)),
                  full((MLA_Q_LORA, HS)), full((MLA_KV_LORA, HS)), full((MLA_KV_LORA, HV)),
                  full((1, HEAD_SLOT)), full((1, HEAD_SLOT))],
        out_specs=(pl.BlockSpec((1, tm, HS), lambda b, i: (b, i, 0)),
                   pl.BlockSpec((1, tm, HS), lambda b, i: (b, i, 0)),
                   pl.BlockSpec((1, tm, HV), lambda b, i: (b, i, 0))),
        compiler_params=_cparams(("parallel", "arbitrary"), 40),
        name="mla_prep",
    )(u, u, u, *tables, qn, kvn, wuq, wuk, wuv, qg, kg)


def _dot_nt(a, b):
    return lax.dot_general(a, b, (((1,), (1,)), ((), ())), preferred_element_type=F32)


def _mla_attn_kernel(*refs, with_latent):
    if with_latent:
        q_ref, kl_ref, vl_ref, kc_ref, vc_ref, o_ref = refs
    else:
        q_ref, kc_ref, vc_ref, o_ref = refs
    tq = q_ref.shape[1]
    first_half = lax.broadcasted_iota(jnp.int32, (tq, LANE), 1) < MLA_V
    for hp in range(MLA_HEADS // 2):
        vsl = slice(hp * LANE, (hp + 1) * LANE)
        outs = []
        for j in range(2):
            sl = slice((2 * hp + j) * HEAD_SLOT, (2 * hp + j + 1) * HEAD_SLOT)
            qh = q_ref[0, :, sl]
            s_c = _dot_nt(qh, kc_ref[0, :, sl])
            m = jnp.max(s_c, axis=-1, keepdims=True)
            if with_latent:
                s_l = _dot_nt(qh, kl_ref[0, :, sl])
                m = jnp.maximum(m, jnp.max(s_l, axis=-1, keepdims=True))
            p_c = jnp.exp(s_c - m)
            den = jnp.sum(p_c, axis=-1, keepdims=True)
            o = jnp.dot(p_c.astype(BF16), vc_ref[0, :, vsl], preferred_element_type=F32)
            if with_latent:
                p_l = jnp.exp(s_l - m)
                den = den + jnp.sum(p_l, axis=-1, keepdims=True)
                o = o + jnp.dot(p_l.astype(BF16), vl_ref[0, :, vsl], preferred_element_type=F32)
            outs.append(o / den)
        o_ref[0, :, vsl] = jnp.where(first_half, outs[0], outs[1]).astype(BF16)


def _mla_attention(q, kl, vl, kc, vc, tq=256):
    B, M, HS = q.shape
    HV = vc.shape[-1]
    Nc = kc.shape[1]
    with_latent = kl is not None
    whole = lambda n, w: pl.BlockSpec((1, n, w), lambda b, i: (b, 0, 0))
    in_specs = [pl.BlockSpec((1, tq, HS), lambda b, i: (b, i, 0))]
    args = [q]
    if with_latent:
        in_specs += [whole(kl.shape[1], HS), whole(kl.shape[1], HV)]
        args += [kl, vl]
    in_specs += [whole(Nc, HS), whole(Nc, HV)]
    args += [kc, vc]
    return pl.pallas_call(
        functools.partial(_mla_attn_kernel, with_latent=with_latent),
        out_shape=jax.ShapeDtypeStruct((B, M, HV), BF16),
        grid=(B, M // tq),
        in_specs=in_specs,
        out_specs=pl.BlockSpec((1, tq, HV), lambda b, i: (b, i, 0)),
        compiler_params=_cparams(("parallel", "arbitrary"), 48),
        name="mla_attention",
    )(*args)


LRU_CW = 256
LRU_HALO = 16
LRU_TN = 256


def _gelu_tanh(x):
    return 0.5 * x * (1.0 + jnp.tanh(0.7978845608028654 * (x + 0.044715 * (x * x * x))))


def _scan_group(a, bv, h, reverse):
    row = lax.broadcasted_iota(jnp.int32, a.shape, 0)
    for s in (1, 2, 4):
        if reverse:
            keep = row < 8 - s
            shift = 8 - s
        else:
            keep = row >= s
            shift = s
        a_s = jnp.where(keep, pltpu.roll(a, shift, 0), 1.0)
        b_s = jnp.where(keep, pltpu.roll(bv, shift, 0), 0.0)
        bv = a * b_s + bv
        a = a * a_s
    hs = a * h + bv
    return hs, (hs[0:1, :] if reverse else hs[7:8, :])


def _lru_kernel(uxl_ref, ugl_ref, uxc_ref, ugc_ref, cw_ref, cb_ref, wg_ref, bg_ref, lam_ref, yl_ref, yc_ref,
                xpl_ref, xpc_ref, xcv_ref, af_ref, bf_ref, ab_ref, bb_ref):
    N, Nc = uxl_ref.shape[1], uxc_ref.shape[1]
    NT = N + Nc
    C = LRU_CW
    H = LRU_HALO

    def conv(src_ref, pad_ref, n, row0):
        pad_ref[0:H, :] = jnp.zeros((H, C), F32)
        pad_ref[H + n:H + n + H, :] = jnp.zeros((H, C), F32)
        pad_ref[H:H + n, :] = src_ref[0].astype(F32)
        y = cb_ref[...] + pad_ref[H - CONV_LEFT:H - CONV_LEFT + n, :] * cw_ref[0:1, :]
        for j in range(1, CONV_W):
            y = y + pad_ref[H - CONV_LEFT + j:H - CONV_LEFT + j + n, :] * cw_ref[j:j + 1, :]
        xcv_ref[row0:row0 + n, :] = y

    conv(uxc_ref, xpc_ref, Nc, 0)
    conv(uxl_ref, xpl_ref, N, Nc)

    lam = lam_ref[...]
    softplus = jnp.maximum(-lam, 0.0) + jnp.log1p(jnp.exp(-jnp.abs(lam)))

    def coeff_chunk(i, carry):
        r0 = pl.multiple_of(i * LRU_TN, LRU_TN)
        x = xcv_ref[pl.ds(r0, LRU_TN), :]
        g = jnp.dot(x.astype(BF16), wg_ref[0], preferred_element_type=F32) + bg_ref[0]
        for d, (a_ref, b_ref) in enumerate(((af_ref, bf_ref), (ab_ref, bb_ref))):
            r = jax.nn.sigmoid(g[:, (2 * d) * C:(2 * d + 1) * C])
            gi = jax.nn.sigmoid(g[:, (2 * d + 1) * C:(2 * d + 2) * C])
            log_a = (-LRU_C) * r * softplus[d:d + 1, :]
            a_ref[pl.ds(r0, LRU_TN), :] = jnp.exp(log_a)
            b_ref[pl.ds(r0, LRU_TN), :] = jnp.sqrt(1.0 - jnp.exp(2.0 * log_a)) * (gi * x)
        return carry

    lax.fori_loop(0, NT // LRU_TN, coeff_chunk, 0)

    def fwd(g, h):
        r0 = pl.multiple_of(g * 8, 8)
        hs, h = _scan_group(af_ref[pl.ds(r0, 8), :], bf_ref[pl.ds(r0, 8), :], h, False)
        bf_ref[pl.ds(r0, 8), :] = hs
        return h

    def bwd_from(top):
        def body(g, h):
            r0 = pl.multiple_of((top - 1 - g) * 8, 8)
            hs, h = _scan_group(ab_ref[pl.ds(r0, 8), :], bb_ref[pl.ds(r0, 8), :], h, True)
            bb_ref[pl.ds(r0, 8), :] = hs
            return h
        return body

    zero = jnp.zeros((1, C), F32)
    lax.fori_loop(0, NT // 8, fwd, zero)
    hb = lax.fori_loop(0, Nc // 8, bwd_from(Nc // 8), zero)
    lax.fori_loop(0, N // 8, bwd_from(NT // 8), hb)

    def out_chunk(i, carry):
        r0 = pl.multiple_of(i * LRU_TN, LRU_TN)
        hsum = bf_ref[pl.ds(Nc + r0, LRU_TN), :] + bb_ref[pl.ds(Nc + r0, LRU_TN), :]
        gate = _gelu_tanh(ugl_ref[0, pl.ds(r0, LRU_TN), :].astype(F32))
        yl_ref[0, pl.ds(r0, LRU_TN), :] = (hsum * gate).astype(BF16)
        return carry

    lax.fori_loop(0, N // LRU_TN, out_chunk, 0)
    yc_ref[0] = ((bf_ref[0:Nc, :] + bb_ref[0:Nc, :]) * _gelu_tanh(ugc_ref[0].astype(F32))).astype(BF16)


def _rglru(ul, uc, conv_w, conv_b, wg, bg, lam):
    B, N, _ = ul.shape
    Nc = uc.shape[1]
    C = LRU_CW
    nh = LRU_W // C
    NT = N + Nc
    col = lambda base: (lambda b, j: (b, 0, base // C + j))
    par = lambda rows: pl.BlockSpec((rows, C), lambda b, j: (0, j))
    return pl.pallas_call(
        _lru_kernel,
        out_shape=(jax.ShapeDtypeStruct((B, N, LRU_W), BF16), jax.ShapeDtypeStruct((B, Nc, LRU_W), BF16)),
        grid=(B, nh),
        in_specs=[pl.BlockSpec((1, N, C), col(EV_UX)), pl.BlockSpec((1, N, C), col(EV_UG)),
                  pl.BlockSpec((1, Nc, C), col(EV_UX)), pl.BlockSpec((1, Nc, C), col(EV_UG)),
                  par(CONV_W), par(1),
                  pl.BlockSpec((1, C, 4 * C), lambda b, j: (j, 0, 0)),
                  pl.BlockSpec((1, 1, 4 * C), lambda b, j: (j, 0, 0)),
                  par(2)],
        out_specs=(pl.BlockSpec((1, N, C), lambda b, j: (b, 0, j)),
                   pl.BlockSpec((1, Nc, C), lambda b, j: (b, 0, j))),
        scratch_shapes=[pltpu.VMEM((N + 2 * LRU_HALO, C), F32), pltpu.VMEM((Nc + 2 * LRU_HALO, C), F32),
                        pltpu.VMEM((NT, C), F32)] + [pltpu.VMEM((NT, C), F32)] * 4,
        compiler_params=_cparams(("parallel", "arbitrary"), 48),
        name="rglru",
    )(ul, ul, uc, uc, conv_w, conv_b.reshape(1, LRU_W), wg, bg, lam)


def _lru_gate_weights(w_a, b_a, w_x, b_x):
    C = LRU_CW
    nh = LRU_W // C
    kb = C // LRU_BS

    def dense(w):
        w = w.reshape(nh, kb, LRU_BS, LRU_BS)
        eye = jnp.eye(kb, dtype=w.dtype)
        return jnp.einsum('hkij,kl->hkilj', w, eye).reshape(nh, C, C)

    wg = jnp.concatenate([dense(w_a[0]), dense(w_x[0]), dense(w_a[1]), dense(w_x[1])], axis=-1).astype(BF16)
    bg = jnp.stack([b_a[0], b_x[0], b_a[1], b_x[1]], axis=0).reshape(4, nh, C)
    bg = jnp.transpose(bg, (1, 0, 2)).reshape(nh, 1, 4 * C)
    return wg, bg


GLA_SC = 256
GLA_QK_W = GLA_HEADS * GLA_DK
GLA_V_W = GLA_HEADS * GLA_DV


def _split3(x):
    hi = x.astype(BF16)
    r1 = x - hi.astype(F32)
    mid = r1.astype(BF16)
    lo = (r1 - mid.astype(F32)).astype(BF16)
    return hi, mid, lo


def _gla_kernel(ql_ref, kl_ref, vl_ref, gl_ref, lrl_ref, kc_ref, vc_ref, lrc_ref, wa_ref, ba_ref, og_ref,
                y_ref, oacc_ref, st_ref):
    N, Nc = ql_ref.shape[1], kc_ref.shape[1]
    T = GLA_SC
    CH = GLA_CHUNK
    npair = GLA_HEADS // 2
    row = lax.broadcasted_iota(jnp.int32, (T, T), 0)
    colm = lax.broadcasted_iota(jnp.int32, (T, T), 1)
    same_chunk = (row // CH) == (colm // CH)
    causal = (same_chunk & (colm <= row), same_chunk & (colm >= row))
    tri = tuple(jnp.where(c, 1.0, 0.0).astype(BF16) for c in causal)
    lane_lo = lax.broadcasted_iota(jnp.int32, (T, LANE), 1) < GLA_DK
    srow = lax.broadcasted_iota(jnp.int32, (LANE, 2 * GLA_DV), 0)
    scol = lax.broadcasted_iota(jnp.int32, (LANE, 2 * GLA_DV), 1)
    state_mask = (srow < GLA_DK) == (scol < GLA_DV)
    qscale = GLA_DK ** -0.5

    def decay_terms(lr, d):
        z = jnp.dot(lr, wa_ref[d], preferred_element_type=F32) + ba_ref[d]
        log_a = jax.nn.log_sigmoid(z) * (1.0 / GLA_TAU)
        hi, mid, lo = _split3(log_a)
        cum = (jnp.dot(tri[d], hi, preferred_element_type=F32) + jnp.dot(tri[d], mid, preferred_element_type=F32)
               + jnp.dot(tri[d], lo, preferred_element_type=F32))
        last = (CH - 1) if d == 0 else 0
        tot = [cum[c * CH + last:c * CH + last + 1, :] for c in range(T // CH)]
        tot_rows = jnp.concatenate([jnp.broadcast_to(t, (CH, GLA_QK_W)) for t in tot], axis=0)
        return cum, tot, tot_rows

    def state_update(p, c, k_dec, v, tot):
        rs = slice(c * CH, (c + 1) * CH)
        kd = k_dec[rs, p * LANE:(p + 1) * LANE].astype(BF16)
        vv = v[rs, p * 2 * GLA_DV:(p + 1) * 2 * GLA_DV]
        ds = lax.dot_general(kd, vv, (((0,), (0,)), ((), ())), preferred_element_type=F32)
        dec_row = jnp.exp(tot[c][:, p * LANE:(p + 1) * LANE])
        dec_col = jnp.transpose(jnp.broadcast_to(dec_row, (LANE, LANE)))
        dec_col = jnp.concatenate([dec_col, dec_col], axis=1)
        st_ref[p] = dec_col * st_ref[p] + jnp.where(state_mask, ds, 0.0)

    def superchunk(d, q_ref, k_ref, v_ref, lr_ref, r0, with_output):
        lr = lr_ref[0, pl.ds(r0, T), :]
        k = k_ref[0, pl.ds(r0, T), :].astype(F32)
        v = v_ref[0, pl.ds(r0, T), :]
        cum, tot, tot_rows = decay_terms(lr, d)
        k_dec = k * jnp.exp(tot_rows - cum)
        order = range(T // CH) if d == 0 else range(T // CH - 1, -1, -1)
        if not with_output:
            for c in order:
                for p in range(npair):
                    state_update(p, c, k_dec, v, tot)
            return
        q_dec = ((q_ref[0, pl.ds(r0, T), :].astype(F32) * qscale) * jnp.exp(cum)).astype(BF16)
        k_inv = (k * jnp.exp(-cum)).astype(BF16)
        inter = [[None] * (T // CH) for _ in range(npair)]
        for c in order:
            rs = slice(c * CH, (c + 1) * CH)
            for p in range(npair):
                inter[p][c] = jnp.dot(q_dec[rs, p * LANE:(p + 1) * LANE], st_ref[p].astype(BF16),
                                      preferred_element_type=F32)
                state_update(p, c, k_dec, v, tot)
        for p in range(npair):
            o_inter = jnp.concatenate(inter[p], axis=0)
            qp = q_dec[:, p * LANE:(p + 1) * LANE]
            kp = k_inv[:, p * LANE:(p + 1) * LANE]
            for j in range(2):
                h = 2 * p + j
                qm = jnp.where(lane_lo if j == 0 else jnp.logical_not(lane_lo), qp, jnp.zeros_like(qp))
                att = jnp.where(causal[d], _dot_nt(qm, kp), 0.0).astype(BF16)
                o = (jnp.dot(att, v[:, h * GLA_DV:(h + 1) * GLA_DV], preferred_element_type=F32)
                     + o_inter[:, j * GLA_DV:(j + 1) * GLA_DV])
                if d == 0:
                    oacc_ref[pl.ds(r0, T), h * GLA_DV:(h + 1) * GLA_DV] = o
                else:
                    oacc_ref[pl.ds(r0, T), h * GLA_DV:(h + 1) * GLA_DV] += o

    for d in range(2):
        st_ref[...] = jnp.zeros_like(st_ref)
        ncs, nls = Nc // T, N // T

        def ctx_body(i, carry, d=d, ncs=ncs):
            r0 = pl.multiple_of((i if d == 0 else ncs - 1 - i) * T, T)
            superchunk(d, None, kc_ref, vc_ref, lrc_ref, r0, False)
            return carry

        def lat_body(i, carry, d=d, nls=nls):
            r0 = pl.multiple_of((i if d == 0 else nls - 1 - i) * T, T)
            superchunk(d, ql_ref, kl_ref, vl_ref, lrl_ref, r0, True)
            return carry

        lax.fori_loop(0, ncs, ctx_body, 0)
        lax.fori_loop(0, nls, lat_body, 0)

    def fin(i, carry):
        r0 = pl.multiple_of(i * T, T)
        g = gl_ref[0, pl.ds(r0, T), :].astype(F32)
        for h in range(GLA_HEADS):
            sl = slice(h * GLA_DV, (h + 1) * GLA_DV)
            gh = g[:, sl]
            y_ref[0, pl.ds(r0, T), sl] = (_rms(oacc_ref[pl.ds(r0, T), sl], og_ref[...])
                                          * (gh * jax.nn.sigmoid(gh))).astype(BF16)
        return carry

    lax.fori_loop(0, N // T, fin, 0)


def _gla(ul, uc, wa, ba, o_gain):
    B, N, _ = ul.shape
    Nc = uc.shape[1]
    blk = lambda n, w, base: pl.BlockSpec((1, n, w), lambda b: (b, 0, base // w))
    full = lambda shape: pl.BlockSpec(shape, lambda b: (0,) * len(shape))
    return pl.pallas_call(
        _gla_kernel,
        out_shape=jax.ShapeDtypeStruct((B, N, GLA_V_W), BF16),
        grid=(B,),
        in_specs=[blk(N, GLA_QK_W, OD_GQ), blk(N, GLA_QK_W, OD_GK), blk(N, GLA_V_W, OD_GV),
                  blk(N, GLA_V_W, OD_GG), blk(N, LANE, OD_GLR),
                  blk(Nc, GLA_QK_W, OD_GK), blk(Nc, GLA_V_W, OD_GV), blk(Nc, LANE, OD_GLR),
                  full((2, LANE, GLA_QK_W)), full((2, 1, GLA_QK_W)), full((1, GLA_DV))],
        out_specs=pl.BlockSpec((1, N, GLA_V_W), lambda b: (b, 0, 0)),
        scratch_shapes=[pltpu.VMEM((N, GLA_V_W), F32), pltpu.VMEM((GLA_HEADS // 2, LANE, 2 * GLA_DV), F32)],
        compiler_params=_cparams(("parallel",), 48),
        name="gla",
    )(ul, ul, ul, ul, ul, uc, uc, uc, wa, ba, o_gain)


NA_W = NA_HEADS * NA_HD
NA_QROWS = 4
NA_KROWS = 12
NA_TQ = NA_QROWS * GRID_W
NA_TK = NA_KROWS * GRID_W
NA_NEG = -1e30


def _na_norm_kernel(q_ref, k_ref, qg_ref, kg_ref, qo_ref, ko_ref):
    lane_lo = lax.broadcasted_iota(jnp.int32, (q_ref.shape[1], LANE), 1) < NA_HD

    def norm(t_ref, g_ref, o_ref, scale):
        for p in range(NA_W // LANE):
            sl = slice(p * LANE, (p + 1) * LANE)
            t = t_ref[0, :, sl].astype(F32)
            sq = t * t
            s_lo = jnp.sum(jnp.where(lane_lo, sq, 0.0), axis=-1, keepdims=True)
            s_hi = jnp.sum(sq, axis=-1, keepdims=True) - s_lo
            ms = jnp.where(lane_lo, s_lo, s_hi) * (1.0 / NA_HD)
            o_ref[0, :, sl] = (t * lax.rsqrt(ms + EPS) * g_ref[...] * scale).astype(BF16)

    norm(q_ref, qg_ref, qo_ref, NA_HD ** -0.5)
    norm(k_ref, kg_ref, ko_ref, 1.0)


def _na_norm(u, qg2, kg2, tm=256):
    G, M, _ = u.shape
    return pl.pallas_call(
        _na_norm_kernel,
        out_shape=(jax.ShapeDtypeStruct((G, M, NA_W), BF16), jax.ShapeDtypeStruct((G, M, NA_W), BF16)),
        grid=(G, M // tm),
        in_specs=[pl.BlockSpec((1, tm, NA_W), lambda b, i: (b, i, OD_NQ // NA_W)),
                  pl.BlockSpec((1, tm, NA_W), lambda b, i: (b, i, OD_NK // NA_W)),
                  pl.BlockSpec((1, LANE), lambda b, i: (0, 0)), pl.BlockSpec((1, LANE), lambda b, i: (0, 0))],
        out_specs=(pl.BlockSpec((1, tm, NA_W), lambda b, i: (b, i, 0)),
                   pl.BlockSpec((1, tm, NA_W), lambda b, i: (b, i, 0))),
        compiler_params=_cparams(("parallel", "arbitrary"), 32),
        name="na_norm",
    )(u, u, qg2, kg2)


def _na_kernel(q_ref, k0_ref, k1_ref, k2_ref, v0_ref, v1_ref, v2_ref, kc_ref, vc_ref, bias_ref, o_ref):
    lane_lo = lax.broadcasted_iota(jnp.int32, (NA_TQ, LANE), 1) < NA_HD
    k_refs = (k0_ref, k1_ref, k2_ref)
    v_refs = (v0_ref, v1_ref, v2_ref)
    for p in range(NA_HEADS // 2):
        sl = slice(p * LANE, (p + 1) * LANE)
        qp = q_ref[0, :, sl]
        outs = []
        for j in range(2):
            h = 2 * p + j
            qm = jnp.where(lane_lo if j == 0 else jnp.logical_not(lane_lo), qp, jnp.zeros_like(qp))
            s_l = [_dot_nt(qm, k_refs[t][0, :, sl]) + bias_ref[0, h, :, t * NA_TQ:(t + 1) * NA_TQ] for t in range(3)]
            s_c = _dot_nt(qm, kc_ref[0, :, sl])
            m = jnp.max(s_c, axis=-1, keepdims=True)
            for t in range(3):
                m = jnp.maximum(m, jnp.max(s_l[t], axis=-1, keepdims=True))
            p_c = jnp.exp(s_c - m)
            den = jnp.sum(p_c, axis=-1, keepdims=True)
            o = jnp.dot(p_c.astype(BF16), vc_ref[0, :, sl], preferred_element_type=F32)
            for t in range(3):
                p_l = jnp.exp(s_l[t] - m)
                den = den + jnp.sum(p_l, axis=-1, keepdims=True)
                o = o + jnp.dot(p_l.astype(BF16), v_refs[t][0, :, sl], preferred_element_type=F32)
            outs.append(o / den)
        o_ref[0, :, sl] = jnp.where(lane_lo, outs[0], outs[1]).astype(BF16)


def _na_bias_tables(rpb, rows):
    nblk = rows // NA_QROWS
    tabs = []
    for m in (0, 1, nblk - 1):
        kb = min(max(m - 1, 0), nblk - NA_KROWS // NA_QROWS)
        qr = m * NA_QROWS + np.arange(NA_QROWS)[:, None, None, None]
        qc = np.arange(GRID_W)[None, :, None, None]
        kr = kb * NA_QROWS + np.arange(NA_KROWS)[None, None, :, None]
        kc = np.arange(GRID_W)[None, None, None, :]
        r0 = np.clip(qr - NA_WIN_ROWS // 2, 0, rows - NA_WIN_ROWS)
        c0 = np.clip(qc - NA_WIN_COLS // 2, 0, GRID_W - NA_WIN_COLS)
        valid = (kr >= r0) & (kr < r0 + NA_WIN_ROWS) & (kc >= c0) & (kc < c0 + NA_WIN_COLS)
        di = np.clip(kr - qr + NA_WIN_ROWS - 1, 0, 2 * NA_WIN_ROWS - 2)
        dj = np.clip(kc - qc + NA_WIN_COLS - 1, 0, 2 * NA_WIN_COLS - 2)
        shape = (NA_QROWS, GRID_W, NA_KROWS, GRID_W)
        di, dj, valid = (np.broadcast_to(a, shape).reshape(NA_TQ, NA_TK) for a in (di, dj, valid))
        tabs.append(jnp.where(jnp.asarray(valid)[None], rpb[:, di, dj], NA_NEG))
    return jnp.stack(tabs, axis=0)


def _natten(q, k, v_src, kc, uc, bias):
    B, N, _ = q.shape
    Nc = kc.shape[1]
    nblk = N // NA_TQ
    kmax = nblk - NA_KROWS // NA_QROWS
    vcol = OD_NV // NA_W

    def kspec(t):
        return pl.BlockSpec((1, NA_TQ, NA_W), lambda m, b: (b, jnp.clip(m - 1, 0, kmax) + t, 0))

    def vspec(t):
        return pl.BlockSpec((1, NA_TQ, NA_W), lambda m, b: (b, jnp.clip(m - 1, 0, kmax) + t, vcol))

    return pl.pallas_call(
        _na_kernel,
        out_shape=jax.ShapeDtypeStruct((B, N, NA_W), BF16),
        grid=(nblk, B),
        in_specs=[pl.BlockSpec((1, NA_TQ, NA_W), lambda m, b: (b, m, 0)),
                  kspec(0), kspec(1), kspec(2), vspec(0), vspec(1), vspec(2),
                  pl.BlockSpec((1, Nc, NA_W), lambda m, b: (b, 0, 0)),
                  pl.BlockSpec((1, Nc, NA_W), lambda m, b: (b, 0, vcol)),
                  pl.BlockSpec((1, NA_HEADS, NA_TQ, NA_TK),
                               lambda m, b: (jnp.where(m == 0, 0, jnp.where(m == nblk - 1, 2, 1)), 0, 0, 0))],
        out_specs=pl.BlockSpec((1, NA_TQ, NA_W), lambda m, b: (b, m, 0)),
        compiler_params=_cparams(("arbitrary", "arbitrary"), 48),
        name="natten",
    )(q, k, k, k, v_src, v_src, v_src, kc, uc, bias)


def _place(w, layout, total):
    out = jnp.zeros((w.shape[0], total), w.dtype)
    for src, width, dst in layout:
        out = out.at[:, dst:dst + width].set(w[:, src:src + width])
    return out


def _ev_in_weight(w):
    lay = [(0, 256, EV_UQ), (256, 128, EV_UKV), (384, MLA_ROPE, EV_UKR + MLA_NOPE),
           (416, LRU_W, EV_UX), (416 + LRU_W, LRU_W, EV_UG)]
    return _place(w, lay, EV_PAD).astype(BF16)


def _od_in_weight(w):
    src = np.cumsum([0, 256, 256, 512, 512, 2 * GLA_LR, 512, 512, 512])
    dst = [OD_GQ, OD_GK, OD_GV, OD_GG, OD_GLR, OD_NQ, OD_NK, OD_NV]
    lay = [(int(src[i]), int(src[i + 1] - src[i]), dst[i]) for i in range(8)]
    return _place(w, lay, OD_PAD).astype(BF16)


def _pad_heads(w, heads, width):
    k = w.shape[0]
    return jnp.pad(w.reshape(k, heads, width), ((0, 0), (0, 0), (0, HEAD_SLOT - width))).reshape(k, heads * HEAD_SLOT)


def _pad_lanes(v, width=HEAD_SLOT):
    return jnp.pad(v, (0, width - v.shape[0])).reshape(1, width)


def kernel(x, c, ctx, c_ctx, ada_w, ada_b, norm_mix, norm_mlp, w_out, mlp_w1, mlp_w2,
           ev_w_in, mla_q_norm, mla_w_uq, mla_kv_norm, mla_w_ukv, mla_q_gain, mla_k_gain,
           lru_conv_w, lru_conv_b, lru_w_a, lru_b_a, lru_w_x, lru_b_x, lru_lam,
           od_w_in, gla_w_a, gla_b_a, gla_o_gain, na_q_gain, na_k_gain, na_rpb):
    B, N, D = x.shape
    Nc = ctx.shape[1]
    depth = ada_w.shape[0]

    R = -(-(B + 1) // 8) * 8
    cc = jnp.concatenate([c, c_ctx[None], jnp.zeros((R - B - 1, D), c.dtype)], axis=0)
    mods = _modulation(cc, ada_w, ada_b).reshape(depth * R, 1, 6 * D)

    xl = x
    xc = ctx.reshape(1, B * Nc, D)
    for l in range(depth):
        last = l == depth - 1
        j = l // 2
        row_l, row_c = l * R, l * R + B
        w1 = mlp_w1[l].astype(BF16)
        w2 = mlp_w2[l].astype(BF16)
        wo = w_out[l].astype(BF16)
        if l % 2 == 0:
            w_in = _ev_in_weight(ev_w_in[j])
            ul = _norm_mod_matmul(xl, norm_mix[l], mods, row_l, w_in)
            uc = _norm_mod_matmul(xc, norm_mix[l], mods, row_c, w_in).reshape(B, Nc, EV_PAD)
            wuq = _pad_heads(mla_w_uq[j], MLA_HEADS, MLA_QK).astype(BF16)
            wukv = mla_w_ukv[j].reshape(MLA_KV_LORA, MLA_HEADS, MLA_NOPE + MLA_V)
            wuk = _pad_heads(wukv[:, :, :MLA_NOPE].reshape(MLA_KV_LORA, -1), MLA_HEADS, MLA_NOPE).astype(BF16)
            wuv = wukv[:, :, MLA_NOPE:].reshape(MLA_KV_LORA, MLA_HEADS * MLA_V).astype(BF16)
            prep = functools.partial(_mla_prep, qn=mla_q_norm[j].reshape(1, -1), kvn=mla_kv_norm[j].reshape(1, -1),
                                     wuq=wuq, wuk=wuk, wuv=wuv, qg=_pad_lanes(mla_q_gain[j]),
                                     kg=_pad_lanes(mla_k_gain[j]))
            ql, kl, vl = prep(ul, _rope_tables(N), use_rope=True)
            qc, kc, vc = prep(uc, _rope_tables(Nc), use_rope=False)
            ya_l = _mla_attention(ql, kl, vl, kc, vc)
            ya_c = None if last else _mla_attention(qc, None, None, kc, vc)
            wg, bg = _lru_gate_weights(lru_w_a[j], lru_b_a[j], lru_w_x[j], lru_b_x[j])
            yb_l, yb_c = _rglru(ul, uc, lru_conv_w[j], lru_conv_b[j], wg, bg, lru_lam[j])
            wa_w = MLA_HEADS * MLA_V
        else:
            w_in = _od_in_weight(od_w_in[j])
            ul = _norm_mod_matmul(xl, norm_mix[l], mods, row_l, w_in)
            uc = _norm_mod_matmul(xc, norm_mix[l], mods, row_c, w_in).reshape(B, Nc, OD_PAD)
            if not last:
                raise NotImplementedError("context outputs of the odd-layer mixers are only needed when depth > 2")
            wa = jnp.zeros((2, LANE, GLA_QK_W), F32)
            for d in range(2):
                wa = wa.at[d, d * GLA_LR:(d + 1) * GLA_LR].set(gla_w_a[j, d])
            ya_l = _gla(ul, uc, wa.astype(BF16), gla_b_a[j].reshape(2, 1, GLA_QK_W), gla_o_gain[j].reshape(1, GLA_DV))
            ya_c = None
            qg2 = jnp.tile(na_q_gain[j], 2).reshape(1, LANE)
            kg2 = jnp.tile(na_k_gain[j], 2).reshape(1, LANE)
            nq, nk = _na_norm(ul, qg2, kg2)
            _, nkc = _na_norm(uc, qg2, kg2)
            yb_l = _natten(nq, nk, ul, nkc, uc, _na_bias_tables(na_rpb[j], N // GRID_W))
            yb_c = None
            wa_w = GLA_V_W
        xl = _out_proj_mlp(xl, ya_l, yb_l, wo[:wa_w], wo[wa_w:], norm_mlp[l], mods, row_l, w1, w2)
        if not last:
            xc = _out_proj_mlp(xc, ya_c.reshape(1, B * Nc, -1), yb_c.reshape(1, B * Nc, -1), wo[:wa_w], wo[wa_w:],
                               norm_mlp[l], mods, row_c, w1, w2)
    return xl
```

```python
import functools

import numpy as np
import jax
import jax.numpy as jnp
from jax import lax
from jax.experimental import pallas as pl
from jax.experimental.pallas import tpu as pltpu

F32 = jnp.float32
BF16 = jnp.bfloat16

D_MODEL = 1024
GRID_W = 64
EPS = 1e-6
ROPE_BASE = 10000.0
D_FF = 4 * D_MODEL

MLA_HEADS = 8
MLA_NOPE = 64
MLA_ROPE = 32
MLA_QK = MLA_NOPE + MLA_ROPE
MLA_V = 64
MLA_Q_LORA = 256
MLA_KV_LORA = 128

LRU_W = 512
LRU_BLOCKS = 8
LRU_BS = LRU_W // LRU_BLOCKS
LRU_C = 8.0
CONV_W = 4
CONV_LEFT = 2

GLA_HEADS = 4
GLA_DK = 64
GLA_DV = 128
GLA_LR = 16
GLA_TAU = 16.0
GLA_CHUNK = 64

NA_HD = 64
NA_HEADS = 8
NA_WIN_ROWS = 8
NA_WIN_COLS = 16

LANE = 128
HEAD_SLOT = 128

EV_UQ, EV_UKV, EV_UKR, EV_UX, EV_UG = 0, 256, 384, 512, 1024
EV_PAD = 1536
OD_GQ, OD_GK, OD_GV, OD_GG, OD_NQ, OD_NK, OD_NV, OD_GLR = 0, 256, 512, 1024, 1536, 2048, 2560, 3072
OD_PAD = 3200


def _cparams(semantics, vmem_mib):
    return pltpu.CompilerParams(dimension_semantics=semantics, vmem_limit_bytes=vmem_mib << 20)


def _rms(x, g):
    return x * lax.rsqrt(jnp.mean(x * x, axis=-1, keepdims=True) + EPS) * g


def _ada_kernel(c_ref, w_ref, b_ref, o_ref):
    cv = c_ref[...]
    s = cv * jax.nn.sigmoid(cv)
    o_ref[0] = jnp.dot(s.astype(BF16), w_ref[0].astype(BF16), preferred_element_type=F32) + b_ref[0]


def _modulation(cc, ada_w, ada_b):
    L, D, D6 = ada_w.shape
    R = cc.shape[0]
    tn = 1536
    return pl.pallas_call(
        _ada_kernel,
        out_shape=jax.ShapeDtypeStruct((L, R, D6), F32),
        grid=(L, D6 // tn),
        in_specs=[pl.BlockSpec((R, D), lambda l, j: (0, 0)),
                  pl.BlockSpec((1, D, tn), lambda l, j: (l, 0, j)),
                  pl.BlockSpec((1, 1, tn), lambda l, j: (l, 0, j))],
        out_specs=pl.BlockSpec((1, R, tn), lambda l, j: (l, 0, j)),
        compiler_params=_cparams(("arbitrary", "arbitrary"), 40),
        name="adaln_modulation",
    )(cc, ada_w, ada_b.reshape(L, 1, D6))


def _mod_spec(row0, k):
    return pl.BlockSpec((1, 1, D_MODEL), lambda g, i: (row0 + g, 0, k))


def _nmm_kernel(x_ref, g_ref, sh_ref, sc_ref, w_ref, o_ref):
    h = _rms(x_ref[0], g_ref[...]) * (1.0 + sc_ref[0]) + sh_ref[0]
    o_ref[0] = jnp.dot(h.astype(BF16), w_ref[...], preferred_element_type=F32).astype(o_ref.dtype)


def _norm_mod_matmul(x, g, mods, row0, w, tm=512):
    G, M, D = x.shape
    assert M % tm == 0
    Nout = w.shape[1]
    return pl.pallas_call(
        _nmm_kernel,
        out_shape=jax.ShapeDtypeStruct((G, M, Nout), BF16),
        grid=(G, M // tm),
        in_specs=[pl.BlockSpec((1, tm, D), lambda b, i: (b, i, 0)),
                  pl.BlockSpec((1, D), lambda b, i: (0, 0)),
                  _mod_spec(row0, 0), _mod_spec(row0, 1),
                  pl.BlockSpec((D, Nout), lambda b, i: (0, 0))],
        out_specs=pl.BlockSpec((1, tm, Nout), lambda b, i: (b, i, 0)),
        compiler_params=_cparams(("parallel", "arbitrary"), 48),
        name="norm_mod_in_proj",
    )(x, g.reshape(1, D), mods, mods, w)


MLP_TF = 1024


def _out_mlp_kernel(x_ref, ya_ref, yb_ref, woa_ref, wob_ref, g_ref, m2_ref, m3_ref, m4_ref, m5_ref,
                    w1_ref, w2_ref, o_ref, h_ref, a_ref):
    y = (jnp.dot(ya_ref[0], woa_ref[...], preferred_element_type=F32)
         + jnp.dot(yb_ref[0], wob_ref[...], preferred_element_type=F32))
    x1 = x_ref[0] + m2_ref[0] * y
    o_ref[0] = x1
    h_ref[...] = (_rms(x1, g_ref[...]) * (1.0 + m4_ref[0]) + m3_ref[0]).astype(BF16)
    for f in range(a_ref.shape[1] // MLP_TF):
        cols = slice(f * MLP_TF, (f + 1) * MLP_TF)
        a = jnp.maximum(jnp.dot(h_ref[...], w1_ref[:, cols], preferred_element_type=F32), 0.0)
        a_ref[:, cols] = (a * a).astype(BF16)
    o_ref[0] += m5_ref[0] * jnp.dot(a_ref[...], w2_ref[...], preferred_element_type=F32)


def _out_proj_mlp(x, ya, yb, woa, wob, g_mlp, mods, row0, w1, w2, tm=512):
    G, M, D = x.shape
    assert M % tm == 0
    Wa, Wb = ya.shape[-1], yb.shape[-1]
    FF = w1.shape[1]
    resident = lambda shape: pl.BlockSpec(shape, lambda b, i: (0, 0), pipeline_mode=pl.Buffered(1))
    return pl.pallas_call(
        _out_mlp_kernel,
        out_shape=jax.ShapeDtypeStruct((G, M, D), F32),
        grid=(G, M // tm),
        in_specs=[pl.BlockSpec((1, tm, D), lambda b, i: (b, i, 0)),
                  pl.BlockSpec((1, tm, Wa), lambda b, i: (b, i, 0)),
                  pl.BlockSpec((1, tm, Wb), lambda b, i: (b, i, 0)),
                  resident((Wa, D)), resident((Wb, D)),
                  pl.BlockSpec((1, D), lambda b, i: (0, 0)),
                  _mod_spec(row0, 2), _mod_spec(row0, 3), _mod_spec(row0, 4), _mod_spec(row0, 5),
                  resident((D, FF)), resident((FF, D))],
        out_specs=pl.BlockSpec((1, tm, D), lambda b, i: (b, i, 0)),
        scratch_shapes=[pltpu.VMEM((tm, D), BF16), pltpu.VMEM((tm, FF), BF16)],
        compiler_params=_cparams(("parallel", "arbitrary"), 52),
        name="out_proj_mlp",
    )(x, ya, yb, woa, wob, g_mlp.reshape(1, D), mods, mods, mods, mods, w1, w2)


def _mla_prep_kernel(uq_ref, ukv_ref, ukr_ref, tq1_ref, tq2_ref, tk1_ref, tka_ref, tkb_ref, qn_ref, kvn_ref,
                     wuq_ref, wuqr_ref, wuk_ref, wuv_ref, q_out, k_out, v_out):
    qn = _rms(uq_ref[0].astype(F32), qn_ref[...]).astype(BF16)
    kvn = _rms(ukv_ref[0].astype(F32), kvn_ref[...]).astype(BF16)
    q_all = jnp.dot(qn, wuq_ref[...], preferred_element_type=F32)
    q_rot = jnp.dot(qn, wuqr_ref[...], preferred_element_type=F32)
    k_all = jnp.dot(kvn, wuk_ref[...], preferred_element_type=F32)
    v_out[0] = jnp.dot(kvn, wuv_ref[...], preferred_element_type=F32).astype(BF16)
    ukr = ukr_ref[0].astype(F32)
    quarter = MLA_ROPE // 4
    k_rot = (pltpu.roll(ukr, HEAD_SLOT - quarter, 1) * tka_ref[...] + pltpu.roll(ukr, quarter, 1) * tkb_ref[...])
    ukr_sq = jnp.sum(ukr * ukr, axis=-1, keepdims=True)
    inv_n = 1.0 / MLA_QK
    for h in range(MLA_HEADS):
        sl = slice(h * HEAD_SLOT, (h + 1) * HEAD_SLOT)
        qh = q_all[:, sl]
        rq = lax.rsqrt(jnp.sum(qh * qh, axis=-1, keepdims=True) * inv_n + EPS)
        q_out[0, :, sl] = (rq * (qh * tq1_ref[...] + q_rot[:, sl] * tq2_ref[...])).astype(BF16)
        kn = k_all[:, sl]
        rk = lax.rsqrt((jnp.sum(kn * kn, axis=-1, keepdims=True) + ukr_sq) * inv_n + EPS)
        k_out[0, :, sl] = (rk * ((kn + ukr) * tk1_ref[...] + k_rot)).astype(BF16)


def _rope_partner(t):
    quarter = MLA_ROPE // 4
    t4 = t.reshape(t.shape[:-1] + (2, 2, quarter))
    return jnp.stack([t4[..., 1, :], t4[..., 0, :]], axis=-2).reshape(t.shape)


def _rope_tables(n, q_gain, k_gain, use_rope):
    quarter = MLA_ROPE // 4
    if use_rope:
        pos = jnp.arange(n)
        inv = ROPE_BASE ** (-jnp.arange(0, MLA_ROPE // 2, 2, dtype=F32) / (MLA_ROPE // 2))
        ang_r = (pos // GRID_W).astype(F32)[:, None] * inv[None, :]
        ang_c = (pos % GRID_W).astype(F32)[:, None] * inv[None, :]
        ang = jnp.concatenate([ang_r, ang_r, ang_c, ang_c], axis=-1)
        cos, sin = jnp.cos(ang), jnp.sin(ang)
    else:
        cos, sin = jnp.ones((n, MLA_ROPE), F32), jnp.zeros((n, MLA_ROPE), F32)
    first = (np.arange(MLA_ROPE) % (2 * quarter)) < quarter
    pad = jnp.zeros((n, HEAD_SLOT - MLA_QK), F32)

    def slot(nope, rope):
        return jnp.concatenate([jnp.broadcast_to(nope, (n, MLA_NOPE)), rope, pad], axis=-1)

    zero = jnp.zeros((MLA_NOPE,), F32)
    qg, kg = q_gain * MLA_QK ** -0.5, k_gain
    tq1 = slot(qg[:MLA_NOPE], cos * qg[MLA_NOPE:])
    tq2 = slot(zero, sin * _rope_partner(qg[MLA_NOPE:]))
    tk1 = slot(kg[:MLA_NOPE], cos * kg[MLA_NOPE:])
    ksin = sin * _rope_partner(kg[MLA_NOPE:])
    tka = slot(zero, jnp.where(first, -ksin, 0.0))
    tkb = slot(zero, jnp.where(first, 0.0, ksin))
    return tq1, tq2, tk1, tka, tkb


def _rope_partner_weight(wuq):
    k = wuq.shape[0]
    quarter = MLA_ROPE // 4
    w = wuq.reshape(k, MLA_HEADS, HEAD_SLOT)
    rope = w[:, :, MLA_NOPE:MLA_QK].reshape(k, MLA_HEADS, 2, 2, quarter)
    rot = jnp.stack([-rope[:, :, :, 1, :], rope[:, :, :, 0, :]], axis=3).reshape(k, MLA_HEADS, MLA_ROPE)
    out = jnp.concatenate([jnp.zeros_like(w[:, :, :MLA_NOPE]), rot, jnp.zeros_like(w[:, :, MLA_QK:])], axis=-1)
    return out.reshape(k, MLA_HEADS * HEAD_SLOT)


def _mla_prep(u, tables, qn, kvn, wuq, wuqr, wuk, wuv, tm=256):
    G, M, _ = u.shape
    assert M % tm == 0
    HS = MLA_HEADS * HEAD_SLOT
    HV = MLA_HEADS * MLA_V
    full = lambda shape: pl.BlockSpec(shape, lambda b, i: (0,) * len(shape))
    tab = pl.BlockSpec((tm, HEAD_SLOT), lambda b, i: (i, 0))
    return pl.pallas_call(
        _mla_prep_kernel,
        out_shape=(jax.ShapeDtypeStruct((G, M, HS), BF16), jax.ShapeDtypeStruct((G, M, HS), BF16),
                   jax.ShapeDtypeStruct((G, M, HV), BF16)),
        grid=(G, M // tm),
        in_specs=[pl.BlockSpec((1, tm, MLA_Q_LORA), lambda b, i: (b, i, EV_UQ // MLA_Q_LORA)),
                  pl.BlockSpec((1, tm, MLA_KV_LORA), lambda b, i: (b, i, EV_UKV // MLA_KV_LORA)),
                  pl.BlockSpec((1, tm, HEAD_SLOT), lambda b, i: (b, i, EV_UKR // HEAD_SLOT)),
                  tab, tab, tab, tab, tab,
                  full((1, MLA_Q_LORA)), full((1, MLA_KV_LORA)),
                  full((MLA_Q_LORA, HS)), full((MLA_Q_LORA, HS)), full((MLA_KV_LORA, HS)), full((MLA_KV_LORA, HV))],
        out_specs=(pl.BlockSpec((1, tm, HS), lambda b, i: (b, i, 0)),
                   pl.BlockSpec((1, tm, HS), lambda b, i: (b, i, 0)),
                   pl.BlockSpec((1, tm, HV), lambda b, i: (b, i, 0))),
        compiler_params=_cparams(("parallel", "arbitrary"), 40),
        name="mla_prep",
    )(u, u, u, *tables, qn, kvn, wuq, wuqr, wuk, wuv)


def _dot_nt(a, b):
    return lax.dot_general(a, b, (((1,), (1,)), ((), ())), preferred_element_type=F32)


def _mla_attn_kernel(*refs, with_latent):
    if with_latent:
        q_ref, kl_ref, vl_ref, kc_ref, vc_ref, o_ref = refs
    else:
        q_ref, kc_ref, vc_ref, o_ref = refs
    tq = q_ref.shape[1]
    first_half = lax.broadcasted_iota(jnp.int32, (tq, LANE), 1) < MLA_V
    for hp in range(MLA_HEADS // 2):
        vsl = slice(hp * LANE, (hp + 1) * LANE)
        outs = []
        for j in range(2):
            sl = slice((2 * hp + j) * HEAD_SLOT, (2 * hp + j + 1) * HEAD_SLOT)
            qh = q_ref[0, :, sl]
            s_c = _dot_nt(qh, kc_ref[0, :, sl])
            m = jnp.max(s_c, axis=-1, keepdims=True)
            if with_latent:
                s_l = _dot_nt(qh, kl_ref[0, :, sl])
                m = jnp.maximum(m, jnp.max(s_l, axis=-1, keepdims=True))
            p_c = jnp.exp(s_c - m)
            den = jnp.sum(p_c, axis=-1, keepdims=True)
            o = jnp.dot(p_c.astype(BF16), vc_ref[0, :, vsl], preferred_element_type=F32)
            if with_latent:
                p_l = jnp.exp(s_l - m)
                den = den + jnp.sum(p_l, axis=-1, keepdims=True)
                o = o + jnp.dot(p_l.astype(BF16), vl_ref[0, :, vsl], preferred_element_type=F32)
            outs.append(o / den)
        o_ref[0, :, vsl] = jnp.where(first_half, outs[0], outs[1]).astype(BF16)


def _mla_attention(q, kl, vl, kc, vc, tq=256):
    B, M, HS = q.shape
    assert M % tq == 0
    HV = vc.shape[-1]
    Nc = kc.shape[1]
    with_latent = kl is not None
    whole = lambda n, w: pl.BlockSpec((1, n, w), lambda b, i: (b, 0, 0))
    in_specs = [pl.BlockSpec((1, tq, HS), lambda b, i: (b, i, 0))]
    args = [q]
    if with_latent:
        in_specs += [whole(kl.shape[1], HS), whole(kl.shape[1], HV)]
        args += [kl, vl]
    in_specs += [whole(Nc, HS), whole(Nc, HV)]
    args += [kc, vc]
    return pl.pallas_call(
        functools.partial(_mla_attn_kernel, with_latent=with_latent),
        out_shape=jax.ShapeDtypeStruct((B, M, HV), BF16),
        grid=(B, M // tq),
        in_specs=in_specs,
        out_specs=pl.BlockSpec((1, tq, HV), lambda b, i: (b, i, 0)),
        compiler_params=_cparams(("parallel", "arbitrary"), 48),
        name="mla_attention",
    )(*args)


LRU_CW = 256
LRU_HALO = 16
LRU_TN = 256


def _sigmoid(x):
    return 0.5 * jnp.tanh(0.5 * x) + 0.5


def _gelu_tanh(x):
    return 0.5 * x * (1.0 + jnp.tanh(0.7978845608028654 * (x + 0.044715 * (x * x * x))))


def _scan_group(a, bv, h, reverse):
    row = lax.broadcasted_iota(jnp.int32, a.shape, 0)
    for s in (1, 2, 4):
        if reverse:
            keep = row < 8 - s
            shift = 8 - s
        else:
            keep = row >= s
            shift = s
        a_s = jnp.where(keep, pltpu.roll(a, shift, 0), 1.0)
        b_s = jnp.where(keep, pltpu.roll(bv, shift, 0), 0.0)
        bv = a * b_s + bv
        a = a * a_s
    hs = a * h + bv
    return hs, (hs[0:1, :] if reverse else hs[7:8, :])


def _lru_kernel(uxl_ref, ugl_ref, uxc_ref, ugc_ref, cw_ref, cb_ref, wg_ref, bg_ref, lam_ref, yl_ref, yc_ref,
                xpl_ref, xpc_ref, xcv_ref, af_ref, bf_ref, ab_ref, bb_ref):
    N, Nc = uxl_ref.shape[1], uxc_ref.shape[1]
    NT = N + Nc
    C = LRU_CW
    H = LRU_HALO

    def conv(src_ref, pad_ref, n, row0):
        pad_ref[0:H, :] = jnp.zeros((H, C), F32)
        pad_ref[H + n:H + n + H, :] = jnp.zeros((H, C), F32)
        pad_ref[H:H + n, :] = src_ref[0].astype(F32)
        y = cb_ref[...] + pad_ref[H - CONV_LEFT:H - CONV_LEFT + n, :] * cw_ref[0:1, :]
        for j in range(1, CONV_W):
            y = y + pad_ref[H - CONV_LEFT + j:H - CONV_LEFT + j + n, :] * cw_ref[j:j + 1, :]
        xcv_ref[row0:row0 + n, :] = y

    conv(uxc_ref, xpc_ref, Nc, 0)
    conv(uxl_ref, xpl_ref, N, Nc)

    lam = lam_ref[...]
    softplus = jnp.maximum(-lam, 0.0) + jnp.log1p(jnp.exp(-jnp.abs(lam)))

    def coeff_chunk(i, carry):
        r0 = pl.multiple_of(i * LRU_TN, LRU_TN)
        x = xcv_ref[pl.ds(r0, LRU_TN), :]
        g = jnp.dot(x.astype(BF16), wg_ref[0], preferred_element_type=F32) + bg_ref[0]
        for d, (a_ref, b_ref) in enumerate(((af_ref, bf_ref), (ab_ref, bb_ref))):
            r = _sigmoid(g[:, (2 * d) * C:(2 * d + 1) * C])
            gi = _sigmoid(g[:, (2 * d + 1) * C:(2 * d + 2) * C])
            a = jnp.exp((-LRU_C) * r * softplus[d:d + 1, :])
            a_ref[pl.ds(r0, LRU_TN), :] = a
            b_ref[pl.ds(r0, LRU_TN), :] = jnp.sqrt(1.0 - a * a) * (gi * x)
        return carry

    lax.fori_loop(0, NT // LRU_TN, coeff_chunk, 0)

    ngc, ngt = Nc // 8, NT // 8

    def scan_step(i, carry):
        hf, hb = carry
        rf = pl.multiple_of(i * 8, 8)
        rb = pl.multiple_of(jnp.where(i < ngc, ngc - 1 - i, ngt + ngc - 1 - i) * 8, 8)
        hs_f, hf = _scan_group(af_ref[pl.ds(rf, 8), :], bf_ref[pl.ds(rf, 8), :], hf, False)
        hs_b, hb = _scan_group(ab_ref[pl.ds(rb, 8), :], bb_ref[pl.ds(rb, 8), :], hb, True)
        bf_ref[pl.ds(rf, 8), :] = hs_f
        bb_ref[pl.ds(rb, 8), :] = hs_b
        return hf, hb

    zero = jnp.zeros((1, C), F32)
    lax.fori_loop(0, ngt, scan_step, (zero, zero), unroll=4)

    def out_chunk(i, carry):
        r0 = pl.multiple_of(i * LRU_TN, LRU_TN)
        hsum = bf_ref[pl.ds(Nc + r0, LRU_TN), :] + bb_ref[pl.ds(Nc + r0, LRU_TN), :]
        gate = _gelu_tanh(ugl_ref[0, pl.ds(r0, LRU_TN), :].astype(F32))
        yl_ref[0, pl.ds(r0, LRU_TN), :] = (hsum * gate).astype(BF16)
        return carry

    lax.fori_loop(0, N // LRU_TN, out_chunk, 0)
    yc_ref[0] = ((bf_ref[0:Nc, :] + bb_ref[0:Nc, :]) * _gelu_tanh(ugc_ref[0].astype(F32))).astype(BF16)


def _rglru(ul, uc, conv_w, conv_b, wg, bg, lam):
    B, N, _ = ul.shape
    Nc = uc.shape[1]
    C = LRU_CW
    nh = LRU_W // C
    NT = N + Nc
    col = lambda base: (lambda b, j: (b, 0, base // C + j))
    par = lambda rows: pl.BlockSpec((rows, C), lambda b, j: (0, j))
    return pl.pallas_call(
        _lru_kernel,
        out_shape=(jax.ShapeDtypeStruct((B, N, LRU_W), BF16), jax.ShapeDtypeStruct((B, Nc, LRU_W), BF16)),
        grid=(B, nh),
        in_specs=[pl.BlockSpec((1, N, C), col(EV_UX)), pl.BlockSpec((1, N, C), col(EV_UG)),
                  pl.BlockSpec((1, Nc, C), col(EV_UX)), pl.BlockSpec((1, Nc, C), col(EV_UG)),
                  par(CONV_W), par(1),
                  pl.BlockSpec((1, C, 4 * C), lambda b, j: (j, 0, 0)),
                  pl.BlockSpec((1, 1, 4 * C), lambda b, j: (j, 0, 0)),
                  par(2)],
        out_specs=(pl.BlockSpec((1, N, C), lambda b, j: (b, 0, j)),
                   pl.BlockSpec((1, Nc, C), lambda b, j: (b, 0, j))),
        scratch_shapes=[pltpu.VMEM((N + 2 * LRU_HALO, C), F32), pltpu.VMEM((Nc + 2 * LRU_HALO, C), F32),
                        pltpu.VMEM((NT, C), F32)] + [pltpu.VMEM((NT, C), F32)] * 4,
        compiler_params=_cparams(("parallel", "arbitrary"), 48),
        name="rglru",
    )(ul, ul, uc, uc, conv_w, conv_b.reshape(1, LRU_W), wg, bg, lam)


def _lru_gate_weights(w_a, b_a, w_x, b_x):
    C = LRU_CW
    nh = LRU_W // C
    kb = C // LRU_BS

    def dense(w):
        w = w.reshape(nh, kb, LRU_BS, LRU_BS)
        eye = jnp.eye(kb, dtype=w.dtype)
        return jnp.einsum('hkij,kl->hkilj', w, eye).reshape(nh, C, C)

    wg = jnp.concatenate([dense(w_a[0]), dense(w_x[0]), dense(w_a[1]), dense(w_x[1])], axis=-1).astype(BF16)
    bg = jnp.stack([b_a[0], b_x[0], b_a[1], b_x[1]], axis=0).reshape(4, nh, C)
    bg = jnp.transpose(bg, (1, 0, 2)).reshape(nh, 1, 4 * C)
    return wg, bg


GLA_SC = 256
GLA_QK_W = GLA_HEADS * GLA_DK
GLA_V_W = GLA_HEADS * GLA_DV


def _split3(x):
    hi = x.astype(BF16)
    r1 = x - hi.astype(F32)
    mid = r1.astype(BF16)
    lo = (r1 - mid.astype(F32)).astype(BF16)
    return hi, mid, lo


def _gla_kernel(ql_ref, kl_ref, vl_ref, gl_ref, lrl_ref, kc_ref, vc_ref, lrc_ref, wa_ref, ba_ref, og_ref,
                y_ref, oacc_ref, st_ref):
    N, Nc = ql_ref.shape[1], kc_ref.shape[1]
    T = GLA_SC
    CH = GLA_CHUNK
    npair = GLA_HEADS // 2
    row = lax.broadcasted_iota(jnp.int32, (T, T), 0)
    colm = lax.broadcasted_iota(jnp.int32, (T, T), 1)
    same_chunk = (row // CH) == (colm // CH)
    causal = (same_chunk & (colm <= row), same_chunk & (colm >= row))
    tri = tuple(jnp.where(c, 1.0, 0.0).astype(BF16) for c in causal)
    lane_lo = lax.broadcasted_iota(jnp.int32, (T, LANE), 1) < GLA_DK
    srow = lax.broadcasted_iota(jnp.int32, (LANE, 2 * GLA_DV), 0)
    scol = lax.broadcasted_iota(jnp.int32, (LANE, 2 * GLA_DV), 1)
    state_mask = (srow < GLA_DK) == (scol < GLA_DV)
    qscale = GLA_DK ** -0.5

    def decay_terms(lr, d):
        z = jnp.dot(lr, wa_ref[d], preferred_element_type=F32) + ba_ref[d]
        log_a = jax.nn.log_sigmoid(z) * (1.0 / GLA_TAU)
        hi, mid, lo = _split3(log_a)
        cum = (jnp.dot(tri[d], hi, preferred_element_type=F32) + jnp.dot(tri[d], mid, preferred_element_type=F32)
               + jnp.dot(tri[d], lo, preferred_element_type=F32))
        last = (CH - 1) if d == 0 else 0
        tot = [cum[c * CH + last:c * CH + last + 1, :] for c in range(T // CH)]
        tot_rows = jnp.concatenate([jnp.broadcast_to(t, (CH, GLA_QK_W)) for t in tot], axis=0)
        return cum, tot, tot_rows

    def state_update(p, c, k_dec, v, tot):
        rs = slice(c * CH, (c + 1) * CH)
        kd = k_dec[rs, p * LANE:(p + 1) * LANE].astype(BF16)
        vv = v[rs, p * 2 * GLA_DV:(p + 1) * 2 * GLA_DV]
        ds = lax.dot_general(kd, vv, (((0,), (0,)), ((), ())), preferred_element_type=F32)
        dec_row = jnp.exp(tot[c][:, p * LANE:(p + 1) * LANE])
        dec_col = jnp.transpose(jnp.broadcast_to(dec_row, (LANE, LANE)))
        dec_col = jnp.concatenate([dec_col, dec_col], axis=1)
        st_ref[p] = dec_col * st_ref[p] + jnp.where(state_mask, ds, 0.0)

    def superchunk(d, q_ref, k_ref, v_ref, lr_ref, r0, with_output):
        lr = lr_ref[0, pl.ds(r0, T), :]
        k = k_ref[0, pl.ds(r0, T), :].astype(F32)
        v = v_ref[0, pl.ds(r0, T), :]
        cum, tot, tot_rows = decay_terms(lr, d)
        k_dec = k * jnp.exp(tot_rows - cum)
        order = range(T // CH) if d == 0 else range(T // CH - 1, -1, -1)
        if not with_output:
            for c in order:
                for p in range(npair):
                    state_update(p, c, k_dec, v, tot)
            return
        q_dec = ((q_ref[0, pl.ds(r0, T), :].astype(F32) * qscale) * jnp.exp(cum)).astype(BF16)
        k_inv = (k * jnp.exp(-cum)).astype(BF16)
        inter = [[None] * (T // CH) for _ in range(npair)]
        for c in order:
            rs = slice(c * CH, (c + 1) * CH)
            for p in range(npair):
                inter[p][c] = jnp.dot(q_dec[rs, p * LANE:(p + 1) * LANE], st_ref[p].astype(BF16),
                                      preferred_element_type=F32)
                state_update(p, c, k_dec, v, tot)
        for p in range(npair):
            o_inter = jnp.concatenate(inter[p], axis=0)
            qp = q_dec[:, p * LANE:(p + 1) * LANE]
            kp = k_inv[:, p * LANE:(p + 1) * LANE]
            for j in range(2):
                h = 2 * p + j
                qm = jnp.where(lane_lo if j == 0 else jnp.logical_not(lane_lo), qp, jnp.zeros_like(qp))
                att = jnp.where(causal[d], _dot_nt(qm, kp), 0.0).astype(BF16)
                o = (jnp.dot(att, v[:, h * GLA_DV:(h + 1) * GLA_DV], preferred_element_type=F32)
                     + o_inter[:, j * GLA_DV:(j + 1) * GLA_DV])
                if d == 0:
                    oacc_ref[pl.ds(r0, T), h * GLA_DV:(h + 1) * GLA_DV] = o
                else:
                    oacc_ref[pl.ds(r0, T), h * GLA_DV:(h + 1) * GLA_DV] += o

    for d in range(2):
        st_ref[...] = jnp.zeros_like(st_ref)
        ncs, nls = Nc // T, N // T

        def ctx_body(i, carry, d=d, ncs=ncs):
            r0 = pl.multiple_of((i if d == 0 else ncs - 1 - i) * T, T)
            superchunk(d, None, kc_ref, vc_ref, lrc_ref, r0, False)
            return carry

        def lat_body(i, carry, d=d, nls=nls):
            r0 = pl.multiple_of((i if d == 0 else nls - 1 - i) * T, T)
            superchunk(d, ql_ref, kl_ref, vl_ref, lrl_ref, r0, True)
            return carry

        lax.fori_loop(0, ncs, ctx_body, 0)
        lax.fori_loop(0, nls, lat_body, 0)

    def fin(i, carry):
        r0 = pl.multiple_of(i * T, T)
        g = gl_ref[0, pl.ds(r0, T), :].astype(F32)
        for h in range(GLA_HEADS):
            sl = slice(h * GLA_DV, (h + 1) * GLA_DV)
            gh = g[:, sl]
            y_ref[0, pl.ds(r0, T), sl] = (_rms(oacc_ref[pl.ds(r0, T), sl], og_ref[...])
                                          * (gh * jax.nn.sigmoid(gh))).astype(BF16)
        return carry

    lax.fori_loop(0, N // T, fin, 0)


def _gla(ul, uc, wa, ba, o_gain):
    B, N, _ = ul.shape
    Nc = uc.shape[1]
    blk = lambda n, w, base: pl.BlockSpec((1, n, w), lambda b: (b, 0, base // w))
    full = lambda shape: pl.BlockSpec(shape, lambda b: (0,) * len(shape))
    return pl.pallas_call(
        _gla_kernel,
        out_shape=jax.ShapeDtypeStruct((B, N, GLA_V_W), BF16),
        grid=(B,),
        in_specs=[blk(N, GLA_QK_W, OD_GQ), blk(N, GLA_QK_W, OD_GK), blk(N, GLA_V_W, OD_GV),
                  blk(N, GLA_V_W, OD_GG), blk(N, LANE, OD_GLR),
                  blk(Nc, GLA_QK_W, OD_GK), blk(Nc, GLA_V_W, OD_GV), blk(Nc, LANE, OD_GLR),
                  full((2, LANE, GLA_QK_W)), full((2, 1, GLA_QK_W)), full((1, GLA_DV))],
        out_specs=pl.BlockSpec((1, N, GLA_V_W), lambda b: (b, 0, 0)),
        scratch_shapes=[pltpu.VMEM((N, GLA_V_W), F32), pltpu.VMEM((GLA_HEADS // 2, LANE, 2 * GLA_DV), F32)],
        compiler_params=_cparams(("parallel",), 48),
        name="gla",
    )(ul, ul, ul, ul, ul, uc, uc, uc, wa, ba, o_gain)


NA_W = NA_HEADS * NA_HD
NA_QROWS = 4
NA_KROWS = 12
NA_TQ = NA_QROWS * GRID_W
NA_TK = NA_KROWS * GRID_W
NA_NEG = -1e30


NA_SEG = 256


def _na_norm_kernel(q_ref, k_ref, qg_ref, kg_ref, seg_ref, qo_ref, ko_ref):
    def norm(t_ref, g_ref, o_ref):
        for p in range(NA_W // NA_SEG):
            sl = slice(p * NA_SEG, (p + 1) * NA_SEG)
            t = t_ref[0, :, sl].astype(F32)
            sq = t * t
            hi = sq.astype(BF16)
            lo = (sq - hi.astype(F32)).astype(BF16)
            ms = (jnp.dot(hi, seg_ref[...], preferred_element_type=F32)
                  + jnp.dot(lo, seg_ref[...], preferred_element_type=F32))
            o_ref[0, :, sl] = (t * lax.rsqrt(ms + EPS) * g_ref[:, sl]).astype(BF16)

    norm(q_ref, qg_ref, qo_ref)
    norm(k_ref, kg_ref, ko_ref)


def _na_norm(u, qg, kg, tm=512):
    G, M, _ = u.shape
    tm = min(tm, M)
    assert M % tm == 0
    lane_head = np.arange(NA_SEG) // NA_HD
    seg = jnp.asarray((lane_head[:, None] == lane_head[None, :]) * (1.0 / NA_HD), BF16)
    return pl.pallas_call(
        _na_norm_kernel,
        out_shape=(jax.ShapeDtypeStruct((G, M, NA_W), BF16), jax.ShapeDtypeStruct((G, M, NA_W), BF16)),
        grid=(G, M // tm),
        in_specs=[pl.BlockSpec((1, tm, NA_W), lambda b, i: (b, i, OD_NQ // NA_W)),
                  pl.BlockSpec((1, tm, NA_W), lambda b, i: (b, i, OD_NK // NA_W)),
                  pl.BlockSpec((1, NA_W), lambda b, i: (0, 0)), pl.BlockSpec((1, NA_W), lambda b, i: (0, 0)),
                  pl.BlockSpec((NA_SEG, NA_SEG), lambda b, i: (0, 0))],
        out_specs=(pl.BlockSpec((1, tm, NA_W), lambda b, i: (b, i, 0)),
                   pl.BlockSpec((1, tm, NA_W), lambda b, i: (b, i, 0))),
        compiler_params=_cparams(("parallel", "arbitrary"), 32),
        name="na_norm",
    )(u, u, qg, kg, seg)


def _na_kernel(q_ref, k0_ref, k1_ref, k2_ref, v0_ref, v1_ref, v2_ref, kc_ref, vc_ref, bias_ref, o_ref):
    lane_lo = lax.broadcasted_iota(jnp.int32, (NA_TQ, LANE), 1) < NA_HD
    k_refs = (k0_ref, k1_ref, k2_ref)
    v_refs = (v0_ref, v1_ref, v2_ref)
    for p in range(NA_HEADS // 2):
        sl = slice(p * LANE, (p + 1) * LANE)
        qp = q_ref[0, :, sl]
        outs = []
        for j in range(2):
            h = 2 * p + j
            qm = jnp.where(lane_lo if j == 0 else jnp.logical_not(lane_lo), qp, jnp.zeros_like(qp))
            s_l = [_dot_nt(qm, k_refs[t][0, :, sl]) + bias_ref[0, h, :, t * NA_TQ:(t + 1) * NA_TQ] for t in range(3)]
            s_c = _dot_nt(qm, kc_ref[0, :, sl])
            m = jnp.max(s_c, axis=-1, keepdims=True)
            for t in range(3):
                m = jnp.maximum(m, jnp.max(s_l[t], axis=-1, keepdims=True))
            p_c = jnp.exp(s_c - m)
            den = jnp.sum(p_c, axis=-1, keepdims=True)
            o = jnp.dot(p_c.astype(BF16), vc_ref[0, :, sl], preferred_element_type=F32)
            for t in range(3):
                p_l = jnp.exp(s_l[t] - m)
                den = den + jnp.sum(p_l, axis=-1, keepdims=True)
                o = o + jnp.dot(p_l.astype(BF16), v_refs[t][0, :, sl], preferred_element_type=F32)
            outs.append(o / den)
        o_ref[0, :, sl] = jnp.where(lane_lo, outs[0], outs[1]).astype(BF16)


def _na_bias_tables(rpb, rows):
    H, ndi, ndj = rpb.shape
    W = GRID_W
    half = NA_WIN_COLS - 1
    zeros = jnp.zeros((H, ndi, W - 1 - half), rpb.dtype)
    vec = jnp.concatenate([rpb[:, :, half:], zeros, zeros, rpb[:, :, :half]], axis=-1)
    toep = jnp.tile(vec, (1, 1, W))[:, :, :W * (2 * W - 2)].reshape(H, ndi, W, 2 * W - 2)[..., :W]
    qc = np.arange(W)[:, None]
    kc = np.arange(W)[None, :]
    c0 = np.clip(qc - NA_WIN_COLS // 2, 0, W - NA_WIN_COLS)
    toep = jnp.where(jnp.asarray((kc >= c0) & (kc < c0 + NA_WIN_COLS)), toep, NA_NEG)
    masked = jnp.full((H, W, W), NA_NEG, rpb.dtype)
    nblk = rows // NA_QROWS
    tabs = []
    for m in (0, 1, nblk - 1):
        kb = min(max(m - 1, 0), nblk - NA_KROWS // NA_QROWS)
        q_rows = []
        for a in range(NA_QROWS):
            qr = m * NA_QROWS + a
            r0 = min(max(qr - NA_WIN_ROWS // 2, 0), rows - NA_WIN_ROWS)
            k_rows = []
            for i in range(NA_KROWS):
                kr = kb * NA_QROWS + i
                k_rows.append(toep[:, kr - qr + NA_WIN_ROWS - 1] if r0 <= kr < r0 + NA_WIN_ROWS else masked)
            q_rows.append(jnp.stack(k_rows, axis=2))
        tabs.append(jnp.stack(q_rows, axis=1).reshape(H, NA_TQ, NA_TK))
    return jnp.stack(tabs, axis=0)


def _natten(q, k, v_src, kc, uc, bias):
    B, N, _ = q.shape
    Nc = kc.shape[1]
    nblk = N // NA_TQ
    kmax = nblk - NA_KROWS // NA_QROWS
    vcol = OD_NV // NA_W

    def kspec(t):
        return pl.BlockSpec((1, NA_TQ, NA_W), lambda m, b: (b, jnp.clip(m - 1, 0, kmax) + t, 0))

    def vspec(t):
        return pl.BlockSpec((1, NA_TQ, NA_W), lambda m, b: (b, jnp.clip(m - 1, 0, kmax) + t, vcol))

    return pl.pallas_call(
        _na_kernel,
        out_shape=jax.ShapeDtypeStruct((B, N, NA_W), BF16),
        grid=(nblk, B),
        in_specs=[pl.BlockSpec((1, NA_TQ, NA_W), lambda m, b: (b, m, 0)),
                  kspec(0), kspec(1), kspec(2), vspec(0), vspec(1), vspec(2),
                  pl.BlockSpec((1, Nc, NA_W), lambda m, b: (b, 0, 0)),
                  pl.BlockSpec((1, Nc, NA_W), lambda m, b: (b, 0, vcol)),
                  pl.BlockSpec((1, NA_HEADS, NA_TQ, NA_TK),
                               lambda m, b: (jnp.where(m == 0, 0, jnp.where(m == nblk - 1, 2, 1)), 0, 0, 0))],
        out_specs=pl.BlockSpec((1, NA_TQ, NA_W), lambda m, b: (b, m, 0)),
        compiler_params=_cparams(("arbitrary", "arbitrary"), 48),
        name="natten",
    )(q, k, k, k, v_src, v_src, v_src, kc, uc, bias)


def _place(w, layout, total):
    out = jnp.zeros((w.shape[0], total), w.dtype)
    for src, width, dst in layout:
        out = out.at[:, dst:dst + width].set(w[:, src:src + width])
    return out


def _ev_in_weight(w):
    lay = [(0, 256, EV_UQ), (256, 128, EV_UKV), (384, MLA_ROPE, EV_UKR + MLA_NOPE),
           (416, LRU_W, EV_UX), (416 + LRU_W, LRU_W, EV_UG)]
    return _place(w, lay, EV_PAD).astype(BF16)


def _od_in_weight(w):
    src = np.cumsum([0, 256, 256, 512, 512, 2 * GLA_LR, 512, 512, 512])
    dst = [OD_GQ, OD_GK, OD_GV, OD_GG, OD_GLR, OD_NQ, OD_NK, OD_NV]
    lay = [(int(src[i]), int(src[i + 1] - src[i]), dst[i]) for i in range(8)]
    return _place(w, lay, OD_PAD).astype(BF16)


def _pad_heads(w, heads, width):
    k = w.shape[0]
    return jnp.pad(w.reshape(k, heads, width), ((0, 0), (0, 0), (0, HEAD_SLOT - width))).reshape(k, heads * HEAD_SLOT)


def kernel(x, c, ctx, c_ctx, ada_w, ada_b, norm_mix, norm_mlp, w_out, mlp_w1, mlp_w2,
           ev_w_in, mla_q_norm, mla_w_uq, mla_kv_norm, mla_w_ukv, mla_q_gain, mla_k_gain,
           lru_conv_w, lru_conv_b, lru_w_a, lru_b_a, lru_w_x, lru_b_x, lru_lam,
           od_w_in, gla_w_a, gla_b_a, gla_o_gain, na_q_gain, na_k_gain, na_rpb):
    B, N, D = x.shape
    Nc = ctx.shape[1]
    depth = ada_w.shape[0]

    R = -(-(B + 1) // 8) * 8
    cc = jnp.concatenate([c, c_ctx[None], jnp.zeros((R - B - 1, D), c.dtype)], axis=0)
    mods = _modulation(cc, ada_w, ada_b).reshape(depth * R, 1, 6 * D)

    xl = x
    xc = ctx.reshape(1, B * Nc, D)
    for l in range(depth):
        last = l == depth - 1
        j = l // 2
        row_l, row_c = l * R, l * R + B
        w1 = mlp_w1[l].astype(BF16)
        w2 = mlp_w2[l].astype(BF16)
        wo = w_out[l].astype(BF16)
        if l % 2 == 0:
            w_in = _ev_in_weight(ev_w_in[j])
            ul = _norm_mod_matmul(xl, norm_mix[l], mods, row_l, w_in)
            uc = _norm_mod_matmul(xc, norm_mix[l], mods, row_c, w_in).reshape(B, Nc, EV_PAD)
            wuq = _pad_heads(mla_w_uq[j], MLA_HEADS, MLA_QK)
            wukv = mla_w_ukv[j].reshape(MLA_KV_LORA, MLA_HEADS, MLA_NOPE + MLA_V)
            wuk = _pad_heads(wukv[:, :, :MLA_NOPE].reshape(MLA_KV_LORA, -1), MLA_HEADS, MLA_NOPE).astype(BF16)
            wuv = wukv[:, :, MLA_NOPE:].reshape(MLA_KV_LORA, MLA_HEADS * MLA_V).astype(BF16)
            prep = functools.partial(_mla_prep, qn=mla_q_norm[j].reshape(1, -1), kvn=mla_kv_norm[j].reshape(1, -1),
                                     wuq=wuq.astype(BF16), wuqr=_rope_partner_weight(wuq).astype(BF16),
                                     wuk=wuk, wuv=wuv)
            ql, kl, vl = prep(ul, _rope_tables(N, mla_q_gain[j], mla_k_gain[j], True))
            qc, kc, vc = prep(uc, _rope_tables(Nc, mla_q_gain[j], mla_k_gain[j], False))
            ya_l = _mla_attention(ql, kl, vl, kc, vc)
            ya_c = None if last else _mla_attention(qc, None, None, kc, vc)
            wg, bg = _lru_gate_weights(lru_w_a[j], lru_b_a[j], lru_w_x[j], lru_b_x[j])
            yb_l, yb_c = _rglru(ul, uc, lru_conv_w[j], lru_conv_b[j], wg, bg, lru_lam[j])
            wa_w = MLA_HEADS * MLA_V
        else:
            w_in = _od_in_weight(od_w_in[j])
            ul = _norm_mod_matmul(xl, norm_mix[l], mods, row_l, w_in)
            uc = _norm_mod_matmul(xc, norm_mix[l], mods, row_c, w_in).reshape(B, Nc, OD_PAD)
            if not last:
                raise NotImplementedError("context outputs of the odd-layer mixers are only needed when depth > 2")
            wa = jnp.zeros((2, LANE, GLA_QK_W), F32)
            for d in range(2):
                wa = wa.at[d, d * GLA_LR:(d + 1) * GLA_LR].set(gla_w_a[j, d])
            ya_l = _gla(ul, uc, wa.astype(BF16), gla_b_a[j].reshape(2, 1, GLA_QK_W), gla_o_gain[j].reshape(1, GLA_DV))
            ya_c = None
            qg = jnp.tile(na_q_gain[j] * NA_HD ** -0.5, NA_HEADS).reshape(1, NA_W)
            kg = jnp.tile(na_k_gain[j], NA_HEADS).reshape(1, NA_W)
            nq, nk = _na_norm(ul, qg, kg)
            _, nkc = _na_norm(uc, qg, kg)
            yb_l = _natten(nq, nk, ul, nkc, uc, _na_bias_tables(na_rpb[j], N // GRID_W))
            yb_c = None
            wa_w = GLA_V_W
        xl = _out_proj_mlp(xl, ya_l, yb_l, wo[:wa_w], wo[wa_w:], norm_mlp[l], mods, row_l, w1, w2)
        if not last:
            xc = _out_proj_mlp(xc, ya_c.reshape(1, B * Nc, -1), yb_c.reshape(1, B * Nc, -1), wo[:wa_w], wo[wa_w:],
                               norm_mlp[l], mods, row_c, w1, w2)
    return xl
```

```python
import functools

import numpy as np
import jax
import jax.numpy as jnp
from jax import lax
from jax.experimental import pallas as pl
from jax.experimental.pallas import tpu as pltpu

F32 = jnp.float32
BF16 = jnp.bfloat16

D_MODEL = 1024
GRID_W = 64
EPS = 1e-6
ROPE_BASE = 10000.0
D_FF = 4 * D_MODEL

MLA_HEADS = 8
MLA_NOPE = 64
MLA_ROPE = 32
MLA_QK = MLA_NOPE + MLA_ROPE
MLA_V = 64
MLA_Q_LORA = 256
MLA_KV_LORA = 128

LRU_W = 512
LRU_BLOCKS = 8
LRU_BS = LRU_W // LRU_BLOCKS
LRU_C = 8.0
CONV_W = 4
CONV_LEFT = 2

GLA_HEADS = 4
GLA_DK = 64
GLA_DV = 128
GLA_LR = 16
GLA_TAU = 16.0
GLA_CHUNK = 64

NA_HD = 64
NA_HEADS = 8
NA_WIN_ROWS = 8
NA_WIN_COLS = 16

LANE = 128
HEAD_SLOT = 128

EV_UQ, EV_UKV, EV_UKR, EV_UX, EV_UG = 0, 256, 384, 512, 1024
EV_PAD = 1536
OD_GQ, OD_GK, OD_GV, OD_GG, OD_NQ, OD_NK, OD_NV, OD_GLR = 0, 256, 512, 1024, 1536, 2048, 2560, 3072
OD_PAD = 3200


def _cparams(semantics, vmem_mib):
    return pltpu.CompilerParams(dimension_semantics=semantics, vmem_limit_bytes=vmem_mib << 20)


def _rms(x, g):
    return x * lax.rsqrt(jnp.mean(x * x, axis=-1, keepdims=True) + EPS) * g


def _ada_kernel(c_ref, w_ref, b_ref, o_ref):
    cv = c_ref[...]
    s = cv * jax.nn.sigmoid(cv)
    o_ref[0] = jnp.dot(s.astype(BF16), w_ref[0].astype(BF16), preferred_element_type=F32) + b_ref[0]


def _modulation(cc, ada_w, ada_b):
    L, D, D6 = ada_w.shape
    R = cc.shape[0]
    tn = 1536
    return pl.pallas_call(
        _ada_kernel,
        out_shape=jax.ShapeDtypeStruct((L, R, D6), F32),
        grid=(L, D6 // tn),
        in_specs=[pl.BlockSpec((R, D), lambda l, j: (0, 0)),
                  pl.BlockSpec((1, D, tn), lambda l, j: (l, 0, j)),
                  pl.BlockSpec((1, 1, tn), lambda l, j: (l, 0, j))],
        out_specs=pl.BlockSpec((1, R, tn), lambda l, j: (l, 0, j)),
        compiler_params=_cparams(("arbitrary", "arbitrary"), 40),
        name="adaln_modulation",
    )(cc, ada_w, ada_b.reshape(L, 1, D6))


def _mod_spec(row0, k):
    return pl.BlockSpec((1, 1, D_MODEL), lambda g, i: (row0 + g, 0, k))


def _nmm_kernel(x_ref, g_ref, sh_ref, sc_ref, w_ref, o_ref):
    h = _rms(x_ref[0], g_ref[...]) * (1.0 + sc_ref[0]) + sh_ref[0]
    o_ref[0] = jnp.dot(h.astype(BF16), w_ref[...], preferred_element_type=F32).astype(o_ref.dtype)


def _norm_mod_matmul(x, g, mods, row0, w, tm=512):
    G, M, D = x.shape
    assert M % tm == 0
    Nout = w.shape[1]
    return pl.pallas_call(
        _nmm_kernel,
        out_shape=jax.ShapeDtypeStruct((G, M, Nout), BF16),
        grid=(G, M // tm),
        in_specs=[pl.BlockSpec((1, tm, D), lambda b, i: (b, i, 0)),
                  pl.BlockSpec((1, D), lambda b, i: (0, 0)),
                  _mod_spec(row0, 0), _mod_spec(row0, 1),
                  pl.BlockSpec((D, Nout), lambda b, i: (0, 0))],
        out_specs=pl.BlockSpec((1, tm, Nout), lambda b, i: (b, i, 0)),
        compiler_params=_cparams(("parallel", "arbitrary"), 48),
        name="norm_mod_in_proj",
    )(x, g.reshape(1, D), mods, mods, w)


MLP_TF = 1024


def _out_mlp_kernel(x_ref, ya_ref, yb_ref, woa_ref, wob_ref, g_ref, m2_ref, m3_ref, m4_ref, m5_ref,
                    w1_ref, w2_ref, o_ref, h_ref, a_ref):
    y = (jnp.dot(ya_ref[0], woa_ref[...], preferred_element_type=F32)
         + jnp.dot(yb_ref[0], wob_ref[...], preferred_element_type=F32))
    x1 = x_ref[0] + m2_ref[0] * y
    o_ref[0] = x1
    h_ref[...] = (_rms(x1, g_ref[...]) * (1.0 + m4_ref[0]) + m3_ref[0]).astype(BF16)
    for f in range(a_ref.shape[1] // MLP_TF):
        cols = slice(f * MLP_TF, (f + 1) * MLP_TF)
        a = jnp.maximum(jnp.dot(h_ref[...], w1_ref[:, cols], preferred_element_type=F32), 0.0)
        a_ref[:, cols] = (a * a).astype(BF16)
    o_ref[0] += m5_ref[0] * jnp.dot(a_ref[...], w2_ref[...], preferred_element_type=F32)


def _out_proj_mlp(x, ya, yb, woa, wob, g_mlp, mods, row0, w1, w2, tm=512):
    G, M, D = x.shape
    assert M % tm == 0
    Wa, Wb = ya.shape[-1], yb.shape[-1]
    FF = w1.shape[1]
    resident = lambda shape: pl.BlockSpec(shape, lambda b, i: (0, 0), pipeline_mode=pl.Buffered(1))
    return pl.pallas_call(
        _out_mlp_kernel,
        out_shape=jax.ShapeDtypeStruct((G, M, D), F32),
        grid=(G, M // tm),
        in_specs=[pl.BlockSpec((1, tm, D), lambda b, i: (b, i, 0)),
                  pl.BlockSpec((1, tm, Wa), lambda b, i: (b, i, 0)),
                  pl.BlockSpec((1, tm, Wb), lambda b, i: (b, i, 0)),
                  resident((Wa, D)), resident((Wb, D)),
                  pl.BlockSpec((1, D), lambda b, i: (0, 0)),
                  _mod_spec(row0, 2), _mod_spec(row0, 3), _mod_spec(row0, 4), _mod_spec(row0, 5),
                  resident((D, FF)), resident((FF, D))],
        out_specs=pl.BlockSpec((1, tm, D), lambda b, i: (b, i, 0)),
        scratch_shapes=[pltpu.VMEM((tm, D), BF16), pltpu.VMEM((tm, FF), BF16)],
        compiler_params=_cparams(("parallel", "arbitrary"), 52),
        name="out_proj_mlp",
    )(x, ya, yb, woa, wob, g_mlp.reshape(1, D), mods, mods, mods, mods, w1, w2)


def _mla_prep_kernel(uq_ref, ukv_ref, ukr_ref, tq1_ref, tq2_ref, tk1_ref, tka_ref, tkb_ref, qn_ref, kvn_ref,
                     wuq_ref, wuqr_ref, wuk_ref, wuv_ref, q_out, k_out, vt_out):
    qn = _rms(uq_ref[0].astype(F32), qn_ref[...]).astype(BF16)
    kvn = _rms(ukv_ref[0].astype(F32), kvn_ref[...]).astype(BF16)
    q_all = jnp.dot(qn, wuq_ref[...], preferred_element_type=F32)
    q_rot = jnp.dot(qn, wuqr_ref[...], preferred_element_type=F32)
    k_all = jnp.dot(kvn, wuk_ref[...], preferred_element_type=F32)
    v_all = jnp.dot(kvn, wuv_ref[...], preferred_element_type=F32)
    ukr = ukr_ref[0].astype(F32)
    quarter = MLA_ROPE // 4
    k_rot = (pltpu.roll(ukr, HEAD_SLOT - quarter, 1) * tka_ref[...] + pltpu.roll(ukr, quarter, 1) * tkb_ref[...])
    ukr_sq = jnp.sum(ukr * ukr, axis=-1, keepdims=True)
    inv_n = 1.0 / MLA_QK
    for h in range(MLA_HEADS):
        sl = slice(h * HEAD_SLOT, (h + 1) * HEAD_SLOT)
        qh = q_all[:, sl]
        rq = lax.rsqrt(jnp.sum(qh * qh, axis=-1, keepdims=True) * inv_n + EPS)
        q_out[0, :, sl] = (rq * (qh * tq1_ref[...] + q_rot[:, sl] * tq2_ref[...])).astype(BF16)
        kn = k_all[:, sl]
        rk = lax.rsqrt((jnp.sum(kn * kn, axis=-1, keepdims=True) + ukr_sq) * inv_n + EPS)
        k_out[0, :, sl] = (rk * ((kn + ukr) * tk1_ref[...] + k_rot)).astype(BF16)
        vt_out[0, sl, :] = _value_slot_t(v_all[:, sl], MLA_V)


def _rope_partner(t):
    quarter = MLA_ROPE // 4
    t4 = t.reshape(t.shape[:-1] + (2, 2, quarter))
    return jnp.stack([t4[..., 1, :], t4[..., 0, :]], axis=-2).reshape(t.shape)


def _rope_tables(n, q_gain, k_gain, use_rope):
    quarter = MLA_ROPE // 4
    if use_rope:
        pos = jnp.arange(n)
        inv = ROPE_BASE ** (-jnp.arange(0, MLA_ROPE // 2, 2, dtype=F32) / (MLA_ROPE // 2))
        ang_r = (pos // GRID_W).astype(F32)[:, None] * inv[None, :]
        ang_c = (pos % GRID_W).astype(F32)[:, None] * inv[None, :]
        ang = jnp.concatenate([ang_r, ang_r, ang_c, ang_c], axis=-1)
        cos, sin = jnp.cos(ang), jnp.sin(ang)
    else:
        cos, sin = jnp.ones((n, MLA_ROPE), F32), jnp.zeros((n, MLA_ROPE), F32)
    first = (np.arange(MLA_ROPE) % (2 * quarter)) < quarter
    pad = jnp.zeros((n, HEAD_SLOT - MLA_QK), F32)

    def slot(nope, rope):
        return jnp.concatenate([jnp.broadcast_to(nope, (n, MLA_NOPE)), rope, pad], axis=-1)

    zero = jnp.zeros((MLA_NOPE,), F32)
    qg, kg = q_gain * MLA_QK ** -0.5, k_gain
    tq1 = slot(qg[:MLA_NOPE], cos * qg[MLA_NOPE:])
    tq2 = slot(zero, sin * _rope_partner(qg[MLA_NOPE:]))
    tk1 = slot(kg[:MLA_NOPE], cos * kg[MLA_NOPE:])
    ksin = sin * _rope_partner(kg[MLA_NOPE:])
    tka = slot(zero, jnp.where(first, -ksin, 0.0))
    tkb = slot(zero, jnp.where(first, 0.0, ksin))
    return tq1, tq2, tk1, tka, tkb


def _rope_partner_weight(wuq):
    k = wuq.shape[0]
    quarter = MLA_ROPE // 4
    w = wuq.reshape(k, MLA_HEADS, HEAD_SLOT)
    rope = w[:, :, MLA_NOPE:MLA_QK].reshape(k, MLA_HEADS, 2, 2, quarter)
    rot = jnp.stack([-rope[:, :, :, 1, :], rope[:, :, :, 0, :]], axis=3).reshape(k, MLA_HEADS, MLA_ROPE)
    out = jnp.concatenate([jnp.zeros_like(w[:, :, :MLA_NOPE]), rot, jnp.zeros_like(w[:, :, MLA_QK:])], axis=-1)
    return out.reshape(k, MLA_HEADS * HEAD_SLOT)


def _mla_prep(u, tables, qn, kvn, wuq, wuqr, wuk, wuv, tm=256):
    G, M, _ = u.shape
    assert M % tm == 0
    HS = MLA_HEADS * HEAD_SLOT
    full = lambda shape: pl.BlockSpec(shape, lambda b, i: (0,) * len(shape))
    tab = pl.BlockSpec((tm, HEAD_SLOT), lambda b, i: (i, 0))
    return pl.pallas_call(
        _mla_prep_kernel,
        out_shape=(jax.ShapeDtypeStruct((G, M, HS), BF16), jax.ShapeDtypeStruct((G, M, HS), BF16),
                   jax.ShapeDtypeStruct((G, HS, M), BF16)),
        grid=(G, M // tm),
        in_specs=[pl.BlockSpec((1, tm, MLA_Q_LORA), lambda b, i: (b, i, EV_UQ // MLA_Q_LORA)),
                  pl.BlockSpec((1, tm, MLA_KV_LORA), lambda b, i: (b, i, EV_UKV // MLA_KV_LORA)),
                  pl.BlockSpec((1, tm, HEAD_SLOT), lambda b, i: (b, i, EV_UKR // HEAD_SLOT)),
                  tab, tab, tab, tab, tab,
                  full((1, MLA_Q_LORA)), full((1, MLA_KV_LORA)),
                  full((MLA_Q_LORA, HS)), full((MLA_Q_LORA, HS)), full((MLA_KV_LORA, HS)), full((MLA_KV_LORA, HS))],
        out_specs=(pl.BlockSpec((1, tm, HS), lambda b, i: (b, i, 0)),
                   pl.BlockSpec((1, tm, HS), lambda b, i: (b, i, 0)),
                   pl.BlockSpec((1, HS, tm), lambda b, i: (b, 0, i))),
        compiler_params=_cparams(("parallel", "arbitrary"), 40),
        name="mla_prep",
    )(u, u, u, *tables, qn, kvn, wuq, wuqr, wuk, wuv)


def _dot_nt(a, b):
    return lax.dot_general(a, b, (((1,), (1,)), ((), ())), preferred_element_type=F32)


def _value_slot_t(v, ones_row):
    lane = lax.broadcasted_iota(jnp.int32, v.shape, 1)
    slot = jnp.where(lane < ones_row, v, jnp.where(lane == ones_row, 1.0, 0.0))
    return jnp.transpose(slot).astype(BF16)


ATTN_GROUP = 8


def _col_max(a, rows=8):
    parts = [a[r:r + rows] for r in range(0, a.shape[0], rows)]
    while len(parts) > 1:
        parts = [jnp.maximum(parts[i], parts[i + 1]) if i + 1 < len(parts) else parts[i]
                 for i in range(0, len(parts), 2)]
    return jnp.max(parts[0], axis=0, keepdims=True)


def _attend_t(heads):
    scores = []
    for q, keys, _, biases in heads:
        s = [_dot_nt(k, q) for k in keys]
        if biases is not None:
            s = [a if b is None else a + b for a, b in zip(s, biases)]
        scores.append(s)
    maxes = [functools.reduce(jnp.maximum, [_col_max(a) for a in s]) for s in scores]
    outs = []
    for (_, _, values_t, _), s, m in zip(heads, scores, maxes):
        out_t = None
        for a, vt in zip(s, values_t):
            part = jnp.dot(vt, jnp.exp(a - m).astype(BF16), preferred_element_type=F32)
            out_t = part if out_t is None else out_t + part
        outs.append(out_t)
    return outs


def _pair_output(slots_t, dv):
    halves = [t[:dv] / t[dv:dv + 1] for t in slots_t]
    return jnp.transpose(jnp.concatenate(halves, axis=0))


def _mla_attn_kernel(*refs, with_latent):
    if with_latent:
        q_ref, kl_ref, vtl_ref, kc_ref, vtc_ref, o_ref = refs
    else:
        q_ref, kc_ref, vtc_ref, o_ref = refs
    for h0 in range(0, MLA_HEADS, ATTN_GROUP):
        heads = []
        for h in range(h0, h0 + ATTN_GROUP):
            sl = slice(h * HEAD_SLOT, (h + 1) * HEAD_SLOT)
            keys, values_t = [kc_ref[0, :, sl]], [vtc_ref[0, sl, :]]
            if with_latent:
                keys.append(kl_ref[0, :, sl])
                values_t.append(vtl_ref[0, sl, :])
            heads.append((q_ref[0, :, sl], keys, values_t, None))
        outs = _attend_t(heads)
        for i in range(0, ATTN_GROUP, 2):
            hp = (h0 + i) // 2
            o_ref[0, :, hp * LANE:(hp + 1) * LANE] = _pair_output(outs[i:i + 2], MLA_V).astype(BF16)


def _mla_attention(q, kl, vtl, kc, vtc, tq=256):
    B, M, HS = q.shape
    assert M % tq == 0
    HV = MLA_HEADS * MLA_V
    Nc = kc.shape[1]
    with_latent = kl is not None
    whole = lambda n, w: pl.BlockSpec((1, n, w), lambda b, i: (b, 0, 0))
    in_specs = [pl.BlockSpec((1, tq, HS), lambda b, i: (b, i, 0))]
    args = [q]
    if with_latent:
        in_specs += [whole(kl.shape[1], HS), whole(HS, kl.shape[1])]
        args += [kl, vtl]
    in_specs += [whole(Nc, HS), whole(HS, Nc)]
    args += [kc, vtc]
    return pl.pallas_call(
        functools.partial(_mla_attn_kernel, with_latent=with_latent),
        out_shape=jax.ShapeDtypeStruct((B, M, HV), BF16),
        grid=(B, M // tq),
        in_specs=in_specs,
        out_specs=pl.BlockSpec((1, tq, HV), lambda b, i: (b, i, 0)),
        compiler_params=_cparams(("parallel", "arbitrary"), 48),
        name="mla_attention",
    )(*args)


LRU_CW = 256
LRU_HALO = 16
LRU_TN = 256


def _sigmoid(x):
    return 0.5 * jnp.tanh(0.5 * x) + 0.5


def _gelu_tanh(x):
    return 0.5 * x * (1.0 + jnp.tanh(0.7978845608028654 * (x + 0.044715 * (x * x * x))))


def _scan_group(a, bv, h, reverse):
    row = lax.broadcasted_iota(jnp.int32, a.shape, 0)
    for s in (1, 2, 4):
        if reverse:
            keep = row < 8 - s
            shift = 8 - s
        else:
            keep = row >= s
            shift = s
        a_s = jnp.where(keep, pltpu.roll(a, shift, 0), 1.0)
        b_s = jnp.where(keep, pltpu.roll(bv, shift, 0), 0.0)
        bv = a * b_s + bv
        a = a * a_s
    hs = a * h + bv
    return hs, (hs[0:1, :] if reverse else hs[7:8, :])


def _lru_kernel(uxl_ref, ugl_ref, uxc_ref, ugc_ref, cw_ref, cb_ref, wg_ref, bg_ref, lam_ref, yl_ref, yc_ref,
                xpl_ref, xpc_ref, xcv_ref, af_ref, bf_ref, ab_ref, bb_ref):
    N, Nc = uxl_ref.shape[1], uxc_ref.shape[1]
    NT = N + Nc
    C = LRU_CW
    H = LRU_HALO

    def conv(src_ref, pad_ref, n, row0):
        pad_ref[0:H, :] = jnp.zeros((H, C), F32)
        pad_ref[H + n:H + n + H, :] = jnp.zeros((H, C), F32)
        pad_ref[H:H + n, :] = src_ref[0].astype(F32)
        y = cb_ref[...] + pad_ref[H - CONV_LEFT:H - CONV_LEFT + n, :] * cw_ref[0:1, :]
        for j in range(1, CONV_W):
            y = y + pad_ref[H - CONV_LEFT + j:H - CONV_LEFT + j + n, :] * cw_ref[j:j + 1, :]
        xcv_ref[row0:row0 + n, :] = y

    conv(uxc_ref, xpc_ref, Nc, 0)
    conv(uxl_ref, xpl_ref, N, Nc)

    lam = lam_ref[...]
    softplus = jnp.maximum(-lam, 0.0) + jnp.log1p(jnp.exp(-jnp.abs(lam)))

    def coeff_chunk(i, carry):
        r0 = pl.multiple_of(i * LRU_TN, LRU_TN)
        x = xcv_ref[pl.ds(r0, LRU_TN), :]
        g = jnp.dot(x.astype(BF16), wg_ref[0], preferred_element_type=F32) + bg_ref[0]
        for d, (a_ref, b_ref) in enumerate(((af_ref, bf_ref), (ab_ref, bb_ref))):
            r = _sigmoid(g[:, (2 * d) * C:(2 * d + 1) * C])
            gi = _sigmoid(g[:, (2 * d + 1) * C:(2 * d + 2) * C])
            a = jnp.exp((-LRU_C) * r * softplus[d:d + 1, :])
            a_ref[pl.ds(r0, LRU_TN), :] = a
            b_ref[pl.ds(r0, LRU_TN), :] = jnp.sqrt(1.0 - a * a) * (gi * x)
        return carry

    lax.fori_loop(0, NT // LRU_TN, coeff_chunk, 0)

    ngc, ngt = Nc // 8, NT // 8

    def scan_step(i, carry):
        hf, hb = carry
        rf = pl.multiple_of(i * 8, 8)
        rb = pl.multiple_of(jnp.where(i < ngc, ngc - 1 - i, ngt + ngc - 1 - i) * 8, 8)
        hs_f, hf = _scan_group(af_ref[pl.ds(rf, 8), :], bf_ref[pl.ds(rf, 8), :], hf, False)
        hs_b, hb = _scan_group(ab_ref[pl.ds(rb, 8), :], bb_ref[pl.ds(rb, 8), :], hb, True)
        bf_ref[pl.ds(rf, 8), :] = hs_f
        bb_ref[pl.ds(rb, 8), :] = hs_b
        return hf, hb

    zero = jnp.zeros((1, C), F32)
    lax.fori_loop(0, ngt, scan_step, (zero, zero), unroll=4)

    def out_chunk(i, carry):
        r0 = pl.multiple_of(i * LRU_TN, LRU_TN)
        hsum = bf_ref[pl.ds(Nc + r0, LRU_TN), :] + bb_ref[pl.ds(Nc + r0, LRU_TN), :]
        gate = _gelu_tanh(ugl_ref[0, pl.ds(r0, LRU_TN), :].astype(F32))
        yl_ref[0, pl.ds(r0, LRU_TN), :] = (hsum * gate).astype(BF16)
        return carry

    lax.fori_loop(0, N // LRU_TN, out_chunk, 0)
    yc_ref[0] = ((bf_ref[0:Nc, :] + bb_ref[0:Nc, :]) * _gelu_tanh(ugc_ref[0].astype(F32))).astype(BF16)


def _rglru(ul, uc, conv_w, conv_b, wg, bg, lam):
    B, N, _ = ul.shape
    Nc = uc.shape[1]
    C = LRU_CW
    nh = LRU_W // C
    NT = N + Nc
    col = lambda base: (lambda b, j: (b, 0, base // C + j))
    par = lambda rows: pl.BlockSpec((rows, C), lambda b, j: (0, j))
    return pl.pallas_call(
        _lru_kernel,
        out_shape=(jax.ShapeDtypeStruct((B, N, LRU_W), BF16), jax.ShapeDtypeStruct((B, Nc, LRU_W), BF16)),
        grid=(B, nh),
        in_specs=[pl.BlockSpec((1, N, C), col(EV_UX)), pl.BlockSpec((1, N, C), col(EV_UG)),
                  pl.BlockSpec((1, Nc, C), col(EV_UX)), pl.BlockSpec((1, Nc, C), col(EV_UG)),
                  par(CONV_W), par(1),
                  pl.BlockSpec((1, C, 4 * C), lambda b, j: (j, 0, 0)),
                  pl.BlockSpec((1, 1, 4 * C), lambda b, j: (j, 0, 0)),
                  par(2)],
        out_specs=(pl.BlockSpec((1, N, C), lambda b, j: (b, 0, j)),
                   pl.BlockSpec((1, Nc, C), lambda b, j: (b, 0, j))),
        scratch_shapes=[pltpu.VMEM((N + 2 * LRU_HALO, C), F32), pltpu.VMEM((Nc + 2 * LRU_HALO, C), F32),
                        pltpu.VMEM((NT, C), F32)] + [pltpu.VMEM((NT, C), F32)] * 4,
        compiler_params=_cparams(("parallel", "arbitrary"), 48),
        name="rglru",
    )(ul, ul, uc, uc, conv_w, conv_b.reshape(1, LRU_W), wg, bg, lam)


def _lru_gate_weights(w_a, b_a, w_x, b_x):
    C = LRU_CW
    nh = LRU_W // C
    kb = C // LRU_BS

    def dense(w):
        w = w.reshape(nh, kb, LRU_BS, LRU_BS)
        eye = jnp.eye(kb, dtype=w.dtype)
        return jnp.einsum('hkij,kl->hkilj', w, eye).reshape(nh, C, C)

    wg = jnp.concatenate([dense(w_a[0]), dense(w_x[0]), dense(w_a[1]), dense(w_x[1])], axis=-1).astype(BF16)
    bg = jnp.stack([b_a[0], b_x[0], b_a[1], b_x[1]], axis=0).reshape(4, nh, C)
    bg = jnp.transpose(bg, (1, 0, 2)).reshape(nh, 1, 4 * C)
    return wg, bg


GLA_SC = 256
GLA_QK_W = GLA_HEADS * GLA_DK
GLA_V_W = GLA_HEADS * GLA_DV


def _split3(x):
    hi = x.astype(BF16)
    r1 = x - hi.astype(F32)
    mid = r1.astype(BF16)
    lo = (r1 - mid.astype(F32)).astype(BF16)
    return hi, mid, lo


def _gla_kernel(ql_ref, kl_ref, vl_ref, gl_ref, lrl_ref, kc_ref, vc_ref, lrc_ref, wa_ref, ba_ref, og_ref,
                y_ref, oacc_ref, st_ref):
    N, Nc = ql_ref.shape[1], kc_ref.shape[1]
    T = GLA_SC
    CH = GLA_CHUNK
    npair = GLA_HEADS // 2
    row = lax.broadcasted_iota(jnp.int32, (T, T), 0)
    colm = lax.broadcasted_iota(jnp.int32, (T, T), 1)
    same_chunk = (row // CH) == (colm // CH)
    causal = (same_chunk & (colm <= row), same_chunk & (colm >= row))
    tri = tuple(jnp.where(c, 1.0, 0.0).astype(BF16) for c in causal)
    lane_lo = lax.broadcasted_iota(jnp.int32, (T, LANE), 1) < GLA_DK
    srow = lax.broadcasted_iota(jnp.int32, (LANE, 2 * GLA_DV), 0)
    scol = lax.broadcasted_iota(jnp.int32, (LANE, 2 * GLA_DV), 1)
    state_mask = (srow < GLA_DK) == (scol < GLA_DV)
    qscale = GLA_DK ** -0.5

    def decay_terms(lr, d):
        z = jnp.dot(lr, wa_ref[d], preferred_element_type=F32) + ba_ref[d]
        log_a = jax.nn.log_sigmoid(z) * (1.0 / GLA_TAU)
        hi, mid, lo = _split3(log_a)
        cum = (jnp.dot(tri[d], hi, preferred_element_type=F32) + jnp.dot(tri[d], mid, preferred_element_type=F32)
               + jnp.dot(tri[d], lo, preferred_element_type=F32))
        last = (CH - 1) if d == 0 else 0
        tot = [cum[c * CH + last:c * CH + last + 1, :] for c in range(T // CH)]
        tot_rows = jnp.concatenate([jnp.broadcast_to(t, (CH, GLA_QK_W)) for t in tot], axis=0)
        return cum, tot, tot_rows

    def superchunk(d, q_ref, k_ref, v_ref, lr_ref, r0, with_output):
        lr = lr_ref[0, pl.ds(r0, T), :]
        k = k_ref[0, pl.ds(r0, T), :].astype(F32)
        v = v_ref[0, pl.ds(r0, T), :]
        cum, tot, tot_rows = decay_terms(lr, d)
        k_dec = (k * jnp.exp(tot_rows - cum)).astype(BF16)
        nch = T // CH
        order = range(nch) if d == 0 else range(nch - 1, -1, -1)
        ds, dec = {}, {}
        for c in range(nch):
            rs = slice(c * CH, (c + 1) * CH)
            for p in range(npair):
                kd = k_dec[rs, p * LANE:(p + 1) * LANE]
                vv = v[rs, p * 2 * GLA_DV:(p + 1) * 2 * GLA_DV]
                kv = lax.dot_general(kd, vv, (((0,), (0,)), ((), ())), preferred_element_type=F32)
                ds[c, p] = jnp.where(state_mask, kv, 0.0)
                dec_row = jnp.exp(tot[c][:, p * LANE:(p + 1) * LANE])
                dec_col = jnp.transpose(jnp.broadcast_to(dec_row, (LANE, LANE)))
                dec[c, p] = jnp.concatenate([dec_col, dec_col], axis=1)
        s_in = {}
        for p in range(npair):
            s = st_ref[p]
            for c in order:
                s_in[c, p] = s.astype(BF16)
                s = dec[c, p] * s + ds[c, p]
            st_ref[p] = s
        if not with_output:
            return
        q_dec = ((q_ref[0, pl.ds(r0, T), :].astype(F32) * qscale) * jnp.exp(cum)).astype(BF16)
        k_inv = (k * jnp.exp(-cum)).astype(BF16)
        for p in range(npair):
            qp = q_dec[:, p * LANE:(p + 1) * LANE]
            o_inter = jnp.concatenate(
                [jnp.dot(qp[c * CH:(c + 1) * CH], s_in[c, p], preferred_element_type=F32) for c in range(nch)],
                axis=0)
            kp = k_inv[:, p * LANE:(p + 1) * LANE]
            for j in range(2):
                h = 2 * p + j
                qm = jnp.where(lane_lo if j == 0 else jnp.logical_not(lane_lo), qp, jnp.zeros_like(qp))
                att = jnp.where(causal[d], _dot_nt(qm, kp), 0.0).astype(BF16)
                o = (jnp.dot(att, v[:, h * GLA_DV:(h + 1) * GLA_DV], preferred_element_type=F32)
                     + o_inter[:, j * GLA_DV:(j + 1) * GLA_DV])
                if d == 0:
                    oacc_ref[pl.ds(r0, T), h * GLA_DV:(h + 1) * GLA_DV] = o
                else:
                    oacc_ref[pl.ds(r0, T), h * GLA_DV:(h + 1) * GLA_DV] += o

    for d in range(2):
        st_ref[...] = jnp.zeros_like(st_ref)
        ncs, nls = Nc // T, N // T

        def ctx_body(i, carry, d=d, ncs=ncs):
            r0 = pl.multiple_of((i if d == 0 else ncs - 1 - i) * T, T)
            superchunk(d, None, kc_ref, vc_ref, lrc_ref, r0, False)
            return carry

        def lat_body(i, carry, d=d, nls=nls):
            r0 = pl.multiple_of((i if d == 0 else nls - 1 - i) * T, T)
            superchunk(d, ql_ref, kl_ref, vl_ref, lrl_ref, r0, True)
            return carry

        lax.fori_loop(0, ncs, ctx_body, 0)
        lax.fori_loop(0, nls, lat_body, 0)

    def fin(i, carry):
        r0 = pl.multiple_of(i * T, T)
        g = gl_ref[0, pl.ds(r0, T), :].astype(F32)
        for h in range(GLA_HEADS):
            sl = slice(h * GLA_DV, (h + 1) * GLA_DV)
            gh = g[:, sl]
            y_ref[0, pl.ds(r0, T), sl] = (_rms(oacc_ref[pl.ds(r0, T), sl], og_ref[...])
                                          * (gh * jax.nn.sigmoid(gh))).astype(BF16)
        return carry

    lax.fori_loop(0, N // T, fin, 0)


def _gla(ul, uc, wa, ba, o_gain):
    B, N, _ = ul.shape
    Nc = uc.shape[1]
    blk = lambda n, w, base: pl.BlockSpec((1, n, w), lambda b: (b, 0, base // w))
    full = lambda shape: pl.BlockSpec(shape, lambda b: (0,) * len(shape))
    return pl.pallas_call(
        _gla_kernel,
        out_shape=jax.ShapeDtypeStruct((B, N, GLA_V_W), BF16),
        grid=(B,),
        in_specs=[blk(N, GLA_QK_W, OD_GQ), blk(N, GLA_QK_W, OD_GK), blk(N, GLA_V_W, OD_GV),
                  blk(N, GLA_V_W, OD_GG), blk(N, LANE, OD_GLR),
                  blk(Nc, GLA_QK_W, OD_GK), blk(Nc, GLA_V_W, OD_GV), blk(Nc, LANE, OD_GLR),
                  full((2, LANE, GLA_QK_W)), full((2, 1, GLA_QK_W)), full((1, GLA_DV))],
        out_specs=pl.BlockSpec((1, N, GLA_V_W), lambda b: (b, 0, 0)),
        scratch_shapes=[pltpu.VMEM((N, GLA_V_W), F32), pltpu.VMEM((GLA_HEADS // 2, LANE, 2 * GLA_DV), F32)],
        compiler_params=_cparams(("parallel",), 48),
        name="gla",
    )(ul, ul, ul, ul, ul, uc, uc, uc, wa, ba, o_gain)


NA_W = NA_HEADS * NA_HD
NA_QROWS = 4
NA_KROWS = 12
NA_TQ = NA_QROWS * GRID_W
NA_TK = NA_KROWS * GRID_W
NA_NEG = -1e30


NA_SEG = 256


def _na_norm_kernel(q_ref, k_ref, v_ref, qg_ref, kg_ref, seg_ref, qo_ref, ko_ref, vto_ref):
    for p in range(NA_W // LANE):
        pair = v_ref[0, :, p * LANE:(p + 1) * LANE].astype(F32)
        vto_ref[0, (2 * p) * HEAD_SLOT:(2 * p + 1) * HEAD_SLOT, :] = _value_slot_t(pair, NA_HD)
        vto_ref[0, (2 * p + 1) * HEAD_SLOT:(2 * p + 2) * HEAD_SLOT, :] = _value_slot_t(pltpu.roll(pair, NA_HD, 1), NA_HD)

    def norm(t_ref, g_ref, o_ref):
        for p in range(NA_W // NA_SEG):
            sl = slice(p * NA_SEG, (p + 1) * NA_SEG)
            t = t_ref[0, :, sl].astype(F32)
            sq = t * t
            hi = sq.astype(BF16)
            lo = (sq - hi.astype(F32)).astype(BF16)
            ms = (jnp.dot(hi, seg_ref[...], preferred_element_type=F32)
                  + jnp.dot(lo, seg_ref[...], preferred_element_type=F32))
            o_ref[0, :, sl] = (t * lax.rsqrt(ms + EPS) * g_ref[:, sl]).astype(BF16)

    norm(q_ref, qg_ref, qo_ref)
    norm(k_ref, kg_ref, ko_ref)


def _na_norm(u, qg, kg, tm=512):
    G, M, _ = u.shape
    tm = min(tm, M)
    assert M % tm == 0
    VS = NA_HEADS * HEAD_SLOT
    lane_head = np.arange(NA_SEG) // NA_HD
    seg = jnp.asarray((lane_head[:, None] == lane_head[None, :]) * (1.0 / NA_HD), BF16)
    col = lambda base: pl.BlockSpec((1, tm, NA_W), lambda b, i: (b, i, base // NA_W))
    return pl.pallas_call(
        _na_norm_kernel,
        out_shape=(jax.ShapeDtypeStruct((G, M, NA_W), BF16), jax.ShapeDtypeStruct((G, M, NA_W), BF16),
                   jax.ShapeDtypeStruct((G, VS, M), BF16)),
        grid=(G, M // tm),
        in_specs=[col(OD_NQ), col(OD_NK), col(OD_NV),
                  pl.BlockSpec((1, NA_W), lambda b, i: (0, 0)), pl.BlockSpec((1, NA_W), lambda b, i: (0, 0)),
                  pl.BlockSpec((NA_SEG, NA_SEG), lambda b, i: (0, 0))],
        out_specs=(pl.BlockSpec((1, tm, NA_W), lambda b, i: (b, i, 0)),
                   pl.BlockSpec((1, tm, NA_W), lambda b, i: (b, i, 0)),
                   pl.BlockSpec((1, VS, tm), lambda b, i: (b, 0, i))),
        compiler_params=_cparams(("parallel", "arbitrary"), 32),
        name="na_norm",
    )(u, u, u, qg, kg, seg)


def _na_kernel(q_ref, k0_ref, k1_ref, k2_ref, vt0_ref, vt1_ref, vt2_ref, kc_ref, vtc_ref, bias_ref, o_ref):
    lane_lo = lax.broadcasted_iota(jnp.int32, (NA_TQ, LANE), 1) < NA_HD
    k_refs = (k0_ref, k1_ref, k2_ref, kc_ref)
    vt_refs = (vt0_ref, vt1_ref, vt2_ref, vtc_ref)
    for h0 in range(0, NA_HEADS, ATTN_GROUP):
        heads = []
        for h in range(h0, h0 + ATTN_GROUP):
            sl = slice((h // 2) * LANE, (h // 2 + 1) * LANE)
            hs = slice(h * HEAD_SLOT, (h + 1) * HEAD_SLOT)
            qp = q_ref[0, :, sl]
            qm = jnp.where(lane_lo if h % 2 == 0 else jnp.logical_not(lane_lo), qp, jnp.zeros_like(qp))
            biases = [bias_ref[0, h, t * NA_TQ:(t + 1) * NA_TQ, :] for t in range(3)] + [None]
            heads.append((qm, [r[0, :, sl] for r in k_refs], [r[0, hs, :] for r in vt_refs], biases))
        outs = _attend_t(heads)
        for i in range(0, ATTN_GROUP, 2):
            hp = (h0 + i) // 2
            o_ref[0, :, hp * LANE:(hp + 1) * LANE] = _pair_output(outs[i:i + 2], NA_HD).astype(BF16)


def _na_bias_tables(rpb, rows):
    H, ndi, ndj = rpb.shape
    W = GRID_W
    half = NA_WIN_COLS - 1
    zeros = jnp.zeros((H, ndi, W - 1 - half), rpb.dtype)
    vec = jnp.concatenate([rpb[:, :, half:], zeros, zeros, rpb[:, :, :half]], axis=-1)
    toep = jnp.tile(vec, (1, 1, W))[:, :, :W * (2 * W - 2)].reshape(H, ndi, W, 2 * W - 2)[..., :W]
    qc = np.arange(W)[:, None]
    kc = np.arange(W)[None, :]
    c0 = np.clip(qc - NA_WIN_COLS // 2, 0, W - NA_WIN_COLS)
    toep = jnp.where(jnp.asarray((kc >= c0) & (kc < c0 + NA_WIN_COLS)), toep, NA_NEG)
    toep = jnp.swapaxes(toep, 2, 3)
    masked = jnp.full((H, W, W), NA_NEG, rpb.dtype)
    nblk = rows // NA_QROWS
    tabs = []
    for m in (0, 1, nblk - 1):
        kb = min(max(m - 1, 0), nblk - NA_KROWS // NA_QROWS)
        k_rows = []
        for i in range(NA_KROWS):
            kr = kb * NA_QROWS + i
            q_rows = []
            for a in range(NA_QROWS):
                qr = m * NA_QROWS + a
                r0 = min(max(qr - NA_WIN_ROWS // 2, 0), rows - NA_WIN_ROWS)
                q_rows.append(toep[:, kr - qr + NA_WIN_ROWS - 1] if r0 <= kr < r0 + NA_WIN_ROWS else masked)
            k_rows.append(jnp.stack(q_rows, axis=2))
        tabs.append(jnp.stack(k_rows, axis=1).reshape(H, NA_TK, NA_TQ))
    return jnp.stack(tabs, axis=0)


def _natten(q, k, vt, kc, vtc, bias):
    B, N, _ = q.shape
    Nc = kc.shape[1]
    nblk = N // NA_TQ
    kmax = nblk - NA_KROWS // NA_QROWS
    VS = NA_HEADS * HEAD_SLOT

    def kspec(t):
        return pl.BlockSpec((1, NA_TQ, NA_W), lambda m, b: (b, jnp.clip(m - 1, 0, kmax) + t, 0))

    def vspec(t):
        return pl.BlockSpec((1, VS, NA_TQ), lambda m, b: (b, 0, jnp.clip(m - 1, 0, kmax) + t))

    return pl.pallas_call(
        _na_kernel,
        out_shape=jax.ShapeDtypeStruct((B, N, NA_W), BF16),
        grid=(nblk, B),
        in_specs=[pl.BlockSpec((1, NA_TQ, NA_W), lambda m, b: (b, m, 0)),
                  kspec(0), kspec(1), kspec(2), vspec(0), vspec(1), vspec(2),
                  pl.BlockSpec((1, Nc, NA_W), lambda m, b: (b, 0, 0)),
                  pl.BlockSpec((1, VS, Nc), lambda m, b: (b, 0, 0)),
                  pl.BlockSpec((1, NA_HEADS, NA_TK, NA_TQ),
                               lambda m, b: (jnp.where(m == 0, 0, jnp.where(m == nblk - 1, 2, 1)), 0, 0, 0))],
        out_specs=pl.BlockSpec((1, NA_TQ, NA_W), lambda m, b: (b, m, 0)),
        compiler_params=_cparams(("arbitrary", "arbitrary"), 48),
        name="natten",
    )(q, k, k, k, vt, vt, vt, kc, vtc, bias)


def _place(w, layout, total):
    out = jnp.zeros((w.shape[0], total), w.dtype)
    for src, width, dst in layout:
        out = out.at[:, dst:dst + width].set(w[:, src:src + width])
    return out


def _ev_in_weight(w):
    lay = [(0, 256, EV_UQ), (256, 128, EV_UKV), (384, MLA_ROPE, EV_UKR + MLA_NOPE),
           (416, LRU_W, EV_UX), (416 + LRU_W, LRU_W, EV_UG)]
    return _place(w, lay, EV_PAD).astype(BF16)


def _od_in_weight(w):
    src = np.cumsum([0, 256, 256, 512, 512, 2 * GLA_LR, 512, 512, 512])
    dst = [OD_GQ, OD_GK, OD_GV, OD_GG, OD_GLR, OD_NQ, OD_NK, OD_NV]
    lay = [(int(src[i]), int(src[i + 1] - src[i]), dst[i]) for i in range(8)]
    return _place(w, lay, OD_PAD).astype(BF16)


def _pad_heads(w, heads, width):
    k = w.shape[0]
    return jnp.pad(w.reshape(k, heads, width), ((0, 0), (0, 0), (0, HEAD_SLOT - width))).reshape(k, heads * HEAD_SLOT)


def kernel(x, c, ctx, c_ctx, ada_w, ada_b, norm_mix, norm_mlp, w_out, mlp_w1, mlp_w2,
           ev_w_in, mla_q_norm, mla_w_uq, mla_kv_norm, mla_w_ukv, mla_q_gain, mla_k_gain,
           lru_conv_w, lru_conv_b, lru_w_a, lru_b_a, lru_w_x, lru_b_x, lru_lam,
           od_w_in, gla_w_a, gla_b_a, gla_o_gain, na_q_gain, na_k_gain, na_rpb):
    B, N, D = x.shape
    Nc = ctx.shape[1]
    depth = ada_w.shape[0]

    R = -(-(B + 1) // 8) * 8
    cc = jnp.concatenate([c, c_ctx[None], jnp.zeros((R - B - 1, D), c.dtype)], axis=0)
    mods = _modulation(cc, ada_w, ada_b).reshape(depth * R, 1, 6 * D)

    xl = x
    xc = ctx.reshape(1, B * Nc, D)
    for l in range(depth):
        last = l == depth - 1
        j = l // 2
        row_l, row_c = l * R, l * R + B
        w1 = mlp_w1[l].astype(BF16)
        w2 = mlp_w2[l].astype(BF16)
        wo = w_out[l].astype(BF16)
        if l % 2 == 0:
            w_in = _ev_in_weight(ev_w_in[j])
            ul = _norm_mod_matmul(xl, norm_mix[l], mods, row_l, w_in)
            uc = _norm_mod_matmul(xc, norm_mix[l], mods, row_c, w_in).reshape(B, Nc, EV_PAD)
            wuq = _pad_heads(mla_w_uq[j], MLA_HEADS, MLA_QK)
            wukv = mla_w_ukv[j].reshape(MLA_KV_LORA, MLA_HEADS, MLA_NOPE + MLA_V)
            wuk = _pad_heads(wukv[:, :, :MLA_NOPE].reshape(MLA_KV_LORA, -1), MLA_HEADS, MLA_NOPE).astype(BF16)
            wuv = _pad_heads(wukv[:, :, MLA_NOPE:].reshape(MLA_KV_LORA, -1), MLA_HEADS, MLA_V).astype(BF16)
            prep = functools.partial(_mla_prep, qn=mla_q_norm[j].reshape(1, -1), kvn=mla_kv_norm[j].reshape(1, -1),
                                     wuq=wuq.astype(BF16), wuqr=_rope_partner_weight(wuq).astype(BF16),
                                     wuk=wuk, wuv=wuv)
            ql, kl, vl = prep(ul, _rope_tables(N, mla_q_gain[j], mla_k_gain[j], True))
            qc, kc, vc = prep(uc, _rope_tables(Nc, mla_q_gain[j], mla_k_gain[j], False))
            ya_l = _mla_attention(ql, kl, vl, kc, vc)
            ya_c = None if last else _mla_attention(qc, None, None, kc, vc)
            wg, bg = _lru_gate_weights(lru_w_a[j], lru_b_a[j], lru_w_x[j], lru_b_x[j])
            yb_l, yb_c = _rglru(ul, uc, lru_conv_w[j], lru_conv_b[j], wg, bg, lru_lam[j])
            wa_w = MLA_HEADS * MLA_V
        else:
            w_in = _od_in_weight(od_w_in[j])
            ul = _norm_mod_matmul(xl, norm_mix[l], mods, row_l, w_in)
            uc = _norm_mod_matmul(xc, norm_mix[l], mods, row_c, w_in).reshape(B, Nc, OD_PAD)
            if not last:
                raise NotImplementedError("context outputs of the odd-layer mixers are only needed when depth > 2")
            wa = jnp.zeros((2, LANE, GLA_QK_W), F32)
            for d in range(2):
                wa = wa.at[d, d * GLA_LR:(d + 1) * GLA_LR].set(gla_w_a[j, d])
            ya_l = _gla(ul, uc, wa.astype(BF16), gla_b_a[j].reshape(2, 1, GLA_QK_W), gla_o_gain[j].reshape(1, GLA_DV))
            ya_c = None
            qg = jnp.tile(na_q_gain[j] * NA_HD ** -0.5, NA_HEADS).reshape(1, NA_W)
            kg = jnp.tile(na_k_gain[j], NA_HEADS).reshape(1, NA_W)
            nq, nk, nvt = _na_norm(ul, qg, kg)
            _, nkc, nvtc = _na_norm(uc, qg, kg)
            yb_l = _natten(nq, nk, nvt, nkc, nvtc, _na_bias_tables(na_rpb[j], N // GRID_W))
            yb_c = None
            wa_w = GLA_V_W
        xl = _out_proj_mlp(xl, ya_l, yb_l, wo[:wa_w], wo[wa_w:], norm_mlp[l], mods, row_l, w1, w2)
        if not last:
            xc = _out_proj_mlp(xc, ya_c.reshape(1, B * Nc, -1), yb_c.reshape(1, B * Nc, -1), wo[:wa_w], wo[wa_w:],
                               norm_mlp[l], mods, row_c, w1, w2)
    return xl
```

```python
import functools

import numpy as np
import jax
import jax.numpy as jnp
from jax import lax
from jax.experimental import pallas as pl
from jax.experimental.pallas import tpu as pltpu

F32 = jnp.float32
BF16 = jnp.bfloat16

D_MODEL = 1024
GRID_W = 64
EPS = 1e-6
ROPE_BASE = 10000.0
D_FF = 4 * D_MODEL

MLA_HEADS = 8
MLA_NOPE = 64
MLA_ROPE = 32
MLA_QK = MLA_NOPE + MLA_ROPE
MLA_V = 64
MLA_Q_LORA = 256
MLA_KV_LORA = 128

LRU_W = 512
LRU_BLOCKS = 8
LRU_BS = LRU_W // LRU_BLOCKS
LRU_C = 8.0
CONV_W = 4
CONV_LEFT = 2

GLA_HEADS = 4
GLA_DK = 64
GLA_DV = 128
GLA_LR = 16
GLA_TAU = 16.0
GLA_CHUNK = 64

NA_HD = 64
NA_HEADS = 8
NA_WIN_ROWS = 8
NA_WIN_COLS = 16

LANE = 128
HEAD_SLOT = 128

EV_UQ, EV_UKV, EV_UKR, EV_UX, EV_UG = 0, 256, 384, 512, 1024
EV_PAD = 1536
OD_GQ, OD_GK, OD_GV, OD_GG, OD_NQ, OD_NK, OD_NV, OD_GLR = 0, 256, 512, 1024, 1536, 2048, 2560, 3072
OD_PAD = 3200


def _cparams(semantics, vmem_mib):
    return pltpu.CompilerParams(dimension_semantics=semantics, vmem_limit_bytes=vmem_mib << 20)


def _rms(x, g):
    return x * lax.rsqrt(jnp.mean(x * x, axis=-1, keepdims=True) + EPS) * g


def _ada_kernel(c_ref, w_ref, b_ref, o_ref):
    cv = c_ref[...]
    s = cv * jax.nn.sigmoid(cv)
    o_ref[0] = jnp.dot(s.astype(BF16), w_ref[0].astype(BF16), preferred_element_type=F32) + b_ref[0]


def _modulation(cc, ada_w, ada_b):
    L, D, D6 = ada_w.shape
    R = cc.shape[0]
    tn = 1536
    return pl.pallas_call(
        _ada_kernel,
        out_shape=jax.ShapeDtypeStruct((L, R, D6), F32),
        grid=(L, D6 // tn),
        in_specs=[pl.BlockSpec((R, D), lambda l, j: (0, 0)),
                  pl.BlockSpec((1, D, tn), lambda l, j: (l, 0, j)),
                  pl.BlockSpec((1, 1, tn), lambda l, j: (l, 0, j))],
        out_specs=pl.BlockSpec((1, R, tn), lambda l, j: (l, 0, j)),
        compiler_params=_cparams(("arbitrary", "arbitrary"), 40),
        name="adaln_modulation",
    )(cc, ada_w, ada_b.reshape(L, 1, D6))


def _mod_spec(row0, k):
    return pl.BlockSpec((1, 1, D_MODEL), lambda g, i: (row0 + g, 0, k))


def _nmm_kernel(x_ref, g_ref, sh_ref, sc_ref, w_ref, o_ref):
    h = _rms(x_ref[0], g_ref[...]) * (1.0 + sc_ref[0]) + sh_ref[0]
    o_ref[0] = jnp.dot(h.astype(BF16), w_ref[...], preferred_element_type=F32).astype(o_ref.dtype)


def _norm_mod_matmul(x, g, mods, row0, w, tm=512):
    G, M, D = x.shape
    assert M % tm == 0
    Nout = w.shape[1]
    return pl.pallas_call(
        _nmm_kernel,
        out_shape=jax.ShapeDtypeStruct((G, M, Nout), BF16),
        grid=(G, M // tm),
        in_specs=[pl.BlockSpec((1, tm, D), lambda b, i: (b, i, 0)),
                  pl.BlockSpec((1, D), lambda b, i: (0, 0)),
                  _mod_spec(row0, 0), _mod_spec(row0, 1),
                  pl.BlockSpec((D, Nout), lambda b, i: (0, 0))],
        out_specs=pl.BlockSpec((1, tm, Nout), lambda b, i: (b, i, 0)),
        compiler_params=_cparams(("parallel", "arbitrary"), 48),
        name="norm_mod_in_proj",
    )(x, g.reshape(1, D), mods, mods, w)


MLP_TF = 1024


def _out_mlp_kernel(x_ref, ya_ref, yb_ref, woa_ref, wob_ref, g_ref, m2_ref, m3_ref, m4_ref, m5_ref,
                    w1_ref, w2_ref, o_ref, h_ref, a_ref):
    y = (jnp.dot(ya_ref[0], woa_ref[...], preferred_element_type=F32)
         + jnp.dot(yb_ref[0], wob_ref[...], preferred_element_type=F32))
    x1 = x_ref[0] + m2_ref[0] * y
    o_ref[0] = x1
    h_ref[...] = (_rms(x1, g_ref[...]) * (1.0 + m4_ref[0]) + m3_ref[0]).astype(BF16)
    for f in range(a_ref.shape[1] // MLP_TF):
        cols = slice(f * MLP_TF, (f + 1) * MLP_TF)
        a = jnp.maximum(jnp.dot(h_ref[...], w1_ref[:, cols], preferred_element_type=F32), 0.0)
        a_ref[:, cols] = (a * a).astype(BF16)
    o_ref[0] += m5_ref[0] * jnp.dot(a_ref[...], w2_ref[...], preferred_element_type=F32)


def _out_proj_mlp(x, ya, yb, woa, wob, g_mlp, mods, row0, w1, w2, tm=512):
    G, M, D = x.shape
    assert M % tm == 0
    Wa, Wb = ya.shape[-1], yb.shape[-1]
    FF = w1.shape[1]
    resident = lambda shape: pl.BlockSpec(shape, lambda b, i: (0, 0), pipeline_mode=pl.Buffered(1))
    return pl.pallas_call(
        _out_mlp_kernel,
        out_shape=jax.ShapeDtypeStruct((G, M, D), F32),
        grid=(G, M // tm),
        in_specs=[pl.BlockSpec((1, tm, D), lambda b, i: (b, i, 0)),
                  pl.BlockSpec((1, tm, Wa), lambda b, i: (b, i, 0)),
                  pl.BlockSpec((1, tm, Wb), lambda b, i: (b, i, 0)),
                  resident((Wa, D)), resident((Wb, D)),
                  pl.BlockSpec((1, D), lambda b, i: (0, 0)),
                  _mod_spec(row0, 2), _mod_spec(row0, 3), _mod_spec(row0, 4), _mod_spec(row0, 5),
                  resident((D, FF)), resident((FF, D))],
        out_specs=pl.BlockSpec((1, tm, D), lambda b, i: (b, i, 0)),
        scratch_shapes=[pltpu.VMEM((tm, D), BF16), pltpu.VMEM((tm, FF), BF16)],
        compiler_params=_cparams(("parallel", "arbitrary"), 52),
        name="out_proj_mlp",
    )(x, ya, yb, woa, wob, g_mlp.reshape(1, D), mods, mods, mods, mods, w1, w2)


def _mla_prep_kernel(uq_ref, ukv_ref, ukr_ref, tq1_ref, tq2_ref, tk1_ref, tka_ref, tkb_ref, qn_ref, kvn_ref,
                     wuq_ref, wuqr_ref, wuk_ref, wuv_ref, q_out, k_out, vt_out):
    qn = _rms(uq_ref[0].astype(F32), qn_ref[...]).astype(BF16)
    kvn = _rms(ukv_ref[0].astype(F32), kvn_ref[...]).astype(BF16)
    q_all = jnp.dot(qn, wuq_ref[...], preferred_element_type=F32)
    q_rot = jnp.dot(qn, wuqr_ref[...], preferred_element_type=F32)
    k_all = jnp.dot(kvn, wuk_ref[...], preferred_element_type=F32)
    v_all = jnp.dot(kvn, wuv_ref[...], preferred_element_type=F32)
    ukr = ukr_ref[0].astype(F32)
    quarter = MLA_ROPE // 4
    k_rot = (pltpu.roll(ukr, HEAD_SLOT - quarter, 1) * tka_ref[...] + pltpu.roll(ukr, quarter, 1) * tkb_ref[...])
    ukr_sq = jnp.sum(ukr * ukr, axis=-1, keepdims=True)
    inv_n = 1.0 / MLA_QK
    for h in range(MLA_HEADS):
        sl = slice(h * HEAD_SLOT, (h + 1) * HEAD_SLOT)
        qh = q_all[:, sl]
        rq = lax.rsqrt(jnp.sum(qh * qh, axis=-1, keepdims=True) * inv_n + EPS)
        q_out[0, :, sl] = (rq * (qh * tq1_ref[...] + q_rot[:, sl] * tq2_ref[...])).astype(BF16)
        kn = k_all[:, sl]
        rk = lax.rsqrt((jnp.sum(kn * kn, axis=-1, keepdims=True) + ukr_sq) * inv_n + EPS)
        k_out[0, :, sl] = (rk * ((kn + ukr) * tk1_ref[...] + k_rot)).astype(BF16)
        vt_out[0, sl, :] = _value_slot_t(v_all[:, sl], MLA_V)


def _rope_partner(t):
    quarter = MLA_ROPE // 4
    t4 = t.reshape(t.shape[:-1] + (2, 2, quarter))
    return jnp.stack([t4[..., 1, :], t4[..., 0, :]], axis=-2).reshape(t.shape)


def _rope_tables(n, q_gain, k_gain, use_rope):
    quarter = MLA_ROPE // 4
    if use_rope:
        pos = jnp.arange(n)
        inv = ROPE_BASE ** (-jnp.arange(0, MLA_ROPE // 2, 2, dtype=F32) / (MLA_ROPE // 2))
        ang_r = (pos // GRID_W).astype(F32)[:, None] * inv[None, :]
        ang_c = (pos % GRID_W).astype(F32)[:, None] * inv[None, :]
        ang = jnp.concatenate([ang_r, ang_r, ang_c, ang_c], axis=-1)
        cos, sin = jnp.cos(ang), jnp.sin(ang)
    else:
        cos, sin = jnp.ones((n, MLA_ROPE), F32), jnp.zeros((n, MLA_ROPE), F32)
    first = (np.arange(MLA_ROPE) % (2 * quarter)) < quarter
    pad = jnp.zeros((n, HEAD_SLOT - MLA_QK), F32)

    def slot(nope, rope):
        return jnp.concatenate([jnp.broadcast_to(nope, (n, MLA_NOPE)), rope, pad], axis=-1)

    zero = jnp.zeros((MLA_NOPE,), F32)
    qg, kg = q_gain * MLA_QK ** -0.5, k_gain
    tq1 = slot(qg[:MLA_NOPE], cos * qg[MLA_NOPE:])
    tq2 = slot(zero, sin * _rope_partner(qg[MLA_NOPE:]))
    tk1 = slot(kg[:MLA_NOPE], cos * kg[MLA_NOPE:])
    ksin = sin * _rope_partner(kg[MLA_NOPE:])
    tka = slot(zero, jnp.where(first, -ksin, 0.0))
    tkb = slot(zero, jnp.where(first, 0.0, ksin))
    return tq1, tq2, tk1, tka, tkb


def _rope_partner_weight(wuq):
    k = wuq.shape[0]
    quarter = MLA_ROPE // 4
    w = wuq.reshape(k, MLA_HEADS, HEAD_SLOT)
    rope = w[:, :, MLA_NOPE:MLA_QK].reshape(k, MLA_HEADS, 2, 2, quarter)
    rot = jnp.stack([-rope[:, :, :, 1, :], rope[:, :, :, 0, :]], axis=3).reshape(k, MLA_HEADS, MLA_ROPE)
    out = jnp.concatenate([jnp.zeros_like(w[:, :, :MLA_NOPE]), rot, jnp.zeros_like(w[:, :, MLA_QK:])], axis=-1)
    return out.reshape(k, MLA_HEADS * HEAD_SLOT)


def _mla_prep(u, tables, qn, kvn, wuq, wuqr, wuk, wuv, tm=256):
    G, M, _ = u.shape
    assert M % tm == 0
    HS = MLA_HEADS * HEAD_SLOT
    full = lambda shape: pl.BlockSpec(shape, lambda b, i: (0,) * len(shape))
    tab = pl.BlockSpec((tm, HEAD_SLOT), lambda b, i: (i, 0))
    return pl.pallas_call(
        _mla_prep_kernel,
        out_shape=(jax.ShapeDtypeStruct((G, M, HS), BF16), jax.ShapeDtypeStruct((G, M, HS), BF16),
                   jax.ShapeDtypeStruct((G, HS, M), BF16)),
        grid=(G, M // tm),
        in_specs=[pl.BlockSpec((1, tm, MLA_Q_LORA), lambda b, i: (b, i, EV_UQ // MLA_Q_LORA)),
                  pl.BlockSpec((1, tm, MLA_KV_LORA), lambda b, i: (b, i, EV_UKV // MLA_KV_LORA)),
                  pl.BlockSpec((1, tm, HEAD_SLOT), lambda b, i: (b, i, EV_UKR // HEAD_SLOT)),
                  tab, tab, tab, tab, tab,
                  full((1, MLA_Q_LORA)), full((1, MLA_KV_LORA)),
                  full((MLA_Q_LORA, HS)), full((MLA_Q_LORA, HS)), full((MLA_KV_LORA, HS)), full((MLA_KV_LORA, HS))],
        out_specs=(pl.BlockSpec((1, tm, HS), lambda b, i: (b, i, 0)),
                   pl.BlockSpec((1, tm, HS), lambda b, i: (b, i, 0)),
                   pl.BlockSpec((1, HS, tm), lambda b, i: (b, 0, i))),
        compiler_params=_cparams(("parallel", "arbitrary"), 40),
        name="mla_prep",
    )(u, u, u, *tables, qn, kvn, wuq, wuqr, wuk, wuv)


def _dot_nt(a, b):
    return lax.dot_general(a, b, (((1,), (1,)), ((), ())), preferred_element_type=F32)


def _value_slot_t(v, ones_row):
    lane = lax.broadcasted_iota(jnp.int32, v.shape, 1)
    slot = jnp.where(lane < ones_row, v, jnp.where(lane == ones_row, 1.0, 0.0))
    return jnp.transpose(slot).astype(BF16)


ATTN_GROUP = 8


def _col_max(a, rows=8):
    parts = [a[r:r + rows] for r in range(0, a.shape[0], rows)]
    while len(parts) > 1:
        parts = [jnp.maximum(parts[i], parts[i + 1]) if i + 1 < len(parts) else parts[i]
                 for i in range(0, len(parts), 2)]
    return jnp.max(parts[0], axis=0, keepdims=True)


def _attend_t(heads):
    scores = []
    for q, keys, _, biases in heads:
        s = [_dot_nt(k, q) for k in keys]
        if biases is not None:
            s = [a if b is None else a + b for a, b in zip(s, biases)]
        scores.append(s)
    maxes = [functools.reduce(jnp.maximum, [_col_max(a) for a in s]) for s in scores]
    outs = []
    for (_, _, values_t, _), s, m in zip(heads, scores, maxes):
        out_t = None
        for a, vt in zip(s, values_t):
            part = jnp.dot(vt, jnp.exp(a - m).astype(BF16), preferred_element_type=F32)
            out_t = part if out_t is None else out_t + part
        outs.append(out_t)
    return outs


def _pair_output(slots_t, dv):
    halves = [t[:dv] / t[dv:dv + 1] for t in slots_t]
    return jnp.transpose(jnp.concatenate(halves, axis=0))


def _mla_attn_kernel(*refs, with_latent):
    if with_latent:
        q_ref, kl_ref, vtl_ref, kc_ref, vtc_ref, o_ref = refs
    else:
        q_ref, kc_ref, vtc_ref, o_ref = refs
    for h0 in range(0, MLA_HEADS, ATTN_GROUP):
        heads = []
        for h in range(h0, h0 + ATTN_GROUP):
            sl = slice(h * HEAD_SLOT, (h + 1) * HEAD_SLOT)
            keys, values_t = [kc_ref[0, :, sl]], [vtc_ref[0, sl, :]]
            if with_latent:
                keys.append(kl_ref[0, :, sl])
                values_t.append(vtl_ref[0, sl, :])
            heads.append((q_ref[0, :, sl], keys, values_t, None))
        outs = _attend_t(heads)
        for i in range(0, ATTN_GROUP, 2):
            hp = (h0 + i) // 2
            o_ref[0, :, hp * LANE:(hp + 1) * LANE] = _pair_output(outs[i:i + 2], MLA_V).astype(BF16)


def _mla_attention(q, kl, vtl, kc, vtc, tq=256):
    B, M, HS = q.shape
    assert M % tq == 0
    HV = MLA_HEADS * MLA_V
    Nc = kc.shape[1]
    with_latent = kl is not None
    whole = lambda n, w: pl.BlockSpec((1, n, w), lambda b, i: (b, 0, 0))
    in_specs = [pl.BlockSpec((1, tq, HS), lambda b, i: (b, i, 0))]
    args = [q]
    if with_latent:
        in_specs += [whole(kl.shape[1], HS), whole(HS, kl.shape[1])]
        args += [kl, vtl]
    in_specs += [whole(Nc, HS), whole(HS, Nc)]
    args += [kc, vtc]
    return pl.pallas_call(
        functools.partial(_mla_attn_kernel, with_latent=with_latent),
        out_shape=jax.ShapeDtypeStruct((B, M, HV), BF16),
        grid=(B, M // tq),
        in_specs=in_specs,
        out_specs=pl.BlockSpec((1, tq, HV), lambda b, i: (b, i, 0)),
        compiler_params=_cparams(("parallel", "arbitrary"), 48),
        name="mla_attention",
    )(*args)


LRU_CW = 256
LRU_HALO = 16
LRU_TN = 256


def _gelu_tanh(x):
    return 0.5 * x * (1.0 + jnp.tanh(0.7978845608028654 * (x + 0.044715 * (x * x * x))))


def _scan_group(a, bv, h, reverse):
    row = lax.broadcasted_iota(jnp.int32, a.shape, 0)
    for s in (1, 2, 4):
        if reverse:
            keep = row < 8 - s
            shift = 8 - s
        else:
            keep = row >= s
            shift = s
        a_s = jnp.where(keep, pltpu.roll(a, shift, 0), 1.0)
        b_s = jnp.where(keep, pltpu.roll(bv, shift, 0), 0.0)
        bv = a * b_s + bv
        a = a * a_s
    hs = a * h + bv
    return hs, (hs[0:1, :] if reverse else hs[7:8, :])


def _lru_kernel(uxl_ref, ugl_ref, uxc_ref, ugc_ref, cw_ref, cb_ref, wg_ref, bg_ref, lam_ref, yl_ref, yc_ref,
                xpl_ref, xpc_ref, xcv_ref, af_ref, bf_ref, ab_ref, bb_ref):
    N, Nc = uxl_ref.shape[1], uxc_ref.shape[1]
    NT = N + Nc
    C = LRU_CW
    H = LRU_HALO

    def conv(src_ref, pad_ref, n, row0):
        pad_ref[0:H, :] = jnp.zeros((H, C), F32)
        pad_ref[H + n:H + n + H, :] = jnp.zeros((H, C), F32)
        pad_ref[H:H + n, :] = src_ref[0].astype(F32)
        y = cb_ref[...] + pad_ref[H - CONV_LEFT:H - CONV_LEFT + n, :] * cw_ref[0:1, :]
        for j in range(1, CONV_W):
            y = y + pad_ref[H - CONV_LEFT + j:H - CONV_LEFT + j + n, :] * cw_ref[j:j + 1, :]
        xcv_ref[row0:row0 + n, :] = y

    conv(uxc_ref, xpc_ref, Nc, 0)
    conv(uxl_ref, xpl_ref, N, Nc)

    lam = lam_ref[...]
    c_half = (-0.5 * LRU_C) * (jnp.maximum(-lam, 0.0) + jnp.log1p(jnp.exp(-jnp.abs(lam))))

    def coeff_chunk(i, carry):
        r0 = pl.multiple_of(i * LRU_TN, LRU_TN)
        x = xcv_ref[pl.ds(r0, LRU_TN), :]
        t = jnp.tanh(jnp.dot(x.astype(BF16), wg_ref[0], preferred_element_type=F32) + bg_ref[0])
        hx = 0.5 * x
        for d, (a_ref, b_ref) in enumerate(((af_ref, bf_ref), (ab_ref, bb_ref))):
            c = c_half[d:d + 1, :]
            a = jnp.exp(c * t[:, (2 * d) * C:(2 * d + 1) * C] + c)
            a_ref[pl.ds(r0, LRU_TN), :] = a
            gated_x = hx * t[:, (2 * d + 1) * C:(2 * d + 2) * C] + hx
            b_ref[pl.ds(r0, LRU_TN), :] = jnp.sqrt(1.0 - a * a) * gated_x
        return carry

    lax.fori_loop(0, NT // LRU_TN, coeff_chunk, 0)

    ngc, ngt = Nc // 8, NT // 8

    def scan_step(i, carry):
        hf, hb = carry
        rf = pl.multiple_of(i * 8, 8)
        rb = pl.multiple_of(jnp.where(i < ngc, ngc - 1 - i, ngt + ngc - 1 - i) * 8, 8)
        hs_f, hf = _scan_group(af_ref[pl.ds(rf, 8), :], bf_ref[pl.ds(rf, 8), :], hf, False)
        hs_b, hb = _scan_group(ab_ref[pl.ds(rb, 8), :], bb_ref[pl.ds(rb, 8), :], hb, True)
        bf_ref[pl.ds(rf, 8), :] = hs_f
        bb_ref[pl.ds(rb, 8), :] = hs_b
        return hf, hb

    zero = jnp.zeros((1, C), F32)
    lax.fori_loop(0, ngt, scan_step, (zero, zero), unroll=4)

    def out_chunk(i, carry):
        r0 = pl.multiple_of(i * LRU_TN, LRU_TN)
        hsum = bf_ref[pl.ds(Nc + r0, LRU_TN), :] + bb_ref[pl.ds(Nc + r0, LRU_TN), :]
        gate = _gelu_tanh(ugl_ref[0, pl.ds(r0, LRU_TN), :].astype(F32))
        yl_ref[0, pl.ds(r0, LRU_TN), :] = (hsum * gate).astype(BF16)
        return carry

    lax.fori_loop(0, N // LRU_TN, out_chunk, 0)
    yc_ref[0] = ((bf_ref[0:Nc, :] + bb_ref[0:Nc, :]) * _gelu_tanh(ugc_ref[0].astype(F32))).astype(BF16)


def _rglru(ul, uc, conv_w, conv_b, wg, bg, lam):
    B, N, _ = ul.shape
    Nc = uc.shape[1]
    C = LRU_CW
    nh = LRU_W // C
    NT = N + Nc
    col = lambda base: (lambda b, j: (b, 0, base // C + j))
    par = lambda rows: pl.BlockSpec((rows, C), lambda b, j: (0, j))
    return pl.pallas_call(
        _lru_kernel,
        out_shape=(jax.ShapeDtypeStruct((B, N, LRU_W), BF16), jax.ShapeDtypeStruct((B, Nc, LRU_W), BF16)),
        grid=(B, nh),
        in_specs=[pl.BlockSpec((1, N, C), col(EV_UX)), pl.BlockSpec((1, N, C), col(EV_UG)),
                  pl.BlockSpec((1, Nc, C), col(EV_UX)), pl.BlockSpec((1, Nc, C), col(EV_UG)),
                  par(CONV_W), par(1),
                  pl.BlockSpec((1, C, 4 * C), lambda b, j: (j, 0, 0)),
                  pl.BlockSpec((1, 1, 4 * C), lambda b, j: (j, 0, 0)),
                  par(2)],
        out_specs=(pl.BlockSpec((1, N, C), lambda b, j: (b, 0, j)),
                   pl.BlockSpec((1, Nc, C), lambda b, j: (b, 0, j))),
        scratch_shapes=[pltpu.VMEM((N + 2 * LRU_HALO, C), F32), pltpu.VMEM((Nc + 2 * LRU_HALO, C), F32),
                        pltpu.VMEM((NT, C), F32)] + [pltpu.VMEM((NT, C), F32)] * 4,
        compiler_params=_cparams(("parallel", "arbitrary"), 48),
        name="rglru",
    )(ul, ul, uc, uc, conv_w, conv_b.reshape(1, LRU_W), wg, bg, lam)


def _lru_gate_weights(w_a, b_a, w_x, b_x):
    C = LRU_CW
    nh = LRU_W // C
    kb = C // LRU_BS

    def dense(w):
        w = w.reshape(nh, kb, LRU_BS, LRU_BS)
        eye = jnp.eye(kb, dtype=w.dtype)
        return jnp.einsum('hkij,kl->hkilj', w, eye).reshape(nh, C, C)

    wg = jnp.concatenate([dense(w_a[0]), dense(w_x[0]), dense(w_a[1]), dense(w_x[1])], axis=-1)
    bg = jnp.stack([b_a[0], b_x[0], b_a[1], b_x[1]], axis=0).reshape(4, nh, C)
    bg = jnp.transpose(bg, (1, 0, 2)).reshape(nh, 1, 4 * C)
    return (0.5 * wg).astype(BF16), 0.5 * bg


GLA_SC = 256
GLA_UNROLL = 2
GLA_QK_W = GLA_HEADS * GLA_DK
GLA_V_W = GLA_HEADS * GLA_DV


def _split3(x):
    hi = x.astype(BF16)
    r1 = x - hi.astype(F32)
    mid = r1.astype(BF16)
    lo = (r1 - mid.astype(F32)).astype(BF16)
    return hi, mid, lo


def _gla_kernel(ql_ref, kl_ref, vl_ref, gl_ref, lrl_ref, kc_ref, vc_ref, lrc_ref, wa_ref, ba_ref, og_ref,
                y_ref, oacc_ref, st_ref):
    N, Nc = ql_ref.shape[1], kc_ref.shape[1]
    T = GLA_SC
    CH = GLA_CHUNK
    npair = GLA_HEADS // 2
    row = lax.broadcasted_iota(jnp.int32, (T, T), 0)
    colm = lax.broadcasted_iota(jnp.int32, (T, T), 1)
    same_chunk = (row // CH) == (colm // CH)
    causal = (same_chunk & (colm <= row), same_chunk & (colm >= row))
    tri = tuple(jnp.where(c, 1.0, 0.0).astype(BF16) for c in causal)
    lane_lo = lax.broadcasted_iota(jnp.int32, (T, LANE), 1) < GLA_DK
    srow = lax.broadcasted_iota(jnp.int32, (LANE, 2 * GLA_DV), 0)
    scol = lax.broadcasted_iota(jnp.int32, (LANE, 2 * GLA_DV), 1)
    state_mask = (srow < GLA_DK) == (scol < GLA_DV)
    qscale = GLA_DK ** -0.5

    def superchunks(streams, q_ref, k_ref, v_ref, lr_ref, with_output):
        nch = T // CH
        work = []
        for d, r0 in streams:
            z = jnp.dot(lr_ref[0, pl.ds(r0, T), :], wa_ref[d], preferred_element_type=F32) + ba_ref[d]
            work.append(dict(d=d, r0=r0, split=_split3(jax.nn.log_sigmoid(z) * (1.0 / GLA_TAU))))
        for w in work:
            d = w['d']
            cum = sum(jnp.dot(tri[d], t, preferred_element_type=F32) for t in w['split'])
            last = (CH - 1) if d == 0 else 0
            tot = [cum[c * CH + last:c * CH + last + 1, :] for c in range(nch)]
            tot_rows = jnp.concatenate([jnp.broadcast_to(t, (CH, GLA_QK_W)) for t in tot], axis=0)
            k = k_ref[0, pl.ds(w['r0'], T), :].astype(F32)
            w.update(cum=cum, tot=tot, k=k, v=v_ref[0, pl.ds(w['r0'], T), :],
                     k_dec=(k * jnp.exp(tot_rows - cum)).astype(BF16))
        for w in work:
            ds, dec = {}, {}
            for c in range(nch):
                rs = slice(c * CH, (c + 1) * CH)
                for p in range(npair):
                    kd = w['k_dec'][rs, p * LANE:(p + 1) * LANE]
                    vv = w['v'][rs, p * 2 * GLA_DV:(p + 1) * 2 * GLA_DV]
                    kv = lax.dot_general(kd, vv, (((0,), (0,)), ((), ())), preferred_element_type=F32)
                    ds[c, p] = jnp.where(state_mask, kv, 0.0)
                    dec_row = jnp.exp(w['tot'][c][:, p * LANE:(p + 1) * LANE])
                    dec_col = jnp.transpose(jnp.broadcast_to(dec_row, (LANE, LANE)))
                    dec[c, p] = jnp.concatenate([dec_col, dec_col], axis=1)
            w.update(ds=ds, dec=dec)
        for w in work:
            d = w['d']
            order = range(nch) if d == 0 else range(nch - 1, -1, -1)
            s_in = {}
            for p in range(npair):
                s = st_ref[d, p]
                for c in order:
                    s_in[c, p] = s.astype(BF16)
                    s = w['dec'][c, p] * s + w['ds'][c, p]
                st_ref[d, p] = s
            w.update(s_in=s_in)
        if not with_output:
            return
        for w in work:
            q = q_ref[0, pl.ds(w['r0'], T), :].astype(F32)
            w.update(q_dec=((q * qscale) * jnp.exp(w['cum'])).astype(BF16),
                     k_inv=(w['k'] * jnp.exp(-w['cum'])).astype(BF16))
        for w in work:
            d, r0 = w['d'], w['r0']
            for p in range(npair):
                qp = w['q_dec'][:, p * LANE:(p + 1) * LANE]
                kp = w['k_inv'][:, p * LANE:(p + 1) * LANE]
                o_inter = jnp.concatenate(
                    [jnp.dot(qp[c * CH:(c + 1) * CH], w['s_in'][c, p], preferred_element_type=F32)
                     for c in range(nch)], axis=0)
                for j in range(2):
                    h = 2 * p + j
                    qm = jnp.where(lane_lo if j == 0 else jnp.logical_not(lane_lo), qp, jnp.zeros_like(qp))
                    att = jnp.where(causal[d], _dot_nt(qm, kp), 0.0).astype(BF16)
                    o = (jnp.dot(att, w['v'][:, h * GLA_DV:(h + 1) * GLA_DV], preferred_element_type=F32)
                         + o_inter[:, j * GLA_DV:(j + 1) * GLA_DV])
                    oacc_ref[d, pl.ds(r0, T), h * GLA_DV:(h + 1) * GLA_DV] = o

    st_ref[...] = jnp.zeros_like(st_ref)
    ncs, nls = Nc // T, N // T

    def ctx_body(i, carry):
        streams = [(0, pl.multiple_of(i * T, T)), (1, pl.multiple_of((ncs - 1 - i) * T, T))]
        superchunks(streams, None, kc_ref, vc_ref, lrc_ref, False)
        return carry

    def lat_body(i, carry):
        streams = []
        for u in range(GLA_UNROLL):
            streams += [(0, pl.multiple_of((GLA_UNROLL * i + u) * T, T)),
                        (1, pl.multiple_of((nls - 1 - GLA_UNROLL * i - u) * T, T))]
        superchunks(streams, ql_ref, kl_ref, vl_ref, lrl_ref, True)
        return carry

    lax.fori_loop(0, ncs, ctx_body, 0)
    lax.fori_loop(0, nls // GLA_UNROLL, lat_body, 0)

    def fin(i, carry):
        r0 = pl.multiple_of(i * T, T)
        g = gl_ref[0, pl.ds(r0, T), :].astype(F32)
        for h in range(GLA_HEADS):
            sl = slice(h * GLA_DV, (h + 1) * GLA_DV)
            gh = g[:, sl]
            o = oacc_ref[0, pl.ds(r0, T), sl] + oacc_ref[1, pl.ds(r0, T), sl]
            y_ref[0, pl.ds(r0, T), sl] = (_rms(o, og_ref[...]) * (gh * jax.nn.sigmoid(gh))).astype(BF16)
        return carry

    lax.fori_loop(0, N // T, fin, 0)


def _gla(ul, uc, wa, ba, o_gain):
    B, N, _ = ul.shape
    Nc = uc.shape[1]
    blk = lambda n, w, base: pl.BlockSpec((1, n, w), lambda b: (b, 0, base // w))
    full = lambda shape: pl.BlockSpec(shape, lambda b: (0,) * len(shape))
    return pl.pallas_call(
        _gla_kernel,
        out_shape=jax.ShapeDtypeStruct((B, N, GLA_V_W), BF16),
        grid=(B,),
        in_specs=[blk(N, GLA_QK_W, OD_GQ), blk(N, GLA_QK_W, OD_GK), blk(N, GLA_V_W, OD_GV),
                  blk(N, GLA_V_W, OD_GG), blk(N, LANE, OD_GLR),
                  blk(Nc, GLA_QK_W, OD_GK), blk(Nc, GLA_V_W, OD_GV), blk(Nc, LANE, OD_GLR),
                  full((2, LANE, GLA_QK_W)), full((2, 1, GLA_QK_W)), full((1, GLA_DV))],
        out_specs=pl.BlockSpec((1, N, GLA_V_W), lambda b: (b, 0, 0)),
        scratch_shapes=[pltpu.VMEM((2, N, GLA_V_W), F32),
                        pltpu.VMEM((2, GLA_HEADS // 2, LANE, 2 * GLA_DV), F32)],
        compiler_params=_cparams(("parallel",), 48),
        name="gla",
    )(ul, ul, ul, ul, ul, uc, uc, uc, wa, ba, o_gain)


NA_W = NA_HEADS * NA_HD
NA_QROWS = 4
NA_KROWS = 12
NA_TQ = NA_QROWS * GRID_W
NA_TK = NA_KROWS * GRID_W
NA_NEG = -1e30


NA_SEG = 256


def _na_norm_kernel(q_ref, k_ref, v_ref, qg_ref, kg_ref, seg_ref, qo_ref, ko_ref, vto_ref):
    for p in range(NA_W // LANE):
        pair = v_ref[0, :, p * LANE:(p + 1) * LANE].astype(F32)
        vto_ref[0, (2 * p) * HEAD_SLOT:(2 * p + 1) * HEAD_SLOT, :] = _value_slot_t(pair, NA_HD)
        vto_ref[0, (2 * p + 1) * HEAD_SLOT:(2 * p + 2) * HEAD_SLOT, :] = _value_slot_t(pltpu.roll(pair, NA_HD, 1), NA_HD)

    def norm(t_ref, g_ref, o_ref):
        for p in range(NA_W // NA_SEG):
            sl = slice(p * NA_SEG, (p + 1) * NA_SEG)
            t = t_ref[0, :, sl].astype(F32)
            sq = t * t
            hi = sq.astype(BF16)
            lo = (sq - hi.astype(F32)).astype(BF16)
            ms = (jnp.dot(hi, seg_ref[...], preferred_element_type=F32)
                  + jnp.dot(lo, seg_ref[...], preferred_element_type=F32))
            o_ref[0, :, sl] = (t * lax.rsqrt(ms + EPS) * g_ref[:, sl]).astype(BF16)

    norm(q_ref, qg_ref, qo_ref)
    norm(k_ref, kg_ref, ko_ref)


def _na_norm(u, qg, kg, tm=512):
    G, M, _ = u.shape
    tm = min(tm, M)
    assert M % tm == 0
    VS = NA_HEADS * HEAD_SLOT
    lane_head = np.arange(NA_SEG) // NA_HD
    seg = jnp.asarray((lane_head[:, None] == lane_head[None, :]) * (1.0 / NA_HD), BF16)
    col = lambda base: pl.BlockSpec((1, tm, NA_W), lambda b, i: (b, i, base // NA_W))
    return pl.pallas_call(
        _na_norm_kernel,
        out_shape=(jax.ShapeDtypeStruct((G, M, NA_W), BF16), jax.ShapeDtypeStruct((G, M, NA_W), BF16),
                   jax.ShapeDtypeStruct((G, VS, M), BF16)),
        grid=(G, M // tm),
        in_specs=[col(OD_NQ), col(OD_NK), col(OD_NV),
                  pl.BlockSpec((1, NA_W), lambda b, i: (0, 0)), pl.BlockSpec((1, NA_W), lambda b, i: (0, 0)),
                  pl.BlockSpec((NA_SEG, NA_SEG), lambda b, i: (0, 0))],
        out_specs=(pl.BlockSpec((1, tm, NA_W), lambda b, i: (b, i, 0)),
                   pl.BlockSpec((1, tm, NA_W), lambda b, i: (b, i, 0)),
                   pl.BlockSpec((1, VS, tm), lambda b, i: (b, 0, i))),
        compiler_params=_cparams(("parallel", "arbitrary"), 32),
        name="na_norm",
    )(u, u, u, qg, kg, seg)


def _na_kernel(q_ref, k0_ref, k1_ref, k2_ref, vt0_ref, vt1_ref, vt2_ref, kc_ref, vtc_ref, bias_ref, o_ref):
    lane_lo = lax.broadcasted_iota(jnp.int32, (NA_TQ, LANE), 1) < NA_HD
    k_refs = (k0_ref, k1_ref, k2_ref, kc_ref)
    vt_refs = (vt0_ref, vt1_ref, vt2_ref, vtc_ref)
    for h0 in range(0, NA_HEADS, ATTN_GROUP):
        heads = []
        for h in range(h0, h0 + ATTN_GROUP):
            sl = slice((h // 2) * LANE, (h // 2 + 1) * LANE)
            hs = slice(h * HEAD_SLOT, (h + 1) * HEAD_SLOT)
            qp = q_ref[0, :, sl]
            qm = jnp.where(lane_lo if h % 2 == 0 else jnp.logical_not(lane_lo), qp, jnp.zeros_like(qp))
            biases = [bias_ref[0, h, t * NA_TQ:(t + 1) * NA_TQ, :] for t in range(3)] + [None]
            heads.append((qm, [r[0, :, sl] for r in k_refs], [r[0, hs, :] for r in vt_refs], biases))
        outs = _attend_t(heads)
        for i in range(0, ATTN_GROUP, 2):
            hp = (h0 + i) // 2
            o_ref[0, :, hp * LANE:(hp + 1) * LANE] = _pair_output(outs[i:i + 2], NA_HD).astype(BF16)


def _na_bias_tables(rpb, rows):
    H, ndi, ndj = rpb.shape
    W = GRID_W
    half = NA_WIN_COLS - 1
    zeros = jnp.zeros((H, ndi, W - 1 - half), rpb.dtype)
    vec = jnp.concatenate([rpb[:, :, half:], zeros, zeros, rpb[:, :, :half]], axis=-1)
    toep = jnp.tile(vec, (1, 1, W))[:, :, :W * (2 * W - 2)].reshape(H, ndi, W, 2 * W - 2)[..., :W]
    qc = np.arange(W)[:, None]
    kc = np.arange(W)[None, :]
    c0 = np.clip(qc - NA_WIN_COLS // 2, 0, W - NA_WIN_COLS)
    toep = jnp.where(jnp.asarray((kc >= c0) & (kc < c0 + NA_WIN_COLS)), toep, NA_NEG)
    toep = jnp.swapaxes(toep, 2, 3)
    masked = jnp.full((H, W, W), NA_NEG, rpb.dtype)
    nblk = rows // NA_QROWS
    tabs = []
    for m in (0, 1, nblk - 1):
        kb = min(max(m - 1, 0), nblk - NA_KROWS // NA_QROWS)
        k_rows = []
        for i in range(NA_KROWS):
            kr = kb * NA_QROWS + i
            q_rows = []
            for a in range(NA_QROWS):
                qr = m * NA_QROWS + a
                r0 = min(max(qr - NA_WIN_ROWS // 2, 0), rows - NA_WIN_ROWS)
                q_rows.append(toep[:, kr - qr + NA_WIN_ROWS - 1] if r0 <= kr < r0 + NA_WIN_ROWS else masked)
            k_rows.append(jnp.stack(q_rows, axis=2))
        tabs.append(jnp.stack(k_rows, axis=1).reshape(H, NA_TK, NA_TQ))
    return jnp.stack(tabs, axis=0)


def _natten(q, k, vt, kc, vtc, bias):
    B, N, _ = q.shape
    Nc = kc.shape[1]
    nblk = N // NA_TQ
    kmax = nblk - NA_KROWS // NA_QROWS
    VS = NA_HEADS * HEAD_SLOT

    def kspec(t):
        return pl.BlockSpec((1, NA_TQ, NA_W), lambda m, b: (b, jnp.clip(m - 1, 0, kmax) + t, 0))

    def vspec(t):
        return pl.BlockSpec((1, VS, NA_TQ), lambda m, b: (b, 0, jnp.clip(m - 1, 0, kmax) + t))

    return pl.pallas_call(
        _na_kernel,
        out_shape=jax.ShapeDtypeStruct((B, N, NA_W), BF16),
        grid=(nblk, B),
        in_specs=[pl.BlockSpec((1, NA_TQ, NA_W), lambda m, b: (b, m, 0)),
                  kspec(0), kspec(1), kspec(2), vspec(0), vspec(1), vspec(2),
                  pl.BlockSpec((1, Nc, NA_W), lambda m, b: (b, 0, 0)),
                  pl.BlockSpec((1, VS, Nc), lambda m, b: (b, 0, 0)),
                  pl.BlockSpec((1, NA_HEADS, NA_TK, NA_TQ),
                               lambda m, b: (jnp.where(m == 0, 0, jnp.where(m == nblk - 1, 2, 1)), 0, 0, 0))],
        out_specs=pl.BlockSpec((1, NA_TQ, NA_W), lambda m, b: (b, m, 0)),
        compiler_params=_cparams(("arbitrary", "arbitrary"), 48),
        name="natten",
    )(q, k, k, k, vt, vt, vt, kc, vtc, bias)


def _place(w, layout, total):
    pieces, pos = [], 0
    for src, width, dst in sorted(layout, key=lambda item: item[2]):
        if dst > pos:
            pieces.append(jnp.zeros((w.shape[0], dst - pos), w.dtype))
        pieces.append(w[:, src:src + width])
        pos = dst + width
    if pos < total:
        pieces.append(jnp.zeros((w.shape[0], total - pos), w.dtype))
    return jnp.concatenate(pieces, axis=1)


def _ev_in_weight(w):
    lay = [(0, 256, EV_UQ), (256, 128, EV_UKV), (384, MLA_ROPE, EV_UKR + MLA_NOPE),
           (416, LRU_W, EV_UX), (416 + LRU_W, LRU_W, EV_UG)]
    return _place(w, lay, EV_PAD).astype(BF16)


def _od_in_weight(w):
    src = np.cumsum([0, 256, 256, 512, 512, 2 * GLA_LR, 512, 512, 512])
    dst = [OD_GQ, OD_GK, OD_GV, OD_GG, OD_GLR, OD_NQ, OD_NK, OD_NV]
    lay = [(int(src[i]), int(src[i + 1] - src[i]), dst[i]) for i in range(8)]
    return _place(w, lay, OD_PAD).astype(BF16)


def _pad_heads(w, heads, width):
    k = w.shape[0]
    return jnp.pad(w.reshape(k, heads, width), ((0, 0), (0, 0), (0, HEAD_SLOT - width))).reshape(k, heads * HEAD_SLOT)


def kernel(x, c, ctx, c_ctx, ada_w, ada_b, norm_mix, norm_mlp, w_out, mlp_w1, mlp_w2,
           ev_w_in, mla_q_norm, mla_w_uq, mla_kv_norm, mla_w_ukv, mla_q_gain, mla_k_gain,
           lru_conv_w, lru_conv_b, lru_w_a, lru_b_a, lru_w_x, lru_b_x, lru_lam,
           od_w_in, gla_w_a, gla_b_a, gla_o_gain, na_q_gain, na_k_gain, na_rpb):
    B, N, D = x.shape
    Nc = ctx.shape[1]
    depth = ada_w.shape[0]

    R = -(-(B + 1) // 8) * 8
    cc = jnp.concatenate([c, c_ctx[None], jnp.zeros((R - B - 1, D), c.dtype)], axis=0)
    mods = _modulation(cc, ada_w, ada_b).reshape(depth * R, 1, 6 * D)

    xl = x
    xc = ctx.reshape(1, B * Nc, D)
    for l in range(depth):
        last = l == depth - 1
        j = l // 2
        row_l, row_c = l * R, l * R + B
        w1 = mlp_w1[l].astype(BF16)
        w2 = mlp_w2[l].astype(BF16)
        wo = w_out[l].astype(BF16)
        if l % 2 == 0:
            w_in = _ev_in_weight(ev_w_in[j])
            ul = _norm_mod_matmul(xl, norm_mix[l], mods, row_l, w_in)
            uc = _norm_mod_matmul(xc, norm_mix[l], mods, row_c, w_in).reshape(B, Nc, EV_PAD)
            wuq = _pad_heads(mla_w_uq[j], MLA_HEADS, MLA_QK)
            wukv = mla_w_ukv[j].reshape(MLA_KV_LORA, MLA_HEADS, MLA_NOPE + MLA_V)
            wuk = _pad_heads(wukv[:, :, :MLA_NOPE].reshape(MLA_KV_LORA, -1), MLA_HEADS, MLA_NOPE).astype(BF16)
            wuv = _pad_heads(wukv[:, :, MLA_NOPE:].reshape(MLA_KV_LORA, -1), MLA_HEADS, MLA_V).astype(BF16)
            prep = functools.partial(_mla_prep, qn=mla_q_norm[j].reshape(1, -1), kvn=mla_kv_norm[j].reshape(1, -1),
                                     wuq=wuq.astype(BF16), wuqr=_rope_partner_weight(wuq).astype(BF16),
                                     wuk=wuk, wuv=wuv)
            ql, kl, vl = prep(ul, _rope_tables(N, mla_q_gain[j], mla_k_gain[j], True))
            qc, kc, vc = prep(uc, _rope_tables(Nc, mla_q_gain[j], mla_k_gain[j], False))
            ya_l = _mla_attention(ql, kl, vl, kc, vc)
            ya_c = None if last else _mla_attention(qc, None, None, kc, vc)
            wg, bg = _lru_gate_weights(lru_w_a[j], lru_b_a[j], lru_w_x[j], lru_b_x[j])
            yb_l, yb_c = _rglru(ul, uc, lru_conv_w[j], lru_conv_b[j], wg, bg, lru_lam[j])
            wa_w = MLA_HEADS * MLA_V
        else:
            w_in = _od_in_weight(od_w_in[j])
            ul = _norm_mod_matmul(xl, norm_mix[l], mods, row_l, w_in)
            uc = _norm_mod_matmul(xc, norm_mix[l], mods, row_c, w_in).reshape(B, Nc, OD_PAD)
            if not last:
                raise NotImplementedError("context outputs of the odd-layer mixers are only needed when depth > 2")
            wa = jnp.stack([jnp.pad(gla_w_a[j, d], ((d * GLA_LR, LANE - (d + 1) * GLA_LR), (0, 0))) for d in range(2)])
            ya_l = _gla(ul, uc, wa.astype(BF16), gla_b_a[j].reshape(2, 1, GLA_QK_W), gla_o_gain[j].reshape(1, GLA_DV))
            ya_c = None
            qg = jnp.tile(na_q_gain[j] * NA_HD ** -0.5, NA_HEADS).reshape(1, NA_W)
            kg = jnp.tile(na_k_gain[j], NA_HEADS).reshape(1, NA_W)
            nq, nk, nvt = _na_norm(ul, qg, kg)
            _, nkc, nvtc = _na_norm(uc, qg, kg)
            yb_l = _natten(nq, nk, nvt, nkc, nvtc, _na_bias_tables(na_rpb[j], N // GRID_W))
            yb_c = None
            wa_w = GLA_V_W
        xl = _out_proj_mlp(xl, ya_l, yb_l, wo[:wa_w], wo[wa_w:], norm_mlp[l], mods, row_l, w1, w2)
        if not last:
            xc = _out_proj_mlp(xc, ya_c.reshape(1, B * Nc, -1), yb_c.reshape(1, B * Nc, -1), wo[:wa_w], wo[wa_w:],
                               norm_mlp[l], mods, row_c, w1, w2)
    return xl
```

```python
import functools

import numpy as np
import jax
import jax.numpy as jnp
from jax import lax
from jax.experimental import pallas as pl
from jax.experimental.pallas import tpu as pltpu

F32 = jnp.float32
BF16 = jnp.bfloat16

D_MODEL = 1024
GRID_W = 64
EPS = 1e-6
ROPE_BASE = 10000.0
D_FF = 4 * D_MODEL

MLA_HEADS = 8
MLA_NOPE = 64
MLA_ROPE = 32
MLA_QK = MLA_NOPE + MLA_ROPE
MLA_V = 64
MLA_Q_LORA = 256
MLA_KV_LORA = 128

LRU_W = 512
LRU_BLOCKS = 8
LRU_BS = LRU_W // LRU_BLOCKS
LRU_C = 8.0
CONV_W = 4
CONV_LEFT = 2

GLA_HEADS = 4
GLA_DK = 64
GLA_DV = 128
GLA_LR = 16
GLA_TAU = 16.0
GLA_CHUNK = 64

NA_HD = 64
NA_HEADS = 8
NA_WIN_ROWS = 8
NA_WIN_COLS = 16

LANE = 128
HEAD_SLOT = 128

EV_UQ, EV_UKV, EV_UKR, EV_UX, EV_UG = 0, 256, 384, 512, 1024
EV_PAD = 1536
OD_GQ, OD_GK, OD_GV, OD_GG, OD_NQ, OD_NK, OD_NV, OD_GLR = 0, 256, 512, 1024, 1536, 2048, 2560, 3072
OD_PAD = 3200


def _cparams(semantics, vmem_mib):
    return pltpu.CompilerParams(dimension_semantics=semantics, vmem_limit_bytes=vmem_mib << 20)


def _rms(x, g):
    return x * lax.rsqrt(jnp.mean(x * x, axis=-1, keepdims=True) + EPS) * g


def _ada_kernel(c_ref, w_ref, b_ref, o_ref):
    cv = c_ref[...]
    s = cv * jax.nn.sigmoid(cv)
    o_ref[0] = jnp.dot(s.astype(BF16), w_ref[0].astype(BF16), preferred_element_type=F32) + b_ref[0]


def _modulation(cc, ada_w, ada_b):
    L, D, D6 = ada_w.shape
    R = cc.shape[0]
    tn = 1536
    return pl.pallas_call(
        _ada_kernel,
        out_shape=jax.ShapeDtypeStruct((L, R, D6), F32),
        grid=(L, D6 // tn),
        in_specs=[pl.BlockSpec((R, D), lambda l, j: (0, 0)),
                  pl.BlockSpec((1, D, tn), lambda l, j: (l, 0, j)),
                  pl.BlockSpec((1, 1, tn), lambda l, j: (l, 0, j))],
        out_specs=pl.BlockSpec((1, R, tn), lambda l, j: (l, 0, j)),
        compiler_params=_cparams(("arbitrary", "arbitrary"), 40),
        name="adaln_modulation",
    )(cc, ada_w, ada_b.reshape(L, 1, D6))


def _mod_spec(row0, k):
    return pl.BlockSpec((1, 1, D_MODEL), lambda g, i: (row0 + g, 0, k))


def _nmm_kernel(x_ref, g_ref, sh_ref, sc_ref, w_ref, o_ref):
    h = _rms(x_ref[0], g_ref[...]) * (1.0 + sc_ref[0]) + sh_ref[0]
    o_ref[0] = jnp.dot(h.astype(BF16), w_ref[...], preferred_element_type=F32).astype(o_ref.dtype)


def _norm_mod_matmul(x, g, mods, row0, w, tm=512):
    G, M, D = x.shape
    assert M % tm == 0
    Nout = w.shape[1]
    return pl.pallas_call(
        _nmm_kernel,
        out_shape=jax.ShapeDtypeStruct((G, M, Nout), BF16),
        grid=(G, M // tm),
        in_specs=[pl.BlockSpec((1, tm, D), lambda b, i: (b, i, 0)),
                  pl.BlockSpec((1, D), lambda b, i: (0, 0)),
                  _mod_spec(row0, 0), _mod_spec(row0, 1),
                  pl.BlockSpec((D, Nout), lambda b, i: (0, 0))],
        out_specs=pl.BlockSpec((1, tm, Nout), lambda b, i: (b, i, 0)),
        compiler_params=_cparams(("parallel", "arbitrary"), 48),
        name="norm_mod_in_proj",
    )(x, g.reshape(1, D), mods, mods, w)


MLP_TF = 1024


def _out_mlp_kernel(x_ref, ya_ref, yb_ref, woa_ref, wob_ref, g_ref, m2_ref, m3_ref, m4_ref, m5_ref,
                    w1_ref, w2_ref, o_ref, h_ref, a_ref):
    y = (jnp.dot(ya_ref[0], woa_ref[...], preferred_element_type=F32)
         + jnp.dot(yb_ref[0], wob_ref[...], preferred_element_type=F32))
    x1 = x_ref[0] + m2_ref[0] * y
    o_ref[0] = x1
    h_ref[...] = (_rms(x1, g_ref[...]) * (1.0 + m4_ref[0]) + m3_ref[0]).astype(BF16)
    for f in range(a_ref.shape[1] // MLP_TF):
        cols = slice(f * MLP_TF, (f + 1) * MLP_TF)
        a = jnp.maximum(jnp.dot(h_ref[...], w1_ref[:, cols], preferred_element_type=F32), 0.0)
        a_ref[:, cols] = (a * a).astype(BF16)
    o_ref[0] += m5_ref[0] * jnp.dot(a_ref[...], w2_ref[...], preferred_element_type=F32)


def _out_proj_mlp(x, ya, yb, w_out, g_mlp, mods, row0, w1, w2, layer, tm=512):
    G, M, D = x.shape
    assert M % tm == 0
    Wa, Wb = ya.shape[-1], yb.shape[-1]
    assert Wa == Wb and Wa + Wb == w_out.shape[1]
    FF = w1.shape[2]

    def resident(shape, row_block=0):
        return pl.BlockSpec((None,) + shape, lambda b, i: (layer, row_block, 0), pipeline_mode=pl.Buffered(1))

    return pl.pallas_call(
        _out_mlp_kernel,
        out_shape=jax.ShapeDtypeStruct((G, M, D), F32),
        grid=(G, M // tm),
        in_specs=[pl.BlockSpec((1, tm, D), lambda b, i: (b, i, 0)),
                  pl.BlockSpec((1, tm, Wa), lambda b, i: (b, i, 0)),
                  pl.BlockSpec((1, tm, Wb), lambda b, i: (b, i, 0)),
                  resident((Wa, D), 0), resident((Wb, D), 1),
                  pl.BlockSpec((1, D), lambda b, i: (0, 0)),
                  _mod_spec(row0, 2), _mod_spec(row0, 3), _mod_spec(row0, 4), _mod_spec(row0, 5),
                  resident((D, FF)), resident((FF, D))],
        out_specs=pl.BlockSpec((1, tm, D), lambda b, i: (b, i, 0)),
        scratch_shapes=[pltpu.VMEM((tm, D), BF16), pltpu.VMEM((tm, FF), BF16)],
        compiler_params=_cparams(("parallel", "arbitrary"), 52),
        name="out_proj_mlp",
    )(x, ya, yb, w_out, w_out, g_mlp.reshape(1, D), mods, mods, mods, mods, w1, w2)


def _mla_prep_kernel(uq_ref, ukv_ref, ukr_ref, tq1_ref, tq2_ref, tk1_ref, tka_ref, tkb_ref, qn_ref, kvn_ref,
                     wuq_ref, wuqr_ref, wuk_ref, wuv_ref, q_out, k_out, vt_out):
    qn = _rms(uq_ref[0].astype(F32), qn_ref[...]).astype(BF16)
    kvn = _rms(ukv_ref[0].astype(F32), kvn_ref[...]).astype(BF16)
    q_all = jnp.dot(qn, wuq_ref[...], preferred_element_type=F32)
    q_rot = jnp.dot(qn, wuqr_ref[...], preferred_element_type=F32)
    k_all = jnp.dot(kvn, wuk_ref[...], preferred_element_type=F32)
    v_all = jnp.dot(kvn, wuv_ref[...], preferred_element_type=F32)
    ukr = ukr_ref[0].astype(F32)
    quarter = MLA_ROPE // 4
    k_rot = (pltpu.roll(ukr, HEAD_SLOT - quarter, 1) * tka_ref[...] + pltpu.roll(ukr, quarter, 1) * tkb_ref[...])
    ukr_sq = jnp.sum(ukr * ukr, axis=-1, keepdims=True)
    inv_n = 1.0 / MLA_QK
    for h in range(MLA_HEADS):
        sl = slice(h * HEAD_SLOT, (h + 1) * HEAD_SLOT)
        qh = q_all[:, sl]
        rq = lax.rsqrt(jnp.sum(qh * qh, axis=-1, keepdims=True) * inv_n + EPS)
        q_out[0, :, sl] = (rq * (qh * tq1_ref[...] + q_rot[:, sl] * tq2_ref[...])).astype(BF16)
        kn = k_all[:, sl]
        rk = lax.rsqrt((jnp.sum(kn * kn, axis=-1, keepdims=True) + ukr_sq) * inv_n + EPS)
        k_out[0, :, sl] = (rk * ((kn + ukr) * tk1_ref[...] + k_rot)).astype(BF16)
        vt_out[0, sl, :] = _value_slot_t(v_all[:, sl], MLA_V)


def _rope_partner(t):
    quarter = MLA_ROPE // 4
    t4 = t.reshape(t.shape[:-1] + (2, 2, quarter))
    return jnp.stack([t4[..., 1, :], t4[..., 0, :]], axis=-2).reshape(t.shape)


def _rope_tables(n, q_gain, k_gain, use_rope):
    quarter = MLA_ROPE // 4
    if use_rope:
        pos = jnp.arange(n)
        inv = ROPE_BASE ** (-jnp.arange(0, MLA_ROPE // 2, 2, dtype=F32) / (MLA_ROPE // 2))
        ang_r = (pos // GRID_W).astype(F32)[:, None] * inv[None, :]
        ang_c = (pos % GRID_W).astype(F32)[:, None] * inv[None, :]
        ang = jnp.concatenate([ang_r, ang_r, ang_c, ang_c], axis=-1)
        cos, sin = jnp.cos(ang), jnp.sin(ang)
    else:
        cos, sin = jnp.ones((n, MLA_ROPE), F32), jnp.zeros((n, MLA_ROPE), F32)
    first = (np.arange(MLA_ROPE) % (2 * quarter)) < quarter
    pad = jnp.zeros((n, HEAD_SLOT - MLA_QK), F32)

    def slot(nope, rope):
        return jnp.concatenate([jnp.broadcast_to(nope, (n, MLA_NOPE)), rope, pad], axis=-1)

    zero = jnp.zeros((MLA_NOPE,), F32)
    qg, kg = q_gain * MLA_QK ** -0.5, k_gain
    tq1 = slot(qg[:MLA_NOPE], cos * qg[MLA_NOPE:])
    tq2 = slot(zero, sin * _rope_partner(qg[MLA_NOPE:]))
    tk1 = slot(kg[:MLA_NOPE], cos * kg[MLA_NOPE:])
    ksin = sin * _rope_partner(kg[MLA_NOPE:])
    tka = slot(zero, jnp.where(first, -ksin, 0.0))
    tkb = slot(zero, jnp.where(first, 0.0, ksin))
    return tq1, tq2, tk1, tka, tkb


def _rope_partner_weight(wuq):
    k = wuq.shape[0]
    quarter = MLA_ROPE // 4
    w = wuq.reshape(k, MLA_HEADS, HEAD_SLOT)
    rope = w[:, :, MLA_NOPE:MLA_QK].reshape(k, MLA_HEADS, 2, 2, quarter)
    rot = jnp.stack([-rope[:, :, :, 1, :], rope[:, :, :, 0, :]], axis=3).reshape(k, MLA_HEADS, MLA_ROPE)
    out = jnp.concatenate([jnp.zeros_like(w[:, :, :MLA_NOPE]), rot, jnp.zeros_like(w[:, :, MLA_QK:])], axis=-1)
    return out.reshape(k, MLA_HEADS * HEAD_SLOT)


def _mla_prep(u, tables, qn, kvn, wuq, wuqr, wuk, wuv, tm=256):
    G, M, _ = u.shape
    assert M % tm == 0
    HS = MLA_HEADS * HEAD_SLOT
    full = lambda shape: pl.BlockSpec(shape, lambda b, i: (0,) * len(shape))
    tab = pl.BlockSpec((tm, HEAD_SLOT), lambda b, i: (i, 0))
    return pl.pallas_call(
        _mla_prep_kernel,
        out_shape=(jax.ShapeDtypeStruct((G, M, HS), BF16), jax.ShapeDtypeStruct((G, M, HS), BF16),
                   jax.ShapeDtypeStruct((G, HS, M), BF16)),
        grid=(G, M // tm),
        in_specs=[pl.BlockSpec((1, tm, MLA_Q_LORA), lambda b, i: (b, i, EV_UQ // MLA_Q_LORA)),
                  pl.BlockSpec((1, tm, MLA_KV_LORA), lambda b, i: (b, i, EV_UKV // MLA_KV_LORA)),
                  pl.BlockSpec((1, tm, HEAD_SLOT), lambda b, i: (b, i, EV_UKR // HEAD_SLOT)),
                  tab, tab, tab, tab, tab,
                  full((1, MLA_Q_LORA)), full((1, MLA_KV_LORA)),
                  full((MLA_Q_LORA, HS)), full((MLA_Q_LORA, HS)), full((MLA_KV_LORA, HS)), full((MLA_KV_LORA, HS))],
        out_specs=(pl.BlockSpec((1, tm, HS), lambda b, i: (b, i, 0)),
                   pl.BlockSpec((1, tm, HS), lambda b, i: (b, i, 0)),
                   pl.BlockSpec((1, HS, tm), lambda b, i: (b, 0, i))),
        compiler_params=_cparams(("parallel", "arbitrary"), 40),
        name="mla_prep",
    )(u, u, u, *tables, qn, kvn, wuq, wuqr, wuk, wuv)


def _dot_nt(a, b):
    return lax.dot_general(a, b, (((1,), (1,)), ((), ())), preferred_element_type=F32)


def _value_slot_t(v, ones_row):
    lane = lax.broadcasted_iota(jnp.int32, v.shape, 1)
    slot = jnp.where(lane < ones_row, v, jnp.where(lane == ones_row, 1.0, 0.0))
    return jnp.transpose(slot).astype(BF16)


ATTN_GROUP = 8


def _col_max(a, rows=8):
    parts = [a[r:r + rows] for r in range(0, a.shape[0], rows)]
    while len(parts) > 1:
        parts = [jnp.maximum(parts[i], parts[i + 1]) if i + 1 < len(parts) else parts[i]
                 for i in range(0, len(parts), 2)]
    return jnp.max(parts[0], axis=0, keepdims=True)


def _attend_t(heads):
    scores = []
    for q, keys, _, biases in heads:
        s = [_dot_nt(k, q) for k in keys]
        if biases is not None:
            s = [a if b is None else a + b for a, b in zip(s, biases)]
        scores.append(s)
    maxes = [functools.reduce(jnp.maximum, [_col_max(a) for a in s]) for s in scores]
    outs = []
    for (_, _, values_t, _), s, m in zip(heads, scores, maxes):
        out_t = None
        for a, vt in zip(s, values_t):
            part = jnp.dot(vt, jnp.exp(a - m).astype(BF16), preferred_element_type=F32)
            out_t = part if out_t is None else out_t + part
        outs.append(out_t)
    return outs


def _pair_output(slots_t, dv):
    halves = [t[:dv] / t[dv:dv + 1] for t in slots_t]
    return jnp.transpose(jnp.concatenate(halves, axis=0))


def _mla_attn_kernel(*refs, with_latent):
    if with_latent:
        q_ref, kl_ref, vtl_ref, kc_ref, vtc_ref, o_ref = refs
    else:
        q_ref, kc_ref, vtc_ref, o_ref = refs
    for h0 in range(0, MLA_HEADS, ATTN_GROUP):
        heads = []
        for h in range(h0, h0 + ATTN_GROUP):
            sl = slice(h * HEAD_SLOT, (h + 1) * HEAD_SLOT)
            keys, values_t = [kc_ref[0, :, sl]], [vtc_ref[0, sl, :]]
            if with_latent:
                keys.append(kl_ref[0, :, sl])
                values_t.append(vtl_ref[0, sl, :])
            heads.append((q_ref[0, :, sl], keys, values_t, None))
        outs = _attend_t(heads)
        for i in range(0, ATTN_GROUP, 2):
            hp = (h0 + i) // 2
            o_ref[0, :, hp * LANE:(hp + 1) * LANE] = _pair_output(outs[i:i + 2], MLA_V).astype(BF16)


def _mla_attention(q, kl, vtl, kc, vtc, tq=256):
    B, M, HS = q.shape
    assert M % tq == 0
    HV = MLA_HEADS * MLA_V
    Nc = kc.shape[1]
    with_latent = kl is not None
    whole = lambda n, w: pl.BlockSpec((1, n, w), lambda b, i: (b, 0, 0))
    in_specs = [pl.BlockSpec((1, tq, HS), lambda b, i: (b, i, 0))]
    args = [q]
    if with_latent:
        in_specs += [whole(kl.shape[1], HS), whole(HS, kl.shape[1])]
        args += [kl, vtl]
    in_specs += [whole(Nc, HS), whole(HS, Nc)]
    args += [kc, vtc]
    return pl.pallas_call(
        functools.partial(_mla_attn_kernel, with_latent=with_latent),
        out_shape=jax.ShapeDtypeStruct((B, M, HV), BF16),
        grid=(B, M // tq),
        in_specs=in_specs,
        out_specs=pl.BlockSpec((1, tq, HV), lambda b, i: (b, i, 0)),
        compiler_params=_cparams(("parallel", "arbitrary"), 48),
        name="mla_attention",
    )(*args)


LRU_CW = 256
LRU_HALO = 16
LRU_TN = 256


def _gelu_tanh(x):
    return 0.5 * x * (1.0 + jnp.tanh(0.7978845608028654 * (x + 0.044715 * (x * x * x))))


def _scan_group(a, bv, h, reverse):
    row = lax.broadcasted_iota(jnp.int32, a.shape, 0)
    for s in (1, 2, 4):
        if reverse:
            keep = row < 8 - s
            shift = 8 - s
        else:
            keep = row >= s
            shift = s
        a_s = jnp.where(keep, pltpu.roll(a, shift, 0), 1.0)
        b_s = jnp.where(keep, pltpu.roll(bv, shift, 0), 0.0)
        bv = a * b_s + bv
        a = a * a_s
    hs = a * h + bv
    return hs, (hs[0:1, :] if reverse else hs[7:8, :])


def _lru_kernel(uxl_ref, ugl_ref, uxc_ref, ugc_ref, cw_ref, cb_ref, wg_ref, bg_ref, lam_ref, yl_ref, yc_ref,
                xpl_ref, xpc_ref, xcv_ref, af_ref, bf_ref, ab_ref, bb_ref):
    N, Nc = uxl_ref.shape[1], uxc_ref.shape[1]
    NT = N + Nc
    C = LRU_CW
    H = LRU_HALO

    def conv(src_ref, pad_ref, n, row0):
        pad_ref[0:H, :] = jnp.zeros((H, C), F32)
        pad_ref[H + n:H + n + H, :] = jnp.zeros((H, C), F32)
        pad_ref[H:H + n, :] = src_ref[0].astype(F32)
        y = cb_ref[...] + pad_ref[H - CONV_LEFT:H - CONV_LEFT + n, :] * cw_ref[0:1, :]
        for j in range(1, CONV_W):
            y = y + pad_ref[H - CONV_LEFT + j:H - CONV_LEFT + j + n, :] * cw_ref[j:j + 1, :]
        xcv_ref[row0:row0 + n, :] = y

    conv(uxc_ref, xpc_ref, Nc, 0)
    conv(uxl_ref, xpl_ref, N, Nc)

    lam = lam_ref[...]
    c_half = (-0.5 * LRU_C) * (jnp.maximum(-lam, 0.0) + jnp.log1p(jnp.exp(-jnp.abs(lam))))

    def coeff_chunk(i, carry):
        r0 = pl.multiple_of(i * LRU_TN, LRU_TN)
        x = xcv_ref[pl.ds(r0, LRU_TN), :]
        t = jnp.tanh(jnp.dot(x.astype(BF16), wg_ref[0], preferred_element_type=F32) + bg_ref[0])
        hx = 0.5 * x
        for d, (a_ref, b_ref) in enumerate(((af_ref, bf_ref), (ab_ref, bb_ref))):
            c = c_half[d:d + 1, :]
            a = jnp.exp(c * t[:, (2 * d) * C:(2 * d + 1) * C] + c)
            a_ref[pl.ds(r0, LRU_TN), :] = a
            gated_x = hx * t[:, (2 * d + 1) * C:(2 * d + 2) * C] + hx
            b_ref[pl.ds(r0, LRU_TN), :] = jnp.sqrt(1.0 - a * a) * gated_x
        return carry

    lax.fori_loop(0, NT // LRU_TN, coeff_chunk, 0)

    ngc, ngt = Nc // 8, NT // 8

    def scan_step(i, carry):
        hf, hb = carry
        rf = pl.multiple_of(i * 8, 8)
        rb = pl.multiple_of(jnp.where(i < ngc, ngc - 1 - i, ngt + ngc - 1 - i) * 8, 8)
        hs_f, hf = _scan_group(af_ref[pl.ds(rf, 8), :], bf_ref[pl.ds(rf, 8), :], hf, False)
        hs_b, hb = _scan_group(ab_ref[pl.ds(rb, 8), :], bb_ref[pl.ds(rb, 8), :], hb, True)
        bf_ref[pl.ds(rf, 8), :] = hs_f
        bb_ref[pl.ds(rb, 8), :] = hs_b
        return hf, hb

    zero = jnp.zeros((1, C), F32)
    lax.fori_loop(0, ngt, scan_step, (zero, zero), unroll=4)

    def out_chunk(i, carry):
        r0 = pl.multiple_of(i * LRU_TN, LRU_TN)
        hsum = bf_ref[pl.ds(Nc + r0, LRU_TN), :] + bb_ref[pl.ds(Nc + r0, LRU_TN), :]
        gate = _gelu_tanh(ugl_ref[0, pl.ds(r0, LRU_TN), :].astype(F32))
        yl_ref[0, pl.ds(r0, LRU_TN), :] = (hsum * gate).astype(BF16)
        return carry

    lax.fori_loop(0, N // LRU_TN, out_chunk, 0)
    yc_ref[0] = ((bf_ref[0:Nc, :] + bb_ref[0:Nc, :]) * _gelu_tanh(ugc_ref[0].astype(F32))).astype(BF16)


def _rglru(ul, uc, conv_w, conv_b, wg, bg, lam):
    B, N, _ = ul.shape
    Nc = uc.shape[1]
    C = LRU_CW
    nh = LRU_W // C
    NT = N + Nc
    col = lambda base: (lambda b, j: (b, 0, base // C + j))
    par = lambda rows: pl.BlockSpec((rows, C), lambda b, j: (0, j))
    return pl.pallas_call(
        _lru_kernel,
        out_shape=(jax.ShapeDtypeStruct((B, N, LRU_W), BF16), jax.ShapeDtypeStruct((B, Nc, LRU_W), BF16)),
        grid=(B, nh),
        in_specs=[pl.BlockSpec((1, N, C), col(EV_UX)), pl.BlockSpec((1, N, C), col(EV_UG)),
                  pl.BlockSpec((1, Nc, C), col(EV_UX)), pl.BlockSpec((1, Nc, C), col(EV_UG)),
                  par(CONV_W), par(1),
                  pl.BlockSpec((1, C, 4 * C), lambda b, j: (j, 0, 0)),
                  pl.BlockSpec((1, 1, 4 * C), lambda b, j: (j, 0, 0)),
                  par(2)],
        out_specs=(pl.BlockSpec((1, N, C), lambda b, j: (b, 0, j)),
                   pl.BlockSpec((1, Nc, C), lambda b, j: (b, 0, j))),
        scratch_shapes=[pltpu.VMEM((N + 2 * LRU_HALO, C), F32), pltpu.VMEM((Nc + 2 * LRU_HALO, C), F32),
                        pltpu.VMEM((NT, C), F32)] + [pltpu.VMEM((NT, C), F32)] * 4,
        compiler_params=_cparams(("parallel", "arbitrary"), 48),
        name="rglru",
    )(ul, ul, uc, uc, conv_w, conv_b.reshape(1, LRU_W), wg, bg, lam)


def _lru_gate_weights(w_a, b_a, w_x, b_x):
    C = LRU_CW
    nh = LRU_W // C
    kb = C // LRU_BS

    def dense(w):
        w = w.reshape(nh, kb, LRU_BS, LRU_BS)
        eye = jnp.eye(kb, dtype=w.dtype)
        return jnp.einsum('hkij,kl->hkilj', w, eye).reshape(nh, C, C)

    wg = jnp.concatenate([dense(w_a[0]), dense(w_x[0]), dense(w_a[1]), dense(w_x[1])], axis=-1)
    bg = jnp.stack([b_a[0], b_x[0], b_a[1], b_x[1]], axis=0).reshape(4, nh, C)
    bg = jnp.transpose(bg, (1, 0, 2)).reshape(nh, 1, 4 * C)
    return (0.5 * wg).astype(BF16), 0.5 * bg


GLA_SC = 256
GLA_UNROLL = 2
GLA_QK_W = GLA_HEADS * GLA_DK
GLA_V_W = GLA_HEADS * GLA_DV


def _split3(x):
    hi = x.astype(BF16)
    r1 = x - hi.astype(F32)
    mid = r1.astype(BF16)
    lo = (r1 - mid.astype(F32)).astype(BF16)
    return hi, mid, lo


def _gla_kernel(ql_ref, kl_ref, vl_ref, gl_ref, lrl_ref, kc_ref, vc_ref, lrc_ref, wa_ref, ba_ref, og_ref,
                y_ref, oacc_ref, st_ref):
    N, Nc = ql_ref.shape[1], kc_ref.shape[1]
    T = GLA_SC
    CH = GLA_CHUNK
    npair = GLA_HEADS // 2
    row = lax.broadcasted_iota(jnp.int32, (T, T), 0)
    colm = lax.broadcasted_iota(jnp.int32, (T, T), 1)
    same_chunk = (row // CH) == (colm // CH)
    causal = (same_chunk & (colm <= row), same_chunk & (colm >= row))
    tri = tuple(jnp.where(c, 1.0, 0.0).astype(BF16) for c in causal)
    lane_lo = lax.broadcasted_iota(jnp.int32, (T, LANE), 1) < GLA_DK
    srow = lax.broadcasted_iota(jnp.int32, (LANE, 2 * GLA_DV), 0)
    scol = lax.broadcasted_iota(jnp.int32, (LANE, 2 * GLA_DV), 1)
    state_mask = (srow < GLA_DK) == (scol < GLA_DV)
    qscale = GLA_DK ** -0.5

    def superchunks(streams, q_ref, k_ref, v_ref, lr_ref, with_output):
        nch = T // CH
        work = []
        for d, r0 in streams:
            z = jnp.dot(lr_ref[0, pl.ds(r0, T), :], wa_ref[d], preferred_element_type=F32) + ba_ref[d]
            work.append(dict(d=d, r0=r0, split=_split3(jax.nn.log_sigmoid(z) * (1.0 / GLA_TAU))))
        for w in work:
            d = w['d']
            cum = sum(jnp.dot(tri[d], t, preferred_element_type=F32) for t in w['split'])
            last = (CH - 1) if d == 0 else 0
            tot = [cum[c * CH + last:c * CH + last + 1, :] for c in range(nch)]
            tot_rows = jnp.concatenate([jnp.broadcast_to(t, (CH, GLA_QK_W)) for t in tot], axis=0)
            k = k_ref[0, pl.ds(w['r0'], T), :].astype(F32)
            w.update(cum=cum, tot=tot, k=k, v=v_ref[0, pl.ds(w['r0'], T), :],
                     k_dec=(k * jnp.exp(tot_rows - cum)).astype(BF16))
        for w in work:
            ds, dec = {}, {}
            for c in range(nch):
                rs = slice(c * CH, (c + 1) * CH)
                for p in range(npair):
                    kd = w['k_dec'][rs, p * LANE:(p + 1) * LANE]
                    vv = w['v'][rs, p * 2 * GLA_DV:(p + 1) * 2 * GLA_DV]
                    kv = lax.dot_general(kd, vv, (((0,), (0,)), ((), ())), preferred_element_type=F32)
                    ds[c, p] = jnp.where(state_mask, kv, 0.0)
                    dec_row = jnp.exp(w['tot'][c][:, p * LANE:(p + 1) * LANE])
                    dec_col = jnp.transpose(jnp.broadcast_to(dec_row, (LANE, LANE)))
                    dec[c, p] = jnp.concatenate([dec_col, dec_col], axis=1)
            w.update(ds=ds, dec=dec)
        for w in work:
            d = w['d']
            order = range(nch) if d == 0 else range(nch - 1, -1, -1)
            s_in = {}
            for p in range(npair):
                s = st_ref[d, p]
                for c in order:
                    s_in[c, p] = s.astype(BF16)
                    s = w['dec'][c, p] * s + w['ds'][c, p]
                st_ref[d, p] = s
            w.update(s_in=s_in)
        if not with_output:
            return
        for w in work:
            q = q_ref[0, pl.ds(w['r0'], T), :].astype(F32)
            w.update(q_dec=((q * qscale) * jnp.exp(w['cum'])).astype(BF16),
                     k_inv=(w['k'] * jnp.exp(-w['cum'])).astype(BF16))
        for w in work:
            d, r0 = w['d'], w['r0']
            for p in range(npair):
                qp = w['q_dec'][:, p * LANE:(p + 1) * LANE]
                kp = w['k_inv'][:, p * LANE:(p + 1) * LANE]
                o_inter = jnp.concatenate(
                    [jnp.dot(qp[c * CH:(c + 1) * CH], w['s_in'][c, p], preferred_element_type=F32)
                     for c in range(nch)], axis=0)
                for j in range(2):
                    h = 2 * p + j
                    qm = jnp.where(lane_lo if j == 0 else jnp.logical_not(lane_lo), qp, jnp.zeros_like(qp))
                    att = jnp.where(causal[d], _dot_nt(qm, kp), 0.0).astype(BF16)
                    o = (jnp.dot(att, w['v'][:, h * GLA_DV:(h + 1) * GLA_DV], preferred_element_type=F32)
                         + o_inter[:, j * GLA_DV:(j + 1) * GLA_DV])
                    oacc_ref[d, pl.ds(r0, T), h * GLA_DV:(h + 1) * GLA_DV] = o

    st_ref[...] = jnp.zeros_like(st_ref)
    ncs, nls = Nc // T, N // T

    def ctx_body(i, carry):
        streams = [(0, pl.multiple_of(i * T, T)), (1, pl.multiple_of((ncs - 1 - i) * T, T))]
        superchunks(streams, None, kc_ref, vc_ref, lrc_ref, False)
        return carry

    def lat_body(i, carry):
        streams = []
        for u in range(GLA_UNROLL):
            streams += [(0, pl.multiple_of((GLA_UNROLL * i + u) * T, T)),
                        (1, pl.multiple_of((nls - 1 - GLA_UNROLL * i - u) * T, T))]
        superchunks(streams, ql_ref, kl_ref, vl_ref, lrl_ref, True)
        return carry

    lax.fori_loop(0, ncs, ctx_body, 0)
    lax.fori_loop(0, nls // GLA_UNROLL, lat_body, 0)

    def fin(i, carry):
        r0 = pl.multiple_of(i * T, T)
        g = gl_ref[0, pl.ds(r0, T), :].astype(F32)
        for h in range(GLA_HEADS):
            sl = slice(h * GLA_DV, (h + 1) * GLA_DV)
            gh = g[:, sl]
            o = oacc_ref[0, pl.ds(r0, T), sl] + oacc_ref[1, pl.ds(r0, T), sl]
            y_ref[0, pl.ds(r0, T), sl] = (_rms(o, og_ref[...]) * (gh * jax.nn.sigmoid(gh))).astype(BF16)
        return carry

    lax.fori_loop(0, N // T, fin, 0)


def _gla(ul, uc, wa, ba, o_gain):
    B, N, _ = ul.shape
    Nc = uc.shape[1]
    blk = lambda n, w, base: pl.BlockSpec((1, n, w), lambda b: (b, 0, base // w))
    full = lambda shape: pl.BlockSpec(shape, lambda b: (0,) * len(shape))
    return pl.pallas_call(
        _gla_kernel,
        out_shape=jax.ShapeDtypeStruct((B, N, GLA_V_W), BF16),
        grid=(B,),
        in_specs=[blk(N, GLA_QK_W, OD_GQ), blk(N, GLA_QK_W, OD_GK), blk(N, GLA_V_W, OD_GV),
                  blk(N, GLA_V_W, OD_GG), blk(N, LANE, OD_GLR),
                  blk(Nc, GLA_QK_W, OD_GK), blk(Nc, GLA_V_W, OD_GV), blk(Nc, LANE, OD_GLR),
                  full((2, LANE, GLA_QK_W)), full((2, 1, GLA_QK_W)), full((1, GLA_DV))],
        out_specs=pl.BlockSpec((1, N, GLA_V_W), lambda b: (b, 0, 0)),
        scratch_shapes=[pltpu.VMEM((2, N, GLA_V_W), F32),
                        pltpu.VMEM((2, GLA_HEADS // 2, LANE, 2 * GLA_DV), F32)],
        compiler_params=_cparams(("parallel",), 48),
        name="gla",
    )(ul, ul, ul, ul, ul, uc, uc, uc, wa, ba, o_gain)


NA_W = NA_HEADS * NA_HD
NA_QROWS = 4
NA_KROWS = 12
NA_TQ = NA_QROWS * GRID_W
NA_TK = NA_KROWS * GRID_W
NA_NEG = -1e30


NA_SEG = 256


def _na_norm_kernel(q_ref, k_ref, v_ref, qg_ref, kg_ref, seg_ref, qo_ref, ko_ref, vto_ref):
    for p in range(NA_W // LANE):
        pair = v_ref[0, :, p * LANE:(p + 1) * LANE].astype(F32)
        vto_ref[0, (2 * p) * HEAD_SLOT:(2 * p + 1) * HEAD_SLOT, :] = _value_slot_t(pair, NA_HD)
        vto_ref[0, (2 * p + 1) * HEAD_SLOT:(2 * p + 2) * HEAD_SLOT, :] = _value_slot_t(pltpu.roll(pair, NA_HD, 1), NA_HD)

    def norm(t_ref, g_ref, o_ref):
        for p in range(NA_W // NA_SEG):
            sl = slice(p * NA_SEG, (p + 1) * NA_SEG)
            t = t_ref[0, :, sl].astype(F32)
            sq = t * t
            hi = sq.astype(BF16)
            lo = (sq - hi.astype(F32)).astype(BF16)
            ms = (jnp.dot(hi, seg_ref[...], preferred_element_type=F32)
                  + jnp.dot(lo, seg_ref[...], preferred_element_type=F32))
            o_ref[0, :, sl] = (t * lax.rsqrt(ms + EPS) * g_ref[:, sl]).astype(BF16)

    norm(q_ref, qg_ref, qo_ref)
    norm(k_ref, kg_ref, ko_ref)


def _na_norm(u, qg, kg, tm=512):
    G, M, _ = u.shape
    tm = min(tm, M)
    assert M % tm == 0
    VS = NA_HEADS * HEAD_SLOT
    lane_head = np.arange(NA_SEG) // NA_HD
    seg = jnp.asarray((lane_head[:, None] == lane_head[None, :]) * (1.0 / NA_HD), BF16)
    col = lambda base: pl.BlockSpec((1, tm, NA_W), lambda b, i: (b, i, base // NA_W))
    return pl.pallas_call(
        _na_norm_kernel,
        out_shape=(jax.ShapeDtypeStruct((G, M, NA_W), BF16), jax.ShapeDtypeStruct((G, M, NA_W), BF16),
                   jax.ShapeDtypeStruct((G, VS, M), BF16)),
        grid=(G, M // tm),
        in_specs=[col(OD_NQ), col(OD_NK), col(OD_NV),
                  pl.BlockSpec((1, NA_W), lambda b, i: (0, 0)), pl.BlockSpec((1, NA_W), lambda b, i: (0, 0)),
                  pl.BlockSpec((NA_SEG, NA_SEG), lambda b, i: (0, 0))],
        out_specs=(pl.BlockSpec((1, tm, NA_W), lambda b, i: (b, i, 0)),
                   pl.BlockSpec((1, tm, NA_W), lambda b, i: (b, i, 0)),
                   pl.BlockSpec((1, VS, tm), lambda b, i: (b, 0, i))),
        compiler_params=_cparams(("parallel", "arbitrary"), 32),
        name="na_norm",
    )(u, u, u, qg, kg, seg)


def _na_kernel(q_ref, k0_ref, k1_ref, k2_ref, vt0_ref, vt1_ref, vt2_ref, kc_ref, vtc_ref, bias_ref, o_ref):
    lane_lo = lax.broadcasted_iota(jnp.int32, (NA_TQ, LANE), 1) < NA_HD
    k_refs = (k0_ref, k1_ref, k2_ref, kc_ref)
    vt_refs = (vt0_ref, vt1_ref, vt2_ref, vtc_ref)
    for h0 in range(0, NA_HEADS, ATTN_GROUP):
        heads = []
        for h in range(h0, h0 + ATTN_GROUP):
            sl = slice((h // 2) * LANE, (h // 2 + 1) * LANE)
            hs = slice(h * HEAD_SLOT, (h + 1) * HEAD_SLOT)
            qp = q_ref[0, :, sl]
            qm = jnp.where(lane_lo if h % 2 == 0 else jnp.logical_not(lane_lo), qp, jnp.zeros_like(qp))
            biases = [bias_ref[0, h, t * NA_TQ:(t + 1) * NA_TQ, :] for t in range(3)] + [None]
            heads.append((qm, [r[0, :, sl] for r in k_refs], [r[0, hs, :] for r in vt_refs], biases))
        outs = _attend_t(heads)
        for i in range(0, ATTN_GROUP, 2):
            hp = (h0 + i) // 2
            o_ref[0, :, hp * LANE:(hp + 1) * LANE] = _pair_output(outs[i:i + 2], NA_HD).astype(BF16)


def _na_bias_tables(rpb, rows):
    H, ndi, ndj = rpb.shape
    W = GRID_W
    half = NA_WIN_COLS - 1
    zeros = jnp.zeros((H, ndi, W - 1 - half), rpb.dtype)
    vec = jnp.concatenate([rpb[:, :, half:], zeros, zeros, rpb[:, :, :half]], axis=-1)
    toep = jnp.tile(vec, (1, 1, W))[:, :, :W * (2 * W - 2)].reshape(H, ndi, W, 2 * W - 2)[..., :W]
    qc = np.arange(W)[:, None]
    kc = np.arange(W)[None, :]
    c0 = np.clip(qc - NA_WIN_COLS // 2, 0, W - NA_WIN_COLS)
    toep = jnp.where(jnp.asarray((kc >= c0) & (kc < c0 + NA_WIN_COLS)), toep, NA_NEG)
    toep = jnp.swapaxes(toep, 2, 3)
    nblk = rows // NA_QROWS
    guard = NA_KROWS - NA_WIN_ROWS
    toep = jnp.pad(toep, ((0, 0), (guard, guard), (0, 0), (0, 0)), constant_values=NA_NEG)
    tabs = []
    for m in (0, 1, nblk - 1):
        kb = min(max(m - 1, 0), nblk - NA_KROWS // NA_QROWS)
        qr = m * NA_QROWS + np.arange(NA_QROWS)[None, :]
        kr = kb * NA_QROWS + np.arange(NA_KROWS)[:, None]
        r0 = np.clip(qr - NA_WIN_ROWS // 2, 0, rows - NA_WIN_ROWS)
        row_ok = (kr >= r0) & (kr < r0 + NA_WIN_ROWS)
        di0 = (kb - m) * NA_QROWS + NA_WIN_ROWS - 1 + guard
        per_a = [toep[:, di0 - a:di0 - a + NA_KROWS] for a in range(NA_QROWS)]
        tab = jnp.where(jnp.asarray(row_ok)[None, :, None, :, None], jnp.stack(per_a, axis=3), NA_NEG)
        tabs.append(tab.reshape(H, NA_TK, NA_TQ))
    return jnp.stack(tabs, axis=0)


def _natten(q, k, vt, kc, vtc, bias):
    B, N, _ = q.shape
    Nc = kc.shape[1]
    nblk = N // NA_TQ
    kmax = nblk - NA_KROWS // NA_QROWS
    VS = NA_HEADS * HEAD_SLOT

    def kspec(t):
        return pl.BlockSpec((1, NA_TQ, NA_W), lambda m, b: (b, jnp.clip(m - 1, 0, kmax) + t, 0))

    def vspec(t):
        return pl.BlockSpec((1, VS, NA_TQ), lambda m, b: (b, 0, jnp.clip(m - 1, 0, kmax) + t))

    return pl.pallas_call(
        _na_kernel,
        out_shape=jax.ShapeDtypeStruct((B, N, NA_W), BF16),
        grid=(nblk, B),
        in_specs=[pl.BlockSpec((1, NA_TQ, NA_W), lambda m, b: (b, m, 0)),
                  kspec(0), kspec(1), kspec(2), vspec(0), vspec(1), vspec(2),
                  pl.BlockSpec((1, Nc, NA_W), lambda m, b: (b, 0, 0)),
                  pl.BlockSpec((1, VS, Nc), lambda m, b: (b, 0, 0)),
                  pl.BlockSpec((1, NA_HEADS, NA_TK, NA_TQ),
                               lambda m, b: (jnp.where(m == 0, 0, jnp.where(m == nblk - 1, 2, 1)), 0, 0, 0))],
        out_specs=pl.BlockSpec((1, NA_TQ, NA_W), lambda m, b: (b, m, 0)),
        compiler_params=_cparams(("arbitrary", "arbitrary"), 48),
        name="natten",
    )(q, k, k, k, vt, vt, vt, kc, vtc, bias)


def _place(w, layout, total):
    pieces, pos = [], 0
    for src, width, dst in sorted(layout, key=lambda item: item[2]):
        if dst > pos:
            pieces.append(jnp.zeros((w.shape[0], dst - pos), w.dtype))
        pieces.append(w[:, src:src + width])
        pos = dst + width
    if pos < total:
        pieces.append(jnp.zeros((w.shape[0], total - pos), w.dtype))
    return jnp.concatenate(pieces, axis=1)


def _ev_in_weight(w):
    lay = [(0, 256, EV_UQ), (256, 128, EV_UKV), (384, MLA_ROPE, EV_UKR + MLA_NOPE),
           (416, LRU_W, EV_UX), (416 + LRU_W, LRU_W, EV_UG)]
    return _place(w, lay, EV_PAD).astype(BF16)


def _od_in_weight(w):
    src = np.cumsum([0, 256, 256, 512, 512, 2 * GLA_LR, 512, 512, 512])
    dst = [OD_GQ, OD_GK, OD_GV, OD_GG, OD_GLR, OD_NQ, OD_NK, OD_NV]
    lay = [(int(src[i]), int(src[i + 1] - src[i]), dst[i]) for i in range(8)]
    return _place(w, lay, OD_PAD).astype(BF16)


def _pad_heads(w, heads, width):
    k = w.shape[0]
    return jnp.pad(w.reshape(k, heads, width), ((0, 0), (0, 0), (0, HEAD_SLOT - width))).reshape(k, heads * HEAD_SLOT)


def kernel(x, c, ctx, c_ctx, ada_w, ada_b, norm_mix, norm_mlp, w_out, mlp_w1, mlp_w2,
           ev_w_in, mla_q_norm, mla_w_uq, mla_kv_norm, mla_w_ukv, mla_q_gain, mla_k_gain,
           lru_conv_w, lru_conv_b, lru_w_a, lru_b_a, lru_w_x, lru_b_x, lru_lam,
           od_w_in, gla_w_a, gla_b_a, gla_o_gain, na_q_gain, na_k_gain, na_rpb):
    B, N, D = x.shape
    Nc = ctx.shape[1]
    depth = ada_w.shape[0]

    R = -(-(B + 1) // 8) * 8
    cc = jnp.concatenate([c, c_ctx[None], jnp.zeros((R - B - 1, D), c.dtype)], axis=0)
    mods = _modulation(cc, ada_w, ada_b).reshape(depth * R, 1, 6 * D)

    w1, w2, wo = mlp_w1.astype(BF16), mlp_w2.astype(BF16), w_out.astype(BF16)
    xl = x
    xc = ctx.reshape(1, B * Nc, D)
    for l in range(depth):
        last = l == depth - 1
        j = l // 2
        row_l, row_c = l * R, l * R + B
        if l % 2 == 0:
            w_in = _ev_in_weight(ev_w_in[j])
            ul = _norm_mod_matmul(xl, norm_mix[l], mods, row_l, w_in)
            uc = _norm_mod_matmul(xc, norm_mix[l], mods, row_c, w_in).reshape(B, Nc, EV_PAD)
            wuq = _pad_heads(mla_w_uq[j], MLA_HEADS, MLA_QK)
            wukv = mla_w_ukv[j].reshape(MLA_KV_LORA, MLA_HEADS, MLA_NOPE + MLA_V)
            wuk = _pad_heads(wukv[:, :, :MLA_NOPE].reshape(MLA_KV_LORA, -1), MLA_HEADS, MLA_NOPE).astype(BF16)
            wuv = _pad_heads(wukv[:, :, MLA_NOPE:].reshape(MLA_KV_LORA, -1), MLA_HEADS, MLA_V).astype(BF16)
            prep = functools.partial(_mla_prep, qn=mla_q_norm[j].reshape(1, -1), kvn=mla_kv_norm[j].reshape(1, -1),
                                     wuq=wuq.astype(BF16), wuqr=_rope_partner_weight(wuq).astype(BF16),
                                     wuk=wuk, wuv=wuv)
            ql, kl, vl = prep(ul, _rope_tables(N, mla_q_gain[j], mla_k_gain[j], True))
            qc, kc, vc = prep(uc, _rope_tables(Nc, mla_q_gain[j], mla_k_gain[j], False))
            ya_l = _mla_attention(ql, kl, vl, kc, vc)
            ya_c = None if last else _mla_attention(qc, None, None, kc, vc)
            wg, bg = _lru_gate_weights(lru_w_a[j], lru_b_a[j], lru_w_x[j], lru_b_x[j])
            yb_l, yb_c = _rglru(ul, uc, lru_conv_w[j], lru_conv_b[j], wg, bg, lru_lam[j])
        else:
            w_in = _od_in_weight(od_w_in[j])
            ul = _norm_mod_matmul(xl, norm_mix[l], mods, row_l, w_in)
            uc = _norm_mod_matmul(xc, norm_mix[l], mods, row_c, w_in).reshape(B, Nc, OD_PAD)
            if not last:
                raise NotImplementedError("context outputs of the odd-layer mixers are only needed when depth > 2")
            wa = jnp.stack([jnp.pad(gla_w_a[j, d], ((d * GLA_LR, LANE - (d + 1) * GLA_LR), (0, 0))) for d in range(2)])
            ya_l = _gla(ul, uc, wa.astype(BF16), gla_b_a[j].reshape(2, 1, GLA_QK_W), gla_o_gain[j].reshape(1, GLA_DV))
            ya_c = None
            qg = jnp.tile(na_q_gain[j] * NA_HD ** -0.5, NA_HEADS).reshape(1, NA_W)
            kg = jnp.tile(na_k_gain[j], NA_HEADS).reshape(1, NA_W)
            nq, nk, nvt = _na_norm(ul, qg, kg)
            _, nkc, nvtc = _na_norm(uc, qg, kg)
            yb_l = _natten(nq, nk, nvt, nkc, nvtc, _na_bias_tables(na_rpb[j], N // GRID_W))
            yb_c = None
        xl = _out_proj_mlp(xl, ya_l, yb_l, wo, norm_mlp[l], mods, row_l, w1, w2, l)
        if not last:
            xc = _out_proj_mlp(xc, ya_c.reshape(1, B * Nc, -1), yb_c.reshape(1, B * Nc, -1), wo, norm_mlp[l], mods,
                               row_c, w1, w2, l)
    return xl
```

```python
import functools

import numpy as np
import jax
import jax.numpy as jnp
from jax import lax
from jax.experimental import pallas as pl
from jax.experimental.pallas import tpu as pltpu

F32 = jnp.float32
BF16 = jnp.bfloat16

D_MODEL = 1024
GRID_W = 64
EPS = 1e-6
ROPE_BASE = 10000.0
D_FF = 4 * D_MODEL

MLA_HEADS = 8
MLA_NOPE = 64
MLA_ROPE = 32
MLA_QK = MLA_NOPE + MLA_ROPE
MLA_V = 64
MLA_Q_LORA = 256
MLA_KV_LORA = 128

LRU_W = 512
LRU_BLOCKS = 8
LRU_BS = LRU_W // LRU_BLOCKS
LRU_C = 8.0
CONV_W = 4
CONV_LEFT = 2

GLA_HEADS = 4
GLA_DK = 64
GLA_DV = 128
GLA_LR = 16
GLA_TAU = 16.0
GLA_CHUNK = 64

NA_HD = 64
NA_HEADS = 8
NA_WIN_ROWS = 8
NA_WIN_COLS = 16

LANE = 128
HEAD_SLOT = 128

EV_UQ, EV_UKV, EV_UKR, EV_UX, EV_UG = 0, 256, 384, 512, 1024
EV_PAD = 1536
OD_GQ, OD_GK, OD_GV, OD_GG, OD_NQ, OD_NK, OD_NV, OD_GLR = 0, 256, 512, 1024, 1536, 2048, 2560, 3072
OD_PAD = 3200


def _cparams(semantics, vmem_mib):
    return pltpu.CompilerParams(dimension_semantics=semantics, vmem_limit_bytes=vmem_mib << 20)


def _rms(x, g):
    return x * lax.rsqrt(jnp.mean(x * x, axis=-1, keepdims=True) + EPS) * g


def _ada_kernel(c_ref, w_ref, b_ref, o_ref):
    cv = c_ref[...]
    s = cv * jax.nn.sigmoid(cv)
    o_ref[0] = jnp.dot(s.astype(BF16), w_ref[0].astype(BF16), preferred_element_type=F32) + b_ref[0]


def _modulation(cc, ada_w, ada_b):
    L, D, D6 = ada_w.shape
    R = cc.shape[0]
    tn = 1536
    return pl.pallas_call(
        _ada_kernel,
        out_shape=jax.ShapeDtypeStruct((L, R, D6), F32),
        grid=(L, D6 // tn),
        in_specs=[pl.BlockSpec((R, D), lambda l, j: (0, 0)),
                  pl.BlockSpec((1, D, tn), lambda l, j: (l, 0, j)),
                  pl.BlockSpec((1, 1, tn), lambda l, j: (l, 0, j))],
        out_specs=pl.BlockSpec((1, R, tn), lambda l, j: (l, 0, j)),
        compiler_params=_cparams(("arbitrary", "arbitrary"), 40),
        name="adaln_modulation",
    )(cc, ada_w, ada_b.reshape(L, 1, D6))


def _mod_spec(row0, k):
    return pl.BlockSpec((1, 1, D_MODEL), lambda g, i: (row0 + g, 0, k))


def _modulated_proj(x_ref, g_ref, sh_ref, sc_ref, w_ref):
    h = _rms(x_ref[0], g_ref[...]) * (1.0 + sc_ref[0]) + sh_ref[0]
    return jnp.dot(h.astype(BF16), w_ref[...], preferred_element_type=F32)


def _nmm_kernel(x_ref, g_ref, sh_ref, sc_ref, w_ref, o_ref):
    o_ref[0] = _modulated_proj(x_ref, g_ref, sh_ref, sc_ref, w_ref).astype(BF16)


def _nmm_na_kernel(x_ref, g_ref, sh_ref, sc_ref, w_ref, qg_ref, kg_ref, o_ref, vt_ref):
    u = _modulated_proj(x_ref, g_ref, sh_ref, sc_ref, w_ref)
    o_ref[0, :, :OD_NQ] = u[:, :OD_NQ].astype(BF16)
    o_ref[0, :, OD_NV:] = u[:, OD_NV:].astype(BF16)
    lane_lo = lax.broadcasted_iota(jnp.int32, (u.shape[0], LANE), 1) < NA_HD
    for base, gain_ref in ((OD_NQ, qg_ref), (OD_NK, kg_ref)):
        for p in range(NA_HEADS // 2):
            t = u[:, base + p * LANE:base + (p + 1) * LANE]
            sq = t * t
            s_lo = jnp.sum(jnp.where(lane_lo, sq, 0.0), axis=-1, keepdims=True)
            s_hi = jnp.sum(jnp.where(lane_lo, 0.0, sq), axis=-1, keepdims=True)
            ms = jnp.where(lane_lo, s_lo, s_hi) * (1.0 / NA_HD)
            gain = gain_ref[:, p * LANE:(p + 1) * LANE]
            o_ref[0, :, base + p * LANE:base + (p + 1) * LANE] = (t * lax.rsqrt(ms + EPS) * gain).astype(BF16)
    for p in range(NA_HEADS // 2):
        pair = u[:, OD_NV + p * LANE:OD_NV + (p + 1) * LANE]
        vt_ref[0, (2 * p) * HEAD_SLOT:(2 * p + 1) * HEAD_SLOT, :] = _value_slot_t(pair, NA_HD)
        vt_ref[0, (2 * p + 1) * HEAD_SLOT:(2 * p + 2) * HEAD_SLOT, :] = _value_slot_t(pltpu.roll(pair, NA_HD, 1), NA_HD)


def _norm_mod_matmul(x, g, mods, row0, w, na_gains=None, tm=512):
    G, M, D = x.shape
    assert M % tm == 0
    Nout = w.shape[1]
    in_specs = [pl.BlockSpec((1, tm, D), lambda b, i: (b, i, 0)),
                pl.BlockSpec((1, D), lambda b, i: (0, 0)),
                _mod_spec(row0, 0), _mod_spec(row0, 1),
                pl.BlockSpec((D, Nout), lambda b, i: (0, 0))]
    u_shape = jax.ShapeDtypeStruct((G, M, Nout), BF16)
    u_spec = pl.BlockSpec((1, tm, Nout), lambda b, i: (b, i, 0))
    if na_gains is None:
        kern, args, out_shape, out_specs = _nmm_kernel, (), u_shape, u_spec
    else:
        VS = NA_HEADS * HEAD_SLOT
        kern, args = _nmm_na_kernel, tuple(na_gains)
        in_specs += [pl.BlockSpec((1, NA_HEADS * NA_HD), lambda b, i: (0, 0))] * 2
        out_shape = (u_shape, jax.ShapeDtypeStruct((G, VS, M), BF16))
        out_specs = (u_spec, pl.BlockSpec((1, VS, tm), lambda b, i: (b, 0, i)))
    return pl.pallas_call(
        kern,
        out_shape=out_shape,
        grid=(G, M // tm),
        in_specs=in_specs,
        out_specs=out_specs,
        compiler_params=_cparams(("parallel", "arbitrary"), 48),
        name="norm_mod_in_proj",
    )(x, g.reshape(1, D), mods, mods, w, *args)


MLP_TF = 1024


def _out_mlp_kernel(x_ref, ya_ref, yb_ref, woa_ref, wob_ref, g_ref, m2_ref, m3_ref, m4_ref, m5_ref,
                    w1_ref, w2_ref, o_ref, h_ref, a_ref):
    y = (jnp.dot(ya_ref[0], woa_ref[...], preferred_element_type=F32)
         + jnp.dot(yb_ref[0], wob_ref[...], preferred_element_type=F32))
    x1 = x_ref[0] + m2_ref[0] * y
    o_ref[0] = x1
    h_ref[...] = (_rms(x1, g_ref[...]) * (1.0 + m4_ref[0]) + m3_ref[0]).astype(BF16)
    for f in range(a_ref.shape[1] // MLP_TF):
        cols = slice(f * MLP_TF, (f + 1) * MLP_TF)
        a = jnp.maximum(jnp.dot(h_ref[...], w1_ref[:, cols], preferred_element_type=F32), 0.0)
        a_ref[:, cols] = (a * a).astype(BF16)
    o_ref[0] += m5_ref[0] * jnp.dot(a_ref[...], w2_ref[...], preferred_element_type=F32)


def _out_proj_mlp(x, ya, yb, w_out, g_mlp, mods, row0, w1, w2, layer, tm=512):
    G, M, D = x.shape
    assert M % tm == 0
    Wa, Wb = ya.shape[-1], yb.shape[-1]
    assert Wa == Wb and Wa + Wb == w_out.shape[1]
    FF = w1.shape[2]

    def resident(shape, row_block=0):
        return pl.BlockSpec((None,) + shape, lambda b, i: (layer, row_block, 0), pipeline_mode=pl.Buffered(1))

    return pl.pallas_call(
        _out_mlp_kernel,
        out_shape=jax.ShapeDtypeStruct((G, M, D), F32),
        grid=(G, M // tm),
        in_specs=[pl.BlockSpec((1, tm, D), lambda b, i: (b, i, 0)),
                  pl.BlockSpec((1, tm, Wa), lambda b, i: (b, i, 0)),
                  pl.BlockSpec((1, tm, Wb), lambda b, i: (b, i, 0)),
                  resident((Wa, D), 0), resident((Wb, D), 1),
                  pl.BlockSpec((1, D), lambda b, i: (0, 0)),
                  _mod_spec(row0, 2), _mod_spec(row0, 3), _mod_spec(row0, 4), _mod_spec(row0, 5),
                  resident((D, FF)), resident((FF, D))],
        out_specs=pl.BlockSpec((1, tm, D), lambda b, i: (b, i, 0)),
        scratch_shapes=[pltpu.VMEM((tm, D), BF16), pltpu.VMEM((tm, FF), BF16)],
        compiler_params=_cparams(("parallel", "arbitrary"), 52),
        name="out_proj_mlp",
    )(x, ya, yb, w_out, w_out, g_mlp.reshape(1, D), mods, mods, mods, mods, w1, w2)


def _nmm_mla_kernel(x_ref, g_ref, sh_ref, sc_ref, w_ref, tq1_ref, tq2_ref, tk1_ref, tka_ref, tkb_ref, qn_ref, kvn_ref,
                    wuq_ref, wuqr_ref, wuk_ref, wuv_ref, o_ref, q_out, k_out, vt_out):
    u = _modulated_proj(x_ref, g_ref, sh_ref, sc_ref, w_ref)
    o_ref[0] = u[:, EV_UX:].astype(BF16)
    qn = _rms(u[:, EV_UQ:EV_UQ + MLA_Q_LORA], qn_ref[...]).astype(BF16)
    kvn = _rms(u[:, EV_UKV:EV_UKV + MLA_KV_LORA], kvn_ref[...]).astype(BF16)
    q_all = jnp.dot(qn, wuq_ref[...], preferred_element_type=F32)
    q_rot = jnp.dot(qn, wuqr_ref[...], preferred_element_type=F32)
    k_all = jnp.dot(kvn, wuk_ref[...], preferred_element_type=F32)
    v_all = jnp.dot(kvn, wuv_ref[...], preferred_element_type=F32)
    ukr = u[:, EV_UKR:EV_UKR + HEAD_SLOT]
    quarter = MLA_ROPE // 4
    k_rot = (pltpu.roll(ukr, HEAD_SLOT - quarter, 1) * tka_ref[...] + pltpu.roll(ukr, quarter, 1) * tkb_ref[...])
    ukr_sq = jnp.sum(ukr * ukr, axis=-1, keepdims=True)
    inv_n = 1.0 / MLA_QK
    for h in range(MLA_HEADS):
        sl = slice(h * HEAD_SLOT, (h + 1) * HEAD_SLOT)
        qh = q_all[:, sl]
        rq = lax.rsqrt(jnp.sum(qh * qh, axis=-1, keepdims=True) * inv_n + EPS)
        q_out[0, :, sl] = (rq * (qh * tq1_ref[...] + q_rot[:, sl] * tq2_ref[...])).astype(BF16)
        kn = k_all[:, sl]
        rk = lax.rsqrt((jnp.sum(kn * kn, axis=-1, keepdims=True) + ukr_sq) * inv_n + EPS)
        k_out[0, :, sl] = (rk * ((kn + ukr) * tk1_ref[...] + k_rot)).astype(BF16)
        vt_out[0, sl, :] = _value_slot_t(v_all[:, sl], MLA_V)


def _rope_partner(t):
    quarter = MLA_ROPE // 4
    t4 = t.reshape(t.shape[:-1] + (2, 2, quarter))
    return jnp.stack([t4[..., 1, :], t4[..., 0, :]], axis=-2).reshape(t.shape)


def _rope_tables(n, q_gain, k_gain, use_rope):
    quarter = MLA_ROPE // 4
    if use_rope:
        pos = jnp.arange(n)
        inv = ROPE_BASE ** (-jnp.arange(0, MLA_ROPE // 2, 2, dtype=F32) / (MLA_ROPE // 2))
        ang_r = (pos // GRID_W).astype(F32)[:, None] * inv[None, :]
        ang_c = (pos % GRID_W).astype(F32)[:, None] * inv[None, :]
        ang = jnp.concatenate([ang_r, ang_r, ang_c, ang_c], axis=-1)
        cos, sin = jnp.cos(ang), jnp.sin(ang)
    else:
        cos, sin = jnp.ones((n, MLA_ROPE), F32), jnp.zeros((n, MLA_ROPE), F32)
    first = (np.arange(MLA_ROPE) % (2 * quarter)) < quarter
    pad = jnp.zeros((n, HEAD_SLOT - MLA_QK), F32)

    def slot(nope, rope):
        return jnp.concatenate([jnp.broadcast_to(nope, (n, MLA_NOPE)), rope, pad], axis=-1)

    zero = jnp.zeros((MLA_NOPE,), F32)
    qg, kg = q_gain * MLA_QK ** -0.5, k_gain
    tq1 = slot(qg[:MLA_NOPE], cos * qg[MLA_NOPE:])
    tq2 = slot(zero, sin * _rope_partner(qg[MLA_NOPE:]))
    tk1 = slot(kg[:MLA_NOPE], cos * kg[MLA_NOPE:])
    ksin = sin * _rope_partner(kg[MLA_NOPE:])
    tka = slot(zero, jnp.where(first, -ksin, 0.0))
    tkb = slot(zero, jnp.where(first, 0.0, ksin))
    return tq1, tq2, tk1, tka, tkb


def _rope_partner_weight(wuq):
    k = wuq.shape[0]
    quarter = MLA_ROPE // 4
    w = wuq.reshape(k, MLA_HEADS, HEAD_SLOT)
    rope = w[:, :, MLA_NOPE:MLA_QK].reshape(k, MLA_HEADS, 2, 2, quarter)
    rot = jnp.stack([-rope[:, :, :, 1, :], rope[:, :, :, 0, :]], axis=3).reshape(k, MLA_HEADS, MLA_ROPE)
    out = jnp.concatenate([jnp.zeros_like(w[:, :, :MLA_NOPE]), rot, jnp.zeros_like(w[:, :, MLA_QK:])], axis=-1)
    return out.reshape(k, MLA_HEADS * HEAD_SLOT)


def _ev_in_proj(x, g, mods, row0, w, tables, qn, kvn, wuq, wuqr, wuk, wuv, tm=512):
    G, M, D = x.shape
    assert M % tm == 0
    HS = MLA_HEADS * HEAD_SLOT
    full = lambda shape: pl.BlockSpec(shape, lambda b, i: (0,) * len(shape))
    tab = pl.BlockSpec((tm, HEAD_SLOT), lambda b, i: (i, 0))
    rows = lambda width: pl.BlockSpec((1, tm, width), lambda b, i: (b, i, 0))
    return pl.pallas_call(
        _nmm_mla_kernel,
        out_shape=(jax.ShapeDtypeStruct((G, M, 2 * LRU_W), BF16), jax.ShapeDtypeStruct((G, M, HS), BF16),
                   jax.ShapeDtypeStruct((G, M, HS), BF16), jax.ShapeDtypeStruct((G, HS, M), BF16)),
        grid=(G, M // tm),
        in_specs=[rows(D), full((1, D)), _mod_spec(row0, 0), _mod_spec(row0, 1), full((D, EV_PAD)),
                  tab, tab, tab, tab, tab,
                  full((1, MLA_Q_LORA)), full((1, MLA_KV_LORA)),
                  full((MLA_Q_LORA, HS)), full((MLA_Q_LORA, HS)), full((MLA_KV_LORA, HS)), full((MLA_KV_LORA, HS))],
        out_specs=(rows(2 * LRU_W), rows(HS), rows(HS), pl.BlockSpec((1, HS, tm), lambda b, i: (b, 0, i))),
        compiler_params=_cparams(("parallel", "arbitrary"), 48),
        name="ev_in_proj_mla_prep",
    )(x, g.reshape(1, D), mods, mods, w, *tables, qn, kvn, wuq, wuqr, wuk, wuv)


def _dot_nt(a, b):
    return lax.dot_general(a, b, (((1,), (1,)), ((), ())), preferred_element_type=F32)


def _value_slot_t(v, ones_row):
    lane = lax.broadcasted_iota(jnp.int32, v.shape, 1)
    slot = jnp.where(lane < ones_row, v, jnp.where(lane == ones_row, 1.0, 0.0))
    return jnp.transpose(slot).astype(BF16)


ATTN_GROUP = 8


def _col_max(a, rows=8):
    parts = [a[r:r + rows] for r in range(0, a.shape[0], rows)]
    while len(parts) > 1:
        parts = [jnp.maximum(parts[i], parts[i + 1]) if i + 1 < len(parts) else parts[i]
                 for i in range(0, len(parts), 2)]
    return jnp.max(parts[0], axis=0, keepdims=True)


def _attend_t(heads):
    scores = []
    for q, keys, _, biases in heads:
        s = [_dot_nt(k, q) for k in keys]
        if biases is not None:
            s = [a if b is None else a + b for a, b in zip(s, biases)]
        scores.append(s)
    maxes = [functools.reduce(jnp.maximum, [_col_max(a) for a in s]) for s in scores]
    outs = []
    for (_, _, values_t, _), s, m in zip(heads, scores, maxes):
        out_t = None
        for a, vt in zip(s, values_t):
            part = jnp.dot(vt, jnp.exp(a - m).astype(BF16), preferred_element_type=F32)
            out_t = part if out_t is None else out_t + part
        outs.append(out_t)
    return outs


def _pair_output(slots_t, dv):
    halves = [t[:dv] / t[dv:dv + 1] for t in slots_t]
    return jnp.transpose(jnp.concatenate(halves, axis=0))


def _mla_attn_kernel(*refs, with_latent):
    if with_latent:
        q_ref, kl_ref, vtl_ref, kc_ref, vtc_ref, o_ref = refs
    else:
        q_ref, kc_ref, vtc_ref, o_ref = refs
    for h0 in range(0, MLA_HEADS, ATTN_GROUP):
        heads = []
        for h in range(h0, h0 + ATTN_GROUP):
            sl = slice(h * HEAD_SLOT, (h + 1) * HEAD_SLOT)
            keys, values_t = [kc_ref[0, :, sl]], [vtc_ref[0, sl, :]]
            if with_latent:
                keys.append(kl_ref[0, :, sl])
                values_t.append(vtl_ref[0, sl, :])
            heads.append((q_ref[0, :, sl], keys, values_t, None))
        outs = _attend_t(heads)
        for i in range(0, ATTN_GROUP, 2):
            hp = (h0 + i) // 2
            o_ref[0, :, hp * LANE:(hp + 1) * LANE] = _pair_output(outs[i:i + 2], MLA_V).astype(BF16)


def _mla_attention(q, kl, vtl, kc, vtc, tq=256):
    B, M, HS = q.shape
    assert M % tq == 0
    HV = MLA_HEADS * MLA_V
    Nc = kc.shape[1]
    with_latent = kl is not None
    whole = lambda n, w: pl.BlockSpec((1, n, w), lambda b, i: (b, 0, 0))
    in_specs = [pl.BlockSpec((1, tq, HS), lambda b, i: (b, i, 0))]
    args = [q]
    if with_latent:
        in_specs += [whole(kl.shape[1], HS), whole(HS, kl.shape[1])]
        args += [kl, vtl]
    in_specs += [whole(Nc, HS), pl.BlockSpec((1, HS, Nc), lambda b, i: (0, 0, b))]
    args += [kc, vtc]
    return pl.pallas_call(
        functools.partial(_mla_attn_kernel, with_latent=with_latent),
        out_shape=jax.ShapeDtypeStruct((B, M, HV), BF16),
        grid=(B, M // tq),
        in_specs=in_specs,
        out_specs=pl.BlockSpec((1, tq, HV), lambda b, i: (b, i, 0)),
        compiler_params=_cparams(("parallel", "arbitrary"), 48),
        name="mla_attention",
    )(*args)


LRU_CW = 256
LRU_HALO = 16
LRU_TN = 256


def _gelu_tanh(x):
    return 0.5 * x * (1.0 + jnp.tanh(0.7978845608028654 * (x + 0.044715 * (x * x * x))))


def _scan_group(a, bv, h, reverse):
    row = lax.broadcasted_iota(jnp.int32, a.shape, 0)
    for s in (1, 2, 4):
        if reverse:
            keep = row < 8 - s
            shift = 8 - s
        else:
            keep = row >= s
            shift = s
        a_s = jnp.where(keep, pltpu.roll(a, shift, 0), 1.0)
        b_s = jnp.where(keep, pltpu.roll(bv, shift, 0), 0.0)
        bv = a * b_s + bv
        a = a * a_s
    hs = a * h + bv
    return hs, (hs[0:1, :] if reverse else hs[7:8, :])


def _lru_kernel(uxl_ref, ugl_ref, uxc_ref, ugc_ref, cw_ref, cb_ref, wg_ref, bg_ref, lam_ref, yl_ref, yc_ref,
                xpl_ref, xpc_ref, xcv_ref, af_ref, bf_ref, ab_ref, bb_ref):
    N, Nc = uxl_ref.shape[1], uxc_ref.shape[1]
    NT = N + Nc
    C = LRU_CW
    H = LRU_HALO

    def conv(src_ref, pad_ref, n, row0):
        pad_ref[0:H, :] = jnp.zeros((H, C), F32)
        pad_ref[H + n:H + n + H, :] = jnp.zeros((H, C), F32)
        pad_ref[H:H + n, :] = src_ref[0].astype(F32)
        y = cb_ref[...] + pad_ref[H - CONV_LEFT:H - CONV_LEFT + n, :] * cw_ref[0:1, :]
        for j in range(1, CONV_W):
            y = y + pad_ref[H - CONV_LEFT + j:H - CONV_LEFT + j + n, :] * cw_ref[j:j + 1, :]
        xcv_ref[row0:row0 + n, :] = y

    conv(uxc_ref, xpc_ref, Nc, 0)
    conv(uxl_ref, xpl_ref, N, Nc)

    lam = lam_ref[...]
    c_half = (-0.5 * LRU_C) * (jnp.maximum(-lam, 0.0) + jnp.log1p(jnp.exp(-jnp.abs(lam))))

    def coeff_chunk(i, carry):
        r0 = pl.multiple_of(i * LRU_TN, LRU_TN)
        x = xcv_ref[pl.ds(r0, LRU_TN), :]
        t = jnp.tanh(jnp.dot(x.astype(BF16), wg_ref[0], preferred_element_type=F32) + bg_ref[0])
        hx = 0.5 * x
        for d, (a_ref, b_ref) in enumerate(((af_ref, bf_ref), (ab_ref, bb_ref))):
            c = c_half[d:d + 1, :]
            a = jnp.exp(c * t[:, (2 * d) * C:(2 * d + 1) * C] + c)
            a_ref[pl.ds(r0, LRU_TN), :] = a
            gated_x = hx * t[:, (2 * d + 1) * C:(2 * d + 2) * C] + hx
            b_ref[pl.ds(r0, LRU_TN), :] = jnp.sqrt(1.0 - a * a) * gated_x
        return carry

    lax.fori_loop(0, NT // LRU_TN, coeff_chunk, 0)

    ngc, ngt = Nc // 8, NT // 8

    def scan_step(i, carry):
        hf, hb = carry
        rf = pl.multiple_of(i * 8, 8)
        rb = pl.multiple_of(jnp.where(i < ngc, ngc - 1 - i, ngt + ngc - 1 - i) * 8, 8)
        hs_f, hf = _scan_group(af_ref[pl.ds(rf, 8), :], bf_ref[pl.ds(rf, 8), :], hf, False)
        hs_b, hb = _scan_group(ab_ref[pl.ds(rb, 8), :], bb_ref[pl.ds(rb, 8), :], hb, True)
        bf_ref[pl.ds(rf, 8), :] = hs_f
        bb_ref[pl.ds(rb, 8), :] = hs_b
        return hf, hb

    zero = jnp.zeros((1, C), F32)
    lax.fori_loop(0, ngt, scan_step, (zero, zero), unroll=4)

    def out_chunk(i, carry):
        r0 = pl.multiple_of(i * LRU_TN, LRU_TN)
        hsum = bf_ref[pl.ds(Nc + r0, LRU_TN), :] + bb_ref[pl.ds(Nc + r0, LRU_TN), :]
        gate = _gelu_tanh(ugl_ref[0, pl.ds(r0, LRU_TN), :].astype(F32))
        yl_ref[0, pl.ds(r0, LRU_TN), :] = (hsum * gate).astype(BF16)
        return carry

    lax.fori_loop(0, N // LRU_TN, out_chunk, 0)
    yc_ref[0] = ((bf_ref[0:Nc, :] + bb_ref[0:Nc, :]) * _gelu_tanh(ugc_ref[0].astype(F32))).astype(BF16)


def _rglru(ul, uc, conv_w, conv_b, wg, bg, lam):
    B, N, _ = ul.shape
    Nc = uc.shape[1]
    C = LRU_CW
    nh = LRU_W // C
    NT = N + Nc
    col = lambda base: (lambda b, j: (b, 0, base // C + j))
    par = lambda rows: pl.BlockSpec((rows, C), lambda b, j: (0, j))
    return pl.pallas_call(
        _lru_kernel,
        out_shape=(jax.ShapeDtypeStruct((B, N, LRU_W), BF16), jax.ShapeDtypeStruct((B, Nc, LRU_W), BF16)),
        grid=(B, nh),
        in_specs=[pl.BlockSpec((1, N, C), col(0)), pl.BlockSpec((1, N, C), col(LRU_W)),
                  pl.BlockSpec((1, Nc, C), col(0)), pl.BlockSpec((1, Nc, C), col(LRU_W)),
                  par(CONV_W), par(1),
                  pl.BlockSpec((1, C, 4 * C), lambda b, j: (j, 0, 0)),
                  pl.BlockSpec((1, 1, 4 * C), lambda b, j: (j, 0, 0)),
                  par(2)],
        out_specs=(pl.BlockSpec((1, N, C), lambda b, j: (b, 0, j)),
                   pl.BlockSpec((1, Nc, C), lambda b, j: (b, 0, j))),
        scratch_shapes=[pltpu.VMEM((N + 2 * LRU_HALO, C), F32), pltpu.VMEM((Nc + 2 * LRU_HALO, C), F32),
                        pltpu.VMEM((NT, C), F32)] + [pltpu.VMEM((NT, C), F32)] * 4,
        compiler_params=_cparams(("parallel", "arbitrary"), 48),
        name="rglru",
    )(ul, ul, uc, uc, conv_w, conv_b.reshape(1, LRU_W), wg, bg, lam)


def _lru_gate_weights(w_a, b_a, w_x, b_x):
    C = LRU_CW
    nh = LRU_W // C
    kb = C // LRU_BS

    def dense(w):
        w = w.reshape(nh, kb, LRU_BS, LRU_BS)
        eye = jnp.eye(kb, dtype=w.dtype)
        return jnp.einsum('hkij,kl->hkilj', w, eye).reshape(nh, C, C)

    wg = jnp.concatenate([dense(w_a[0]), dense(w_x[0]), dense(w_a[1]), dense(w_x[1])], axis=-1)
    bg = jnp.stack([b_a[0], b_x[0], b_a[1], b_x[1]], axis=0).reshape(4, nh, C)
    bg = jnp.transpose(bg, (1, 0, 2)).reshape(nh, 1, 4 * C)
    return (0.5 * wg).astype(BF16), 0.5 * bg


GLA_SC = 256
GLA_UNROLL = 2
GLA_QK_W = GLA_HEADS * GLA_DK
GLA_V_W = GLA_HEADS * GLA_DV


def _split3(x):
    hi = x.astype(BF16)
    r1 = x - hi.astype(F32)
    mid = r1.astype(BF16)
    lo = (r1 - mid.astype(F32)).astype(BF16)
    return hi, mid, lo


def _gla_kernel(ql_ref, kl_ref, vl_ref, gl_ref, lrl_ref, kc_ref, vc_ref, lrc_ref, wa_ref, ba_ref, og_ref,
                y_ref, oacc_ref, st_ref):
    N, Nc = ql_ref.shape[1], kc_ref.shape[1]
    T = GLA_SC
    CH = GLA_CHUNK
    npair = GLA_HEADS // 2
    row = lax.broadcasted_iota(jnp.int32, (T, T), 0)
    colm = lax.broadcasted_iota(jnp.int32, (T, T), 1)
    same_chunk = (row // CH) == (colm // CH)
    causal = (same_chunk & (colm <= row), same_chunk & (colm >= row))
    tri = tuple(jnp.where(c, 1.0, 0.0).astype(BF16) for c in causal)
    lane_lo = lax.broadcasted_iota(jnp.int32, (T, LANE), 1) < GLA_DK
    srow = lax.broadcasted_iota(jnp.int32, (LANE, 2 * GLA_DV), 0)
    scol = lax.broadcasted_iota(jnp.int32, (LANE, 2 * GLA_DV), 1)
    state_mask = (srow < GLA_DK) == (scol < GLA_DV)
    qscale = GLA_DK ** -0.5

    def superchunks(streams, q_ref, k_ref, v_ref, lr_ref, with_output):
        nch = T // CH
        work = []
        for d, r0 in streams:
            z = jnp.dot(lr_ref[0, pl.ds(r0, T), :], wa_ref[d], preferred_element_type=F32) + ba_ref[d]
            work.append(dict(d=d, r0=r0, split=_split3(jax.nn.log_sigmoid(z) * (1.0 / GLA_TAU))))
        for w in work:
            d = w['d']
            cum = sum(jnp.dot(tri[d], t, preferred_element_type=F32) for t in w['split'])
            last = (CH - 1) if d == 0 else 0
            tot = [cum[c * CH + last:c * CH + last + 1, :] for c in range(nch)]
            tot_rows = jnp.concatenate([jnp.broadcast_to(t, (CH, GLA_QK_W)) for t in tot], axis=0)
            k = k_ref[0, pl.ds(w['r0'], T), :].astype(F32)
            w.update(cum=cum, tot=tot, k=k, v=v_ref[0, pl.ds(w['r0'], T), :],
                     k_dec=(k * jnp.exp(tot_rows - cum)).astype(BF16))
        for w in work:
            ds, dec = {}, {}
            for c in range(nch):
                rs = slice(c * CH, (c + 1) * CH)
                for p in range(npair):
                    kd = w['k_dec'][rs, p * LANE:(p + 1) * LANE]
                    vv = w['v'][rs, p * 2 * GLA_DV:(p + 1) * 2 * GLA_DV]
                    kv = lax.dot_general(kd, vv, (((0,), (0,)), ((), ())), preferred_element_type=F32)
                    ds[c, p] = jnp.where(state_mask, kv, 0.0)
                    dec_row = jnp.exp(w['tot'][c][:, p * LANE:(p + 1) * LANE])
                    dec_col = jnp.transpose(jnp.broadcast_to(dec_row, (LANE, LANE)))
                    dec[c, p] = jnp.concatenate([dec_col, dec_col], axis=1)
            w.update(ds=ds, dec=dec)
        for w in work:
            d = w['d']
            order = range(nch) if d == 0 else range(nch - 1, -1, -1)
            s_in = {}
            for p in range(npair):
                s = st_ref[d, p]
                for c in order:
                    s_in[c, p] = s.astype(BF16)
                    s = w['dec'][c, p] * s + w['ds'][c, p]
                st_ref[d, p] = s
            w.update(s_in=s_in)
        if not with_output:
            return
        for w in work:
            q = q_ref[0, pl.ds(w['r0'], T), :].astype(F32)
            w.update(q_dec=((q * qscale) * jnp.exp(w['cum'])).astype(BF16),
                     k_inv=(w['k'] * jnp.exp(-w['cum'])).astype(BF16))
        for w in work:
            d, r0 = w['d'], w['r0']
            for p in range(npair):
                qp = w['q_dec'][:, p * LANE:(p + 1) * LANE]
                kp = w['k_inv'][:, p * LANE:(p + 1) * LANE]
                o_inter = jnp.concatenate(
                    [jnp.dot(qp[c * CH:(c + 1) * CH], w['s_in'][c, p], preferred_element_type=F32)
                     for c in range(nch)], axis=0)
                for j in range(2):
                    h = 2 * p + j
                    qm = jnp.where(lane_lo if j == 0 else jnp.logical_not(lane_lo), qp, jnp.zeros_like(qp))
                    att = jnp.where(causal[d], _dot_nt(qm, kp), 0.0).astype(BF16)
                    o = (jnp.dot(att, w['v'][:, h * GLA_DV:(h + 1) * GLA_DV], preferred_element_type=F32)
                         + o_inter[:, j * GLA_DV:(j + 1) * GLA_DV])
                    oacc_ref[d, pl.ds(r0, T), h * GLA_DV:(h + 1) * GLA_DV] = o

    st_ref[...] = jnp.zeros_like(st_ref)
    ncs, nls = Nc // T, N // T

    def ctx_body(i, carry):
        streams = [(0, pl.multiple_of(i * T, T)), (1, pl.multiple_of((ncs - 1 - i) * T, T))]
        superchunks(streams, None, kc_ref, vc_ref, lrc_ref, False)
        return carry

    def lat_body(i, carry):
        streams = []
        for u in range(GLA_UNROLL):
            streams += [(0, pl.multiple_of((GLA_UNROLL * i + u) * T, T)),
                        (1, pl.multiple_of((nls - 1 - GLA_UNROLL * i - u) * T, T))]
        superchunks(streams, ql_ref, kl_ref, vl_ref, lrl_ref, True)
        return carry

    lax.fori_loop(0, ncs, ctx_body, 0)
    lax.fori_loop(0, nls // GLA_UNROLL, lat_body, 0)

    def fin(i, carry):
        r0 = pl.multiple_of(i * T, T)
        g = gl_ref[0, pl.ds(r0, T), :].astype(F32)
        for h in range(GLA_HEADS):
            sl = slice(h * GLA_DV, (h + 1) * GLA_DV)
            gh = g[:, sl]
            o = oacc_ref[0, pl.ds(r0, T), sl] + oacc_ref[1, pl.ds(r0, T), sl]
            y_ref[0, pl.ds(r0, T), sl] = (_rms(o, og_ref[...]) * (gh * jax.nn.sigmoid(gh))).astype(BF16)
        return carry

    lax.fori_loop(0, N // T, fin, 0)


def _gla(ul, uc, wa, ba, o_gain):
    B, N, _ = ul.shape
    Nc = uc.shape[1]
    blk = lambda n, w, base: pl.BlockSpec((1, n, w), lambda b: (b, 0, base // w))
    full = lambda shape: pl.BlockSpec(shape, lambda b: (0,) * len(shape))
    return pl.pallas_call(
        _gla_kernel,
        out_shape=jax.ShapeDtypeStruct((B, N, GLA_V_W), BF16),
        grid=(B,),
        in_specs=[blk(N, GLA_QK_W, OD_GQ), blk(N, GLA_QK_W, OD_GK), blk(N, GLA_V_W, OD_GV),
                  blk(N, GLA_V_W, OD_GG), blk(N, LANE, OD_GLR),
                  blk(Nc, GLA_QK_W, OD_GK), blk(Nc, GLA_V_W, OD_GV), blk(Nc, LANE, OD_GLR),
                  full((2, LANE, GLA_QK_W)), full((2, 1, GLA_QK_W)), full((1, GLA_DV))],
        out_specs=pl.BlockSpec((1, N, GLA_V_W), lambda b: (b, 0, 0)),
        scratch_shapes=[pltpu.VMEM((2, N, GLA_V_W), F32),
                        pltpu.VMEM((2, GLA_HEADS // 2, LANE, 2 * GLA_DV), F32)],
        compiler_params=_cparams(("parallel",), 48),
        name="gla",
    )(ul, ul, ul, ul, ul, uc, uc, uc, wa, ba, o_gain)


NA_W = NA_HEADS * NA_HD
NA_QROWS = 4
NA_KROWS = 12
NA_TQ = NA_QROWS * GRID_W
NA_TK = NA_KROWS * GRID_W
NA_NEG = -1e30


def _na_kernel(q_ref, k0_ref, k1_ref, k2_ref, vt0_ref, vt1_ref, vt2_ref, kc_ref, vtc_ref, bias_ref, o_ref):
    lane_lo = lax.broadcasted_iota(jnp.int32, (NA_TQ, LANE), 1) < NA_HD
    k_refs = (k0_ref, k1_ref, k2_ref, kc_ref)
    vt_refs = (vt0_ref, vt1_ref, vt2_ref, vtc_ref)
    for h0 in range(0, NA_HEADS, ATTN_GROUP):
        heads = []
        for h in range(h0, h0 + ATTN_GROUP):
            sl = slice((h // 2) * LANE, (h // 2 + 1) * LANE)
            hs = slice(h * HEAD_SLOT, (h + 1) * HEAD_SLOT)
            qp = q_ref[0, :, sl]
            qm = jnp.where(lane_lo if h % 2 == 0 else jnp.logical_not(lane_lo), qp, jnp.zeros_like(qp))
            biases = [bias_ref[0, h, t * NA_TQ:(t + 1) * NA_TQ, :] for t in range(3)] + [None]
            heads.append((qm, [r[0, :, sl] for r in k_refs], [r[0, hs, :] for r in vt_refs], biases))
        outs = _attend_t(heads)
        for i in range(0, ATTN_GROUP, 2):
            hp = (h0 + i) // 2
            o_ref[0, :, hp * LANE:(hp + 1) * LANE] = _pair_output(outs[i:i + 2], NA_HD).astype(BF16)


def _na_bias_tables(rpb, rows):
    H, ndi, ndj = rpb.shape
    W = GRID_W
    half = NA_WIN_COLS - 1
    zeros = jnp.zeros((H, ndi, W - 1 - half), rpb.dtype)
    vec = jnp.concatenate([rpb[:, :, half:], zeros, zeros, rpb[:, :, :half]], axis=-1)
    toep = jnp.tile(vec, (1, 1, W))[:, :, :W * (2 * W - 2)].reshape(H, ndi, W, 2 * W - 2)[..., :W]
    qc = np.arange(W)[:, None]
    kc = np.arange(W)[None, :]
    c0 = np.clip(qc - NA_WIN_COLS // 2, 0, W - NA_WIN_COLS)
    toep = jnp.where(jnp.asarray((kc >= c0) & (kc < c0 + NA_WIN_COLS)), toep, NA_NEG)
    toep = jnp.swapaxes(toep, 2, 3)
    nblk = rows // NA_QROWS
    guard = NA_KROWS - NA_WIN_ROWS
    toep = jnp.pad(toep, ((0, 0), (guard, guard), (0, 0), (0, 0)), constant_values=NA_NEG)
    tabs = []
    for m in (0, 1, nblk - 1):
        kb = min(max(m - 1, 0), nblk - NA_KROWS // NA_QROWS)
        qr = m * NA_QROWS + np.arange(NA_QROWS)[None, :]
        kr = kb * NA_QROWS + np.arange(NA_KROWS)[:, None]
        r0 = np.clip(qr - NA_WIN_ROWS // 2, 0, rows - NA_WIN_ROWS)
        row_ok = (kr >= r0) & (kr < r0 + NA_WIN_ROWS)
        di0 = (kb - m) * NA_QROWS + NA_WIN_ROWS - 1 + guard
        per_a = [toep[:, di0 - a:di0 - a + NA_KROWS] for a in range(NA_QROWS)]
        ok = np.repeat(row_ok, W, axis=1)[None, :, None, :]
        tab = jnp.where(jnp.asarray(ok), jnp.concatenate(per_a, axis=-1), NA_NEG)
        tabs.append(tab.reshape(H, NA_TK, NA_TQ))
    return jnp.stack(tabs, axis=0)


def _natten(ul, vt, uc, vtc, bias):
    B, N, _ = ul.shape
    Nc = uc.shape[1]
    nblk = N // NA_TQ
    kmax = nblk - NA_KROWS // NA_QROWS
    VS = NA_HEADS * HEAD_SLOT
    qcol, kcol = OD_NQ // NA_W, OD_NK // NA_W

    def kspec(t):
        return pl.BlockSpec((1, NA_TQ, NA_W), lambda m, b: (b, jnp.clip(m - 1, 0, kmax) + t, kcol))

    def vspec(t):
        return pl.BlockSpec((1, VS, NA_TQ), lambda m, b: (b, 0, jnp.clip(m - 1, 0, kmax) + t))

    return pl.pallas_call(
        _na_kernel,
        out_shape=jax.ShapeDtypeStruct((B, N, NA_W), BF16),
        grid=(nblk, B),
        in_specs=[pl.BlockSpec((1, NA_TQ, NA_W), lambda m, b: (b, m, qcol)),
                  kspec(0), kspec(1), kspec(2), vspec(0), vspec(1), vspec(2),
                  pl.BlockSpec((1, Nc, NA_W), lambda m, b: (b, 0, kcol)),
                  pl.BlockSpec((1, VS, Nc), lambda m, b: (0, 0, b)),
                  pl.BlockSpec((1, NA_HEADS, NA_TK, NA_TQ),
                               lambda m, b: (jnp.where(m == 0, 0, jnp.where(m == nblk - 1, 2, 1)), 0, 0, 0))],
        out_specs=pl.BlockSpec((1, NA_TQ, NA_W), lambda m, b: (b, m, 0)),
        compiler_params=_cparams(("arbitrary", "arbitrary"), 48),
        name="natten",
    )(ul, ul, ul, ul, vt, vt, vt, uc, vtc, bias)


def _place(w, layout, total):
    pieces, pos = [], 0
    for src, width, dst in sorted(layout, key=lambda item: item[2]):
        if dst > pos:
            pieces.append(jnp.zeros((w.shape[0], dst - pos), w.dtype))
        pieces.append(w[:, src:src + width])
        pos = dst + width
    if pos < total:
        pieces.append(jnp.zeros((w.shape[0], total - pos), w.dtype))
    return jnp.concatenate(pieces, axis=1)


def _ev_in_weight(w):
    lay = [(0, 256, EV_UQ), (256, 128, EV_UKV), (384, MLA_ROPE, EV_UKR + MLA_NOPE),
           (416, LRU_W, EV_UX), (416 + LRU_W, LRU_W, EV_UG)]
    return _place(w, lay, EV_PAD).astype(BF16)


def _od_in_weight(w):
    src = np.cumsum([0, 256, 256, 512, 512, 2 * GLA_LR, 512, 512, 512])
    dst = [OD_GQ, OD_GK, OD_GV, OD_GG, OD_GLR, OD_NQ, OD_NK, OD_NV]
    lay = [(int(src[i]), int(src[i + 1] - src[i]), dst[i]) for i in range(8)]
    return _place(w, lay, OD_PAD).astype(BF16)


def _pad_heads(w, heads, width):
    k = w.shape[0]
    return jnp.pad(w.reshape(k, heads, width), ((0, 0), (0, 0), (0, HEAD_SLOT - width))).reshape(k, heads * HEAD_SLOT)


def kernel(x, c, ctx, c_ctx, ada_w, ada_b, norm_mix, norm_mlp, w_out, mlp_w1, mlp_w2,
           ev_w_in, mla_q_norm, mla_w_uq, mla_kv_norm, mla_w_ukv, mla_q_gain, mla_k_gain,
           lru_conv_w, lru_conv_b, lru_w_a, lru_b_a, lru_w_x, lru_b_x, lru_lam,
           od_w_in, gla_w_a, gla_b_a, gla_o_gain, na_q_gain, na_k_gain, na_rpb):
    B, N, D = x.shape
    Nc = ctx.shape[1]
    depth = ada_w.shape[0]

    R = -(-(B + 1) // 8) * 8
    cc = jnp.concatenate([c, c_ctx[None], jnp.zeros((R - B - 1, D), c.dtype)], axis=0)
    mods = _modulation(cc, ada_w, ada_b).reshape(depth * R, 1, 6 * D)

    w1, w2, wo = mlp_w1.astype(BF16), mlp_w2.astype(BF16), w_out.astype(BF16)
    xl = x
    xc = ctx.reshape(1, B * Nc, D)
    for l in range(depth):
        last = l == depth - 1
        j = l // 2
        row_l, row_c = l * R, l * R + B
        if l % 2 == 0:
            w_in = _ev_in_weight(ev_w_in[j])
            wuq = _pad_heads(mla_w_uq[j], MLA_HEADS, MLA_QK)
            wukv = mla_w_ukv[j].reshape(MLA_KV_LORA, MLA_HEADS, MLA_NOPE + MLA_V)
            wuk = _pad_heads(wukv[:, :, :MLA_NOPE].reshape(MLA_KV_LORA, -1), MLA_HEADS, MLA_NOPE).astype(BF16)
            wuv = _pad_heads(wukv[:, :, MLA_NOPE:].reshape(MLA_KV_LORA, -1), MLA_HEADS, MLA_V).astype(BF16)
            proj = functools.partial(_ev_in_proj, g=norm_mix[l], mods=mods, w=w_in, qn=mla_q_norm[j].reshape(1, -1),
                                     kvn=mla_kv_norm[j].reshape(1, -1), wuq=wuq.astype(BF16),
                                     wuqr=_rope_partner_weight(wuq).astype(BF16), wuk=wuk, wuv=wuv)
            ul, ql, kl, vl = proj(xl, row0=row_l, tables=_rope_tables(N, mla_q_gain[j], mla_k_gain[j], True))
            uc, qc, kc, vc = proj(xc, row0=row_c, tables=_rope_tables(B * Nc, mla_q_gain[j], mla_k_gain[j], False))
            uc, qc, kc = (t.reshape(B, Nc, -1) for t in (uc, qc, kc))
            ya_l = _mla_attention(ql, kl, vl, kc, vc)
            ya_c = None if last else _mla_attention(qc, None, None, kc, vc)
            wg, bg = _lru_gate_weights(lru_w_a[j], lru_b_a[j], lru_w_x[j], lru_b_x[j])
            yb_l, yb_c = _rglru(ul, uc, lru_conv_w[j], lru_conv_b[j], wg, bg, lru_lam[j])
        else:
            w_in = _od_in_weight(od_w_in[j])
            na_gains = (jnp.tile(na_q_gain[j] * NA_HD ** -0.5, NA_HEADS).reshape(1, NA_HEADS * NA_HD),
                        jnp.tile(na_k_gain[j], NA_HEADS).reshape(1, NA_HEADS * NA_HD))
            ul, nvt = _norm_mod_matmul(xl, norm_mix[l], mods, row_l, w_in, na_gains)
            uc, nvtc = _norm_mod_matmul(xc, norm_mix[l], mods, row_c, w_in, na_gains)
            uc = uc.reshape(B, Nc, OD_PAD)
            if not last:
                raise NotImplementedError("context outputs of the odd-layer mixers are only needed when depth > 2")
            wa = jnp.stack([jnp.pad(gla_w_a[j, d], ((d * GLA_LR, LANE - (d + 1) * GLA_LR), (0, 0))) for d in range(2)])
            ya_l = _gla(ul, uc, wa.astype(BF16), gla_b_a[j].reshape(2, 1, GLA_QK_W), gla_o_gain[j].reshape(1, GLA_DV))
            ya_c = None
            yb_l = _natten(ul, nvt, uc, nvtc, _na_bias_tables(na_rpb[j], N // GRID_W))
            yb_c = None
        xl = _out_proj_mlp(xl, ya_l, yb_l, wo, norm_mlp[l], mods, row_l, w1, w2, l)
        if not last:
            xc = _out_proj_mlp(xc, ya_c.reshape(1, B * Nc, -1), yb_c.reshape(1, B * Nc, -1), wo, norm_mlp[l], mods,
                               row_c, w1, w2, l)
    return xl
```

```python
import functools

import numpy as np
import jax
import jax.numpy as jnp
from jax import lax
from jax.experimental import pallas as pl
from jax.experimental.pallas import tpu as pltpu

F32 = jnp.float32
BF16 = jnp.bfloat16

D_MODEL = 1024
GRID_W = 64
EPS = 1e-6
LOG2_E = 1.4426950408889634
ROPE_BASE = 10000.0
D_FF = 4 * D_MODEL

MLA_HEADS = 8
MLA_NOPE = 64
MLA_ROPE = 32
MLA_QK = MLA_NOPE + MLA_ROPE
MLA_V = 64
MLA_Q_LORA = 256
MLA_KV_LORA = 128

LRU_W = 512
LRU_BLOCKS = 8
LRU_BS = LRU_W // LRU_BLOCKS
LRU_C = 8.0
CONV_W = 4
CONV_LEFT = 2

GLA_HEADS = 4
GLA_DK = 64
GLA_DV = 128
GLA_LR = 16
GLA_TAU = 16.0
GLA_CHUNK = 64

NA_HD = 64
NA_HEADS = 8
NA_WIN_ROWS = 8
NA_WIN_COLS = 16

LANE = 128
HEAD_SLOT = 128

EV_UQ, EV_UKV, EV_UKR, EV_UX, EV_UG = 0, 256, 384, 512, 1024
EV_PAD = 1536
OD_GQ, OD_GK, OD_GV, OD_GG, OD_NQ, OD_NK, OD_NV, OD_GLR = 0, 256, 512, 1024, 1536, 2048, 2560, 3072
OD_PAD = 3200


def _cparams(semantics, vmem_mib):
    return pltpu.CompilerParams(dimension_semantics=semantics, vmem_limit_bytes=vmem_mib << 20)


def _rms(x, g):
    return x * lax.rsqrt(jnp.mean(x * x, axis=-1, keepdims=True) + EPS) * g


def _ada_kernel(c_ref, w_ref, b_ref, o_ref):
    cv = c_ref[...]
    s = cv * jax.nn.sigmoid(cv)
    o_ref[0] = jnp.dot(s.astype(BF16), w_ref[0].astype(BF16), preferred_element_type=F32) + b_ref[0]


def _modulation(cc, ada_w, ada_b):
    L, D, D6 = ada_w.shape
    R = cc.shape[0]
    tn = 1536
    return pl.pallas_call(
        _ada_kernel,
        out_shape=jax.ShapeDtypeStruct((L, R, D6), F32),
        grid=(L, D6 // tn),
        in_specs=[pl.BlockSpec((R, D), lambda l, j: (0, 0)),
                  pl.BlockSpec((1, D, tn), lambda l, j: (l, 0, j)),
                  pl.BlockSpec((1, 1, tn), lambda l, j: (l, 0, j))],
        out_specs=pl.BlockSpec((1, R, tn), lambda l, j: (l, 0, j)),
        compiler_params=_cparams(("arbitrary", "arbitrary"), 40),
        name="adaln_modulation",
    )(cc, ada_w, ada_b.reshape(L, 1, D6))


def _mod_spec(row0, k):
    return pl.BlockSpec((1, 1, D_MODEL), lambda g, i: (row0 + g, 0, k))


def _modulated_proj(x_ref, g_ref, sh_ref, sc_ref, w_ref):
    h = _rms(x_ref[0], g_ref[...]) * (1.0 + sc_ref[0]) + sh_ref[0]
    return jnp.dot(h.astype(BF16), w_ref[...], preferred_element_type=F32)


def _nmm_kernel(x_ref, g_ref, sh_ref, sc_ref, w_ref, o_ref):
    o_ref[0] = _modulated_proj(x_ref, g_ref, sh_ref, sc_ref, w_ref).astype(BF16)


def _nmm_na_kernel(x_ref, g_ref, sh_ref, sc_ref, w_ref, qg_ref, kg_ref, o_ref, vt_ref):
    u = _modulated_proj(x_ref, g_ref, sh_ref, sc_ref, w_ref)
    o_ref[0, :, :OD_NQ] = u[:, :OD_NQ].astype(BF16)
    o_ref[0, :, OD_NV:] = u[:, OD_NV:].astype(BF16)
    lane_lo = lax.broadcasted_iota(jnp.int32, (u.shape[0], LANE), 1) < NA_HD
    for base, gain_ref in ((OD_NQ, qg_ref), (OD_NK, kg_ref)):
        for p in range(NA_HEADS // 2):
            t = u[:, base + p * LANE:base + (p + 1) * LANE]
            sq = t * t
            s_lo = jnp.sum(jnp.where(lane_lo, sq, 0.0), axis=-1, keepdims=True)
            s_hi = jnp.sum(jnp.where(lane_lo, 0.0, sq), axis=-1, keepdims=True)
            ms = jnp.where(lane_lo, s_lo, s_hi) * (1.0 / NA_HD)
            gain = gain_ref[:, p * LANE:(p + 1) * LANE]
            o_ref[0, :, base + p * LANE:base + (p + 1) * LANE] = (t * lax.rsqrt(ms + EPS) * gain).astype(BF16)
    for p in range(NA_HEADS // 2):
        pair = u[:, OD_NV + p * LANE:OD_NV + (p + 1) * LANE]
        vt_ref[0, (2 * p) * HEAD_SLOT:(2 * p + 1) * HEAD_SLOT, :] = _value_slot_t(pair, NA_HD)
        vt_ref[0, (2 * p + 1) * HEAD_SLOT:(2 * p + 2) * HEAD_SLOT, :] = _value_slot_t(pltpu.roll(pair, NA_HD, 1), NA_HD)


def _norm_mod_matmul(x, g, mods, row0, w, na_gains=None, tm=512):
    G, M, D = x.shape
    assert M % tm == 0
    Nout = w.shape[1]
    in_specs = [pl.BlockSpec((1, tm, D), lambda b, i: (b, i, 0)),
                pl.BlockSpec((1, D), lambda b, i: (0, 0)),
                _mod_spec(row0, 0), _mod_spec(row0, 1),
                pl.BlockSpec((D, Nout), lambda b, i: (0, 0))]
    u_shape = jax.ShapeDtypeStruct((G, M, Nout), BF16)
    u_spec = pl.BlockSpec((1, tm, Nout), lambda b, i: (b, i, 0))
    if na_gains is None:
        kern, args, out_shape, out_specs = _nmm_kernel, (), u_shape, u_spec
    else:
        VS = NA_HEADS * HEAD_SLOT
        kern, args = _nmm_na_kernel, tuple(na_gains)
        in_specs += [pl.BlockSpec((1, NA_HEADS * NA_HD), lambda b, i: (0, 0))] * 2
        out_shape = (u_shape, jax.ShapeDtypeStruct((G, VS, M), BF16))
        out_specs = (u_spec, pl.BlockSpec((1, VS, tm), lambda b, i: (b, 0, i)))
    return pl.pallas_call(
        kern,
        out_shape=out_shape,
        grid=(G, M // tm),
        in_specs=in_specs,
        out_specs=out_specs,
        compiler_params=_cparams(("parallel", "arbitrary"), 48),
        name="norm_mod_in_proj",
    )(x, g.reshape(1, D), mods, mods, w, *args)


MLP_TF = 1024


def _out_mlp_kernel(x_ref, ya_ref, yb_ref, woa_ref, wob_ref, g_ref, m2_ref, m3_ref, m4_ref, m5_ref,
                    w1_ref, w2_ref, o_ref, h_ref, a_ref):
    y = (jnp.dot(ya_ref[0], woa_ref[...], preferred_element_type=F32)
         + jnp.dot(yb_ref[0], wob_ref[...], preferred_element_type=F32))
    x1 = x_ref[0] + m2_ref[0] * y
    o_ref[0] = x1
    h_ref[...] = (_rms(x1, g_ref[...]) * (1.0 + m4_ref[0]) + m3_ref[0]).astype(BF16)
    for f in range(a_ref.shape[1] // MLP_TF):
        cols = slice(f * MLP_TF, (f + 1) * MLP_TF)
        a = jnp.maximum(jnp.dot(h_ref[...], w1_ref[:, cols], preferred_element_type=F32), 0.0)
        a_ref[:, cols] = (a * a).astype(BF16)
    o_ref[0] += m5_ref[0] * jnp.dot(a_ref[...], w2_ref[...], preferred_element_type=F32)


def _out_proj_mlp(x, ya, yb, w_out, g_mlp, mods, row0, w1, w2, layer, tm=512):
    G, M, D = x.shape
    assert M % tm == 0
    Wa, Wb = ya.shape[-1], yb.shape[-1]
    assert Wa == Wb and Wa + Wb == w_out.shape[1]
    FF = w1.shape[2]

    def resident(shape, row_block=0):
        return pl.BlockSpec((None,) + shape, lambda b, i: (layer, row_block, 0), pipeline_mode=pl.Buffered(1))

    return pl.pallas_call(
        _out_mlp_kernel,
        out_shape=jax.ShapeDtypeStruct((G, M, D), F32),
        grid=(G, M // tm),
        in_specs=[pl.BlockSpec((1, tm, D), lambda b, i: (b, i, 0)),
                  pl.BlockSpec((1, tm, Wa), lambda b, i: (b, i, 0)),
                  pl.BlockSpec((1, tm, Wb), lambda b, i: (b, i, 0)),
                  resident((Wa, D), 0), resident((Wb, D), 1),
                  pl.BlockSpec((1, D), lambda b, i: (0, 0)),
                  _mod_spec(row0, 2), _mod_spec(row0, 3), _mod_spec(row0, 4), _mod_spec(row0, 5),
                  resident((D, FF)), resident((FF, D))],
        out_specs=pl.BlockSpec((1, tm, D), lambda b, i: (b, i, 0)),
        scratch_shapes=[pltpu.VMEM((tm, D), BF16), pltpu.VMEM((tm, FF), BF16)],
        compiler_params=_cparams(("parallel", "arbitrary"), 52),
        name="out_proj_mlp",
    )(x, ya, yb, w_out, w_out, g_mlp.reshape(1, D), mods, mods, mods, mods, w1, w2)


def _nmm_mla_kernel(x_ref, g_ref, sh_ref, sc_ref, w_ref, tq1_ref, tq2_ref, tk1_ref, tka_ref, tkb_ref, qn_ref, kvn_ref,
                    wuq_ref, wuqr_ref, wuk_ref, wuv_ref, o_ref, q_out, k_out, vt_out):
    u = _modulated_proj(x_ref, g_ref, sh_ref, sc_ref, w_ref)
    o_ref[0] = u[:, EV_UX:].astype(BF16)
    qn = _rms(u[:, EV_UQ:EV_UQ + MLA_Q_LORA], qn_ref[...]).astype(BF16)
    kvn = _rms(u[:, EV_UKV:EV_UKV + MLA_KV_LORA], kvn_ref[...]).astype(BF16)
    q_all = jnp.dot(qn, wuq_ref[...], preferred_element_type=F32)
    q_rot = jnp.dot(qn, wuqr_ref[...], preferred_element_type=F32)
    k_all = jnp.dot(kvn, wuk_ref[...], preferred_element_type=F32)
    v_all = jnp.dot(kvn, wuv_ref[...], preferred_element_type=F32)
    ukr = u[:, EV_UKR:EV_UKR + HEAD_SLOT]
    quarter = MLA_ROPE // 4
    k_rot = (pltpu.roll(ukr, HEAD_SLOT - quarter, 1) * tka_ref[...] + pltpu.roll(ukr, quarter, 1) * tkb_ref[...])
    ukr_sq = jnp.sum(ukr * ukr, axis=-1, keepdims=True)
    inv_n = 1.0 / MLA_QK
    for h in range(MLA_HEADS):
        sl = slice(h * HEAD_SLOT, (h + 1) * HEAD_SLOT)
        qh = q_all[:, sl]
        rq = lax.rsqrt(jnp.sum(qh * qh, axis=-1, keepdims=True) * inv_n + EPS)
        q_out[0, :, sl] = (rq * (qh * tq1_ref[...] + q_rot[:, sl] * tq2_ref[...])).astype(BF16)
        kn = k_all[:, sl]
        rk = lax.rsqrt((jnp.sum(kn * kn, axis=-1, keepdims=True) + ukr_sq) * inv_n + EPS)
        k_out[0, :, sl] = (rk * ((kn + ukr) * tk1_ref[...] + k_rot)).astype(BF16)
        vt_out[0, sl, :] = _value_slot_t(v_all[:, sl], MLA_V)


def _rope_partner(t):
    quarter = MLA_ROPE // 4
    t4 = t.reshape(t.shape[:-1] + (2, 2, quarter))
    return jnp.stack([t4[..., 1, :], t4[..., 0, :]], axis=-2).reshape(t.shape)


def _rope_tables(n, q_gain, k_gain, use_rope):
    quarter = MLA_ROPE // 4
    if use_rope:
        pos = jnp.arange(n)
        inv = ROPE_BASE ** (-jnp.arange(0, MLA_ROPE // 2, 2, dtype=F32) / (MLA_ROPE // 2))
        ang_r = (pos // GRID_W).astype(F32)[:, None] * inv[None, :]
        ang_c = (pos % GRID_W).astype(F32)[:, None] * inv[None, :]
        ang = jnp.concatenate([ang_r, ang_r, ang_c, ang_c], axis=-1)
        cos, sin = jnp.cos(ang), jnp.sin(ang)
    else:
        cos, sin = jnp.ones((n, MLA_ROPE), F32), jnp.zeros((n, MLA_ROPE), F32)
    first = (np.arange(MLA_ROPE) % (2 * quarter)) < quarter
    pad = jnp.zeros((n, HEAD_SLOT - MLA_QK), F32)

    def slot(nope, rope):
        return jnp.concatenate([jnp.broadcast_to(nope, (n, MLA_NOPE)), rope, pad], axis=-1)

    zero = jnp.zeros((MLA_NOPE,), F32)
    qg, kg = q_gain * (MLA_QK ** -0.5 * LOG2_E), k_gain
    tq1 = slot(qg[:MLA_NOPE], cos * qg[MLA_NOPE:])
    tq2 = slot(zero, sin * _rope_partner(qg[MLA_NOPE:]))
    tk1 = slot(kg[:MLA_NOPE], cos * kg[MLA_NOPE:])
    ksin = sin * _rope_partner(kg[MLA_NOPE:])
    tka = slot(zero, jnp.where(first, -ksin, 0.0))
    tkb = slot(zero, jnp.where(first, 0.0, ksin))
    return tq1, tq2, tk1, tka, tkb


def _rope_partner_weight(wuq):
    k = wuq.shape[0]
    quarter = MLA_ROPE // 4
    w = wuq.reshape(k, MLA_HEADS, HEAD_SLOT)
    rope = w[:, :, MLA_NOPE:MLA_QK].reshape(k, MLA_HEADS, 2, 2, quarter)
    rot = jnp.stack([-rope[:, :, :, 1, :], rope[:, :, :, 0, :]], axis=3).reshape(k, MLA_HEADS, MLA_ROPE)
    out = jnp.concatenate([jnp.zeros_like(w[:, :, :MLA_NOPE]), rot, jnp.zeros_like(w[:, :, MLA_QK:])], axis=-1)
    return out.reshape(k, MLA_HEADS * HEAD_SLOT)


def _ev_in_proj(x, g, mods, row0, w, tables, qn, kvn, wuq, wuqr, wuk, wuv, tm=512):
    G, M, D = x.shape
    assert M % tm == 0
    HS = MLA_HEADS * HEAD_SLOT
    full = lambda shape: pl.BlockSpec(shape, lambda b, i: (0,) * len(shape))
    tab = pl.BlockSpec((tm, HEAD_SLOT), lambda b, i: (i, 0))
    rows = lambda width: pl.BlockSpec((1, tm, width), lambda b, i: (b, i, 0))
    return pl.pallas_call(
        _nmm_mla_kernel,
        out_shape=(jax.ShapeDtypeStruct((G, M, 2 * LRU_W), BF16), jax.ShapeDtypeStruct((G, M, HS), BF16),
                   jax.ShapeDtypeStruct((G, M, HS), BF16), jax.ShapeDtypeStruct((G, HS, M), BF16)),
        grid=(G, M // tm),
        in_specs=[rows(D), full((1, D)), _mod_spec(row0, 0), _mod_spec(row0, 1), full((D, EV_PAD)),
                  tab, tab, tab, tab, tab,
                  full((1, MLA_Q_LORA)), full((1, MLA_KV_LORA)),
                  full((MLA_Q_LORA, HS)), full((MLA_Q_LORA, HS)), full((MLA_KV_LORA, HS)), full((MLA_KV_LORA, HS))],
        out_specs=(rows(2 * LRU_W), rows(HS), rows(HS), pl.BlockSpec((1, HS, tm), lambda b, i: (b, 0, i))),
        compiler_params=_cparams(("parallel", "arbitrary"), 48),
        name="ev_in_proj_mla_prep",
    )(x, g.reshape(1, D), mods, mods, w, *tables, qn, kvn, wuq, wuqr, wuk, wuv)


def _dot_nt(a, b):
    return lax.dot_general(a, b, (((1,), (1,)), ((), ())), preferred_element_type=F32)


def _value_slot_t(v, ones_row):
    lane = lax.broadcasted_iota(jnp.int32, v.shape, 1)
    slot = jnp.where(lane < ones_row, v, jnp.where(lane == ones_row, 1.0, 0.0))
    return jnp.transpose(slot).astype(BF16)


ATTN_GROUP = 8


def _col_max(a, rows=8):
    parts = [a[r:r + rows] for r in range(0, a.shape[0], rows)]
    while len(parts) > 1:
        parts = [jnp.maximum(parts[i], parts[i + 1]) if i + 1 < len(parts) else parts[i]
                 for i in range(0, len(parts), 2)]
    return jnp.max(parts[0], axis=0, keepdims=True)


def _attend_t(heads):
    scores = []
    for q, keys, _, biases in heads:
        s = [_dot_nt(k, q) for k in keys]
        if biases is not None:
            s = [a if b is None else a + b for a, b in zip(s, biases)]
        scores.append(s)
    maxes = [functools.reduce(jnp.maximum, [_col_max(a) for a in s]) for s in scores]
    outs = []
    for (_, _, values_t, _), s, m in zip(heads, scores, maxes):
        out_t = None
        for a, vt in zip(s, values_t):
            part = jnp.dot(vt, jnp.exp2(a - m).astype(BF16), preferred_element_type=F32)
            out_t = part if out_t is None else out_t + part
        outs.append(out_t)
    return outs


def _pair_output(slots_t, dv):
    halves = [t[:dv] / t[dv:dv + 1] for t in slots_t]
    return jnp.transpose(jnp.concatenate(halves, axis=0))


def _mla_attn_kernel(*refs, with_latent):
    if with_latent:
        q_ref, kl_ref, vtl_ref, kc_ref, vtc_ref, o_ref = refs
    else:
        q_ref, kc_ref, vtc_ref, o_ref = refs
    for h0 in range(0, MLA_HEADS, ATTN_GROUP):
        heads = []
        for h in range(h0, h0 + ATTN_GROUP):
            sl = slice(h * HEAD_SLOT, (h + 1) * HEAD_SLOT)
            keys, values_t = [kc_ref[0, :, sl]], [vtc_ref[0, sl, :]]
            if with_latent:
                keys.append(kl_ref[0, :, sl])
                values_t.append(vtl_ref[0, sl, :])
            heads.append((q_ref[0, :, sl], keys, values_t, None))
        outs = _attend_t(heads)
        for i in range(0, ATTN_GROUP, 2):
            hp = (h0 + i) // 2
            o_ref[0, :, hp * LANE:(hp + 1) * LANE] = _pair_output(outs[i:i + 2], MLA_V).astype(BF16)


def _mla_attention(q, kl, vtl, kc, vtc, tq=256):
    B, M, HS = q.shape
    assert M % tq == 0
    HV = MLA_HEADS * MLA_V
    Nc = kc.shape[1]
    with_latent = kl is not None
    whole = lambda n, w: pl.BlockSpec((1, n, w), lambda b, i: (b, 0, 0))
    in_specs = [pl.BlockSpec((1, tq, HS), lambda b, i: (b, i, 0))]
    args = [q]
    if with_latent:
        in_specs += [whole(kl.shape[1], HS), whole(HS, kl.shape[1])]
        args += [kl, vtl]
    in_specs += [whole(Nc, HS), pl.BlockSpec((1, HS, Nc), lambda b, i: (0, 0, b))]
    args += [kc, vtc]
    return pl.pallas_call(
        functools.partial(_mla_attn_kernel, with_latent=with_latent),
        out_shape=jax.ShapeDtypeStruct((B, M, HV), BF16),
        grid=(B, M // tq),
        in_specs=in_specs,
        out_specs=pl.BlockSpec((1, tq, HV), lambda b, i: (b, i, 0)),
        compiler_params=_cparams(("parallel", "arbitrary"), 48),
        name="mla_attention",
    )(*args)


LRU_CW = 256
LRU_HALO = 16
LRU_TN = 256


def _gelu_tanh(x):
    return 0.5 * x * (1.0 + jnp.tanh(0.7978845608028654 * (x + 0.044715 * (x * x * x))))


def _scan_group(a, bv, h, reverse):
    row = lax.broadcasted_iota(jnp.int32, a.shape, 0)
    for s in (1, 2, 4):
        if reverse:
            keep = row < 8 - s
            shift = 8 - s
        else:
            keep = row >= s
            shift = s
        a_s = jnp.where(keep, pltpu.roll(a, shift, 0), 1.0)
        b_s = jnp.where(keep, pltpu.roll(bv, shift, 0), 0.0)
        bv = a * b_s + bv
        a = a * a_s
    hs = a * h + bv
    return hs, (hs[0:1, :] if reverse else hs[7:8, :])


def _lru_kernel(uxl_ref, ugl_ref, uxc_ref, ugc_ref, cw_ref, cb_ref, wg_ref, bg_ref, lam_ref, yl_ref, yc_ref,
                xpl_ref, xpc_ref, xcv_ref, af_ref, bf_ref, ab_ref, bb_ref):
    N, Nc = uxl_ref.shape[1], uxc_ref.shape[1]
    NT = N + Nc
    C = LRU_CW
    H = LRU_HALO

    def conv(src_ref, pad_ref, n, row0):
        pad_ref[0:H, :] = jnp.zeros((H, C), F32)
        pad_ref[H + n:H + n + H, :] = jnp.zeros((H, C), F32)
        pad_ref[H:H + n, :] = src_ref[0].astype(F32)
        y = cb_ref[...] + pad_ref[H - CONV_LEFT:H - CONV_LEFT + n, :] * cw_ref[0:1, :]
        for j in range(1, CONV_W):
            y = y + pad_ref[H - CONV_LEFT + j:H - CONV_LEFT + j + n, :] * cw_ref[j:j + 1, :]
        xcv_ref[row0:row0 + n, :] = y

    conv(uxc_ref, xpc_ref, Nc, 0)
    conv(uxl_ref, xpl_ref, N, Nc)

    lam = lam_ref[...]
    c_half = (-0.5 * LRU_C) * (jnp.maximum(-lam, 0.0) + jnp.log1p(jnp.exp(-jnp.abs(lam))))

    def coeff_chunk(i, carry):
        r0 = pl.multiple_of(i * LRU_TN, LRU_TN)
        x = xcv_ref[pl.ds(r0, LRU_TN), :]
        t = jnp.tanh(jnp.dot(x.astype(BF16), wg_ref[0], preferred_element_type=F32) + bg_ref[0])
        hx = 0.5 * x
        for d, (a_ref, b_ref) in enumerate(((af_ref, bf_ref), (ab_ref, bb_ref))):
            c = c_half[d:d + 1, :]
            a = jnp.exp(c * t[:, (2 * d) * C:(2 * d + 1) * C] + c)
            a_ref[pl.ds(r0, LRU_TN), :] = a
            gated_x = hx * t[:, (2 * d + 1) * C:(2 * d + 2) * C] + hx
            b_ref[pl.ds(r0, LRU_TN), :] = jnp.sqrt(1.0 - a * a) * gated_x
        return carry

    lax.fori_loop(0, NT // LRU_TN, coeff_chunk, 0)

    ngc, ngt = Nc // 8, NT // 8

    def scan_step(i, carry):
        hf, hb = carry
        rf = pl.multiple_of(i * 8, 8)
        rb = pl.multiple_of(jnp.where(i < ngc, ngc - 1 - i, ngt + ngc - 1 - i) * 8, 8)
        hs_f, hf = _scan_group(af_ref[pl.ds(rf, 8), :], bf_ref[pl.ds(rf, 8), :], hf, False)
        hs_b, hb = _scan_group(ab_ref[pl.ds(rb, 8), :], bb_ref[pl.ds(rb, 8), :], hb, True)
        bf_ref[pl.ds(rf, 8), :] = hs_f
        bb_ref[pl.ds(rb, 8), :] = hs_b
        return hf, hb

    zero = jnp.zeros((1, C), F32)
    lax.fori_loop(0, ngt, scan_step, (zero, zero), unroll=4)

    def out_chunk(i, carry):
        r0 = pl.multiple_of(i * LRU_TN, LRU_TN)
        hsum = bf_ref[pl.ds(Nc + r0, LRU_TN), :] + bb_ref[pl.ds(Nc + r0, LRU_TN), :]
        gate = _gelu_tanh(ugl_ref[0, pl.ds(r0, LRU_TN), :].astype(F32))
        yl_ref[0, pl.ds(r0, LRU_TN), :] = (hsum * gate).astype(BF16)
        return carry

    lax.fori_loop(0, N // LRU_TN, out_chunk, 0)
    yc_ref[0] = ((bf_ref[0:Nc, :] + bb_ref[0:Nc, :]) * _gelu_tanh(ugc_ref[0].astype(F32))).astype(BF16)


def _rglru(ul, uc, conv_w, conv_b, wg, bg, lam):
    B, N, _ = ul.shape
    Nc = uc.shape[1]
    C = LRU_CW
    nh = LRU_W // C
    NT = N + Nc
    col = lambda base: (lambda b, j: (b, 0, base // C + j))
    par = lambda rows: pl.BlockSpec((rows, C), lambda b, j: (0, j))
    return pl.pallas_call(
        _lru_kernel,
        out_shape=(jax.ShapeDtypeStruct((B, N, LRU_W), BF16), jax.ShapeDtypeStruct((B, Nc, LRU_W), BF16)),
        grid=(B, nh),
        in_specs=[pl.BlockSpec((1, N, C), col(0)), pl.BlockSpec((1, N, C), col(LRU_W)),
                  pl.BlockSpec((1, Nc, C), col(0)), pl.BlockSpec((1, Nc, C), col(LRU_W)),
                  par(CONV_W), par(1),
                  pl.BlockSpec((1, C, 4 * C), lambda b, j: (j, 0, 0)),
                  pl.BlockSpec((1, 1, 4 * C), lambda b, j: (j, 0, 0)),
                  par(2)],
        out_specs=(pl.BlockSpec((1, N, C), lambda b, j: (b, 0, j)),
                   pl.BlockSpec((1, Nc, C), lambda b, j: (b, 0, j))),
        scratch_shapes=[pltpu.VMEM((N + 2 * LRU_HALO, C), F32), pltpu.VMEM((Nc + 2 * LRU_HALO, C), F32),
                        pltpu.VMEM((NT, C), F32)] + [pltpu.VMEM((NT, C), F32)] * 4,
        compiler_params=_cparams(("parallel", "arbitrary"), 48),
        name="rglru",
    )(ul, ul, uc, uc, conv_w, conv_b.reshape(1, LRU_W), wg, bg, lam)


def _lru_gate_weights(w_a, b_a, w_x, b_x):
    C = LRU_CW
    nh = LRU_W // C
    kb = C // LRU_BS

    def dense(w):
        w = w.reshape(nh, kb, LRU_BS, LRU_BS)
        eye = jnp.eye(kb, dtype=w.dtype)
        return jnp.einsum('hkij,kl->hkilj', w, eye).reshape(nh, C, C)

    wg = jnp.concatenate([dense(w_a[0]), dense(w_x[0]), dense(w_a[1]), dense(w_x[1])], axis=-1)
    bg = jnp.stack([b_a[0], b_x[0], b_a[1], b_x[1]], axis=0).reshape(4, nh, C)
    bg = jnp.transpose(bg, (1, 0, 2)).reshape(nh, 1, 4 * C)
    return (0.5 * wg).astype(BF16), 0.5 * bg


GLA_SC = 256
GLA_UNROLL = 2
GLA_QK_W = GLA_HEADS * GLA_DK
GLA_V_W = GLA_HEADS * GLA_DV


def _split3(x):
    hi = x.astype(BF16)
    r1 = x - hi.astype(F32)
    mid = r1.astype(BF16)
    lo = (r1 - mid.astype(F32)).astype(BF16)
    return hi, mid, lo


def _gla_kernel(ql_ref, kl_ref, vl_ref, gl_ref, lrl_ref, kc_ref, vc_ref, lrc_ref, wa_ref, ba_ref, og_ref,
                y_ref, oacc_ref, st_ref):
    N, Nc = ql_ref.shape[1], kc_ref.shape[1]
    T = GLA_SC
    CH = GLA_CHUNK
    npair = GLA_HEADS // 2
    row = lax.broadcasted_iota(jnp.int32, (T, T), 0)
    colm = lax.broadcasted_iota(jnp.int32, (T, T), 1)
    same_chunk = (row // CH) == (colm // CH)
    causal = (same_chunk & (colm <= row), same_chunk & (colm >= row))
    tri = tuple(jnp.where(c, 1.0, 0.0).astype(BF16) for c in causal)
    lane_lo = lax.broadcasted_iota(jnp.int32, (T, LANE), 1) < GLA_DK
    srow = lax.broadcasted_iota(jnp.int32, (LANE, 2 * GLA_DV), 0)
    scol = lax.broadcasted_iota(jnp.int32, (LANE, 2 * GLA_DV), 1)
    state_mask = (srow < GLA_DK) == (scol < GLA_DV)
    qscale = GLA_DK ** -0.5

    def superchunks(streams, q_ref, k_ref, v_ref, lr_ref, with_output):
        nch = T // CH
        work = []
        for d, r0 in streams:
            z = jnp.dot(lr_ref[0, pl.ds(r0, T), :], wa_ref[d], preferred_element_type=F32) + ba_ref[d]
            work.append(dict(d=d, r0=r0, split=_split3(jax.nn.log_sigmoid(z) * (1.0 / GLA_TAU))))
        for w in work:
            d = w['d']
            cum = sum(jnp.dot(tri[d], t, preferred_element_type=F32) for t in w['split'])
            last = (CH - 1) if d == 0 else 0
            tot = [cum[c * CH + last:c * CH + last + 1, :] for c in range(nch)]
            tot_rows = jnp.concatenate([jnp.broadcast_to(t, (CH, GLA_QK_W)) for t in tot], axis=0)
            k = k_ref[0, pl.ds(w['r0'], T), :].astype(F32)
            w.update(cum=cum, tot=tot, k=k, v=v_ref[0, pl.ds(w['r0'], T), :],
                     k_dec=(k * jnp.exp(tot_rows - cum)).astype(BF16))
        for w in work:
            ds, dec = {}, {}
            for c in range(nch):
                rs = slice(c * CH, (c + 1) * CH)
                for p in range(npair):
                    kd = w['k_dec'][rs, p * LANE:(p + 1) * LANE]
                    vv = w['v'][rs, p * 2 * GLA_DV:(p + 1) * 2 * GLA_DV]
                    kv = lax.dot_general(kd, vv, (((0,), (0,)), ((), ())), preferred_element_type=F32)
                    ds[c, p] = jnp.where(state_mask, kv, 0.0)
                    dec_row = jnp.exp(w['tot'][c][:, p * LANE:(p + 1) * LANE])
                    dec_col = jnp.transpose(jnp.broadcast_to(dec_row, (LANE, LANE)))
                    dec[c, p] = jnp.concatenate([dec_col, dec_col], axis=1)
            w.update(ds=ds, dec=dec)
        for w in work:
            d = w['d']
            order = range(nch) if d == 0 else range(nch - 1, -1, -1)
            s_in = {}
            for p in range(npair):
                s = st_ref[d, p]
                for c in order:
                    s_in[c, p] = s.astype(BF16)
                    s = w['dec'][c, p] * s + w['ds'][c, p]
                st_ref[d, p] = s
            w.update(s_in=s_in)
        if not with_output:
            return
        for w in work:
            q = q_ref[0, pl.ds(w['r0'], T), :].astype(F32)
            w.update(q_dec=((q * qscale) * jnp.exp(w['cum'])).astype(BF16),
                     k_inv=(w['k'] * jnp.exp(-w['cum'])).astype(BF16))
        for w in work:
            d, r0 = w['d'], w['r0']
            for p in range(npair):
                qp = w['q_dec'][:, p * LANE:(p + 1) * LANE]
                kp = w['k_inv'][:, p * LANE:(p + 1) * LANE]
                o_inter = jnp.concatenate(
                    [jnp.dot(qp[c * CH:(c + 1) * CH], w['s_in'][c, p], preferred_element_type=F32)
                     for c in range(nch)], axis=0)
                for j in range(2):
                    h = 2 * p + j
                    qm = jnp.where(lane_lo if j == 0 else jnp.logical_not(lane_lo), qp, jnp.zeros_like(qp))
                    att = jnp.where(causal[d], _dot_nt(qm, kp), 0.0).astype(BF16)
                    o = (jnp.dot(att, w['v'][:, h * GLA_DV:(h + 1) * GLA_DV], preferred_element_type=F32)
                         + o_inter[:, j * GLA_DV:(j + 1) * GLA_DV])
                    oacc_ref[d, pl.ds(r0, T), h * GLA_DV:(h + 1) * GLA_DV] = o

    st_ref[...] = jnp.zeros_like(st_ref)
    ncs, nls = Nc // T, N // T

    def ctx_body(i, carry):
        streams = [(0, pl.multiple_of(i * T, T)), (1, pl.multiple_of((ncs - 1 - i) * T, T))]
        superchunks(streams, None, kc_ref, vc_ref, lrc_ref, False)
        return carry

    def lat_body(i, carry):
        streams = []
        for u in range(GLA_UNROLL):
            streams += [(0, pl.multiple_of((GLA_UNROLL * i + u) * T, T)),
                        (1, pl.multiple_of((nls - 1 - GLA_UNROLL * i - u) * T, T))]
        superchunks(streams, ql_ref, kl_ref, vl_ref, lrl_ref, True)
        return carry

    lax.fori_loop(0, ncs, ctx_body, 0)
    lax.fori_loop(0, nls // GLA_UNROLL, lat_body, 0)

    def fin(i, carry):
        r0 = pl.multiple_of(i * T, T)
        g = gl_ref[0, pl.ds(r0, T), :].astype(F32)
        for h in range(GLA_HEADS):
            sl = slice(h * GLA_DV, (h + 1) * GLA_DV)
            gh = g[:, sl]
            o = oacc_ref[0, pl.ds(r0, T), sl] + oacc_ref[1, pl.ds(r0, T), sl]
            y_ref[0, pl.ds(r0, T), sl] = (_rms(o, og_ref[...]) * (gh * jax.nn.sigmoid(gh))).astype(BF16)
        return carry

    lax.fori_loop(0, N // T, fin, 0)


def _gla(ul, uc, wa, ba, o_gain):
    B, N, _ = ul.shape
    Nc = uc.shape[1]
    blk = lambda n, w, base: pl.BlockSpec((1, n, w), lambda b: (b, 0, base // w))
    full = lambda shape: pl.BlockSpec(shape, lambda b: (0,) * len(shape))
    return pl.pallas_call(
        _gla_kernel,
        out_shape=jax.ShapeDtypeStruct((B, N, GLA_V_W), BF16),
        grid=(B,),
        in_specs=[blk(N, GLA_QK_W, OD_GQ), blk(N, GLA_QK_W, OD_GK), blk(N, GLA_V_W, OD_GV),
                  blk(N, GLA_V_W, OD_GG), blk(N, LANE, OD_GLR),
                  blk(Nc, GLA_QK_W, OD_GK), blk(Nc, GLA_V_W, OD_GV), blk(Nc, LANE, OD_GLR),
                  full((2, LANE, GLA_QK_W)), full((2, 1, GLA_QK_W)), full((1, GLA_DV))],
        out_specs=pl.BlockSpec((1, N, GLA_V_W), lambda b: (b, 0, 0)),
        scratch_shapes=[pltpu.VMEM((2, N, GLA_V_W), F32),
                        pltpu.VMEM((2, GLA_HEADS // 2, LANE, 2 * GLA_DV), F32)],
        compiler_params=_cparams(("parallel",), 48),
        name="gla",
    )(ul, ul, ul, ul, ul, uc, uc, uc, wa, ba, o_gain)


NA_W = NA_HEADS * NA_HD
NA_QROWS = 4
NA_KROWS = 12
NA_TQ = NA_QROWS * GRID_W
NA_TK = NA_KROWS * GRID_W
NA_NEG = -1e30


def _na_kernel(q_ref, k0_ref, k1_ref, k2_ref, vt0_ref, vt1_ref, vt2_ref, kc_ref, vtc_ref, bias_ref, o_ref):
    lane_lo = lax.broadcasted_iota(jnp.int32, (NA_TQ, LANE), 1) < NA_HD
    k_refs = (k0_ref, k1_ref, k2_ref, kc_ref)
    vt_refs = (vt0_ref, vt1_ref, vt2_ref, vtc_ref)
    for h0 in range(0, NA_HEADS, ATTN_GROUP):
        heads = []
        for h in range(h0, h0 + ATTN_GROUP):
            sl = slice((h // 2) * LANE, (h // 2 + 1) * LANE)
            hs = slice(h * HEAD_SLOT, (h + 1) * HEAD_SLOT)
            qp = q_ref[0, :, sl]
            qm = jnp.where(lane_lo if h % 2 == 0 else jnp.logical_not(lane_lo), qp, jnp.zeros_like(qp))
            biases = [bias_ref[0, h, t * NA_TQ:(t + 1) * NA_TQ, :] for t in range(3)] + [None]
            heads.append((qm, [r[0, :, sl] for r in k_refs], [r[0, hs, :] for r in vt_refs], biases))
        outs = _attend_t(heads)
        for i in range(0, ATTN_GROUP, 2):
            hp = (h0 + i) // 2
            o_ref[0, :, hp * LANE:(hp + 1) * LANE] = _pair_output(outs[i:i + 2], NA_HD).astype(BF16)


def _na_bias_tables(rpb, rows):
    H, ndi, ndj = rpb.shape
    rpb = rpb * LOG2_E
    W = GRID_W
    half = NA_WIN_COLS - 1
    zeros = jnp.zeros((H, ndi, W - 1 - half), rpb.dtype)
    vec = jnp.concatenate([rpb[:, :, half:], zeros, zeros, rpb[:, :, :half]], axis=-1)
    toep = jnp.tile(vec, (1, 1, W))[:, :, :W * (2 * W - 2)].reshape(H, ndi, W, 2 * W - 2)[..., :W]
    qc = np.arange(W)[:, None]
    kc = np.arange(W)[None, :]
    c0 = np.clip(qc - NA_WIN_COLS // 2, 0, W - NA_WIN_COLS)
    toep = jnp.where(jnp.asarray((kc >= c0) & (kc < c0 + NA_WIN_COLS)), toep, NA_NEG)
    toep = jnp.swapaxes(toep, 2, 3)
    nblk = rows // NA_QROWS
    guard = NA_KROWS - NA_WIN_ROWS
    toep = jnp.pad(toep, ((0, 0), (guard, guard), (0, 0), (0, 0)), constant_values=NA_NEG)
    tabs = []
    for m in (0, 1, nblk - 1):
        kb = min(max(m - 1, 0), nblk - NA_KROWS // NA_QROWS)
        qr = m * NA_QROWS + np.arange(NA_QROWS)[None, :]
        kr = kb * NA_QROWS + np.arange(NA_KROWS)[:, None]
        r0 = np.clip(qr - NA_WIN_ROWS // 2, 0, rows - NA_WIN_ROWS)
        row_ok = (kr >= r0) & (kr < r0 + NA_WIN_ROWS)
        di0 = (kb - m) * NA_QROWS + NA_WIN_ROWS - 1 + guard
        per_a = [toep[:, di0 - a:di0 - a + NA_KROWS] for a in range(NA_QROWS)]
        ok = np.repeat(row_ok, W, axis=1)[None, :, None, :]
        tab = jnp.where(jnp.asarray(ok), jnp.concatenate(per_a, axis=-1), NA_NEG)
        tabs.append(tab.reshape(H, NA_TK, NA_TQ))
    return jnp.stack(tabs, axis=0)


def _natten(ul, vt, uc, vtc, bias):
    B, N, _ = ul.shape
    Nc = uc.shape[1]
    nblk = N // NA_TQ
    kmax = nblk - NA_KROWS // NA_QROWS
    VS = NA_HEADS * HEAD_SLOT
    qcol, kcol = OD_NQ // NA_W, OD_NK // NA_W

    def kspec(t):
        return pl.BlockSpec((1, NA_TQ, NA_W), lambda m, b: (b, jnp.clip(m - 1, 0, kmax) + t, kcol))

    def vspec(t):
        return pl.BlockSpec((1, VS, NA_TQ), lambda m, b: (b, 0, jnp.clip(m - 1, 0, kmax) + t))

    return pl.pallas_call(
        _na_kernel,
        out_shape=jax.ShapeDtypeStruct((B, N, NA_W), BF16),
        grid=(nblk, B),
        in_specs=[pl.BlockSpec((1, NA_TQ, NA_W), lambda m, b: (b, m, qcol)),
                  kspec(0), kspec(1), kspec(2), vspec(0), vspec(1), vspec(2),
                  pl.BlockSpec((1, Nc, NA_W), lambda m, b: (b, 0, kcol)),
                  pl.BlockSpec((1, VS, Nc), lambda m, b: (0, 0, b)),
                  pl.BlockSpec((1, NA_HEADS, NA_TK, NA_TQ),
                               lambda m, b: (jnp.where(m == 0, 0, jnp.where(m == nblk - 1, 2, 1)), 0, 0, 0))],
        out_specs=pl.BlockSpec((1, NA_TQ, NA_W), lambda m, b: (b, m, 0)),
        compiler_params=_cparams(("arbitrary", "arbitrary"), 48),
        name="natten",
    )(ul, ul, ul, ul, vt, vt, vt, uc, vtc, bias)


def _place(w, layout, total):
    pieces, pos = [], 0
    for src, width, dst in sorted(layout, key=lambda item: item[2]):
        if dst > pos:
            pieces.append(jnp.zeros((w.shape[0], dst - pos), w.dtype))
        pieces.append(w[:, src:src + width])
        pos = dst + width
    if pos < total:
        pieces.append(jnp.zeros((w.shape[0], total - pos), w.dtype))
    return jnp.concatenate(pieces, axis=1)


def _ev_in_weight(w):
    lay = [(0, 256, EV_UQ), (256, 128, EV_UKV), (384, MLA_ROPE, EV_UKR + MLA_NOPE),
           (416, LRU_W, EV_UX), (416 + LRU_W, LRU_W, EV_UG)]
    return _place(w, lay, EV_PAD).astype(BF16)


def _od_in_weight(w):
    src = np.cumsum([0, 256, 256, 512, 512, 2 * GLA_LR, 512, 512, 512])
    dst = [OD_GQ, OD_GK, OD_GV, OD_GG, OD_GLR, OD_NQ, OD_NK, OD_NV]
    lay = [(int(src[i]), int(src[i + 1] - src[i]), dst[i]) for i in range(8)]
    return _place(w, lay, OD_PAD).astype(BF16)


def _pad_heads(w, heads, width):
    k = w.shape[0]
    return jnp.pad(w.reshape(k, heads, width), ((0, 0), (0, 0), (0, HEAD_SLOT - width))).reshape(k, heads * HEAD_SLOT)


def kernel(x, c, ctx, c_ctx, ada_w, ada_b, norm_mix, norm_mlp, w_out, mlp_w1, mlp_w2,
           ev_w_in, mla_q_norm, mla_w_uq, mla_kv_norm, mla_w_ukv, mla_q_gain, mla_k_gain,
           lru_conv_w, lru_conv_b, lru_w_a, lru_b_a, lru_w_x, lru_b_x, lru_lam,
           od_w_in, gla_w_a, gla_b_a, gla_o_gain, na_q_gain, na_k_gain, na_rpb):
    B, N, D = x.shape
    Nc = ctx.shape[1]
    depth = ada_w.shape[0]

    R = -(-(B + 1) // 8) * 8
    cc = jnp.concatenate([c, c_ctx[None], jnp.zeros((R - B - 1, D), c.dtype)], axis=0)
    mods = _modulation(cc, ada_w, ada_b).reshape(depth * R, 1, 6 * D)

    w1, w2, wo = mlp_w1.astype(BF16), mlp_w2.astype(BF16), w_out.astype(BF16)
    xl = x
    xc = ctx.reshape(1, B * Nc, D)
    for l in range(depth):
        last = l == depth - 1
        j = l // 2
        row_l, row_c = l * R, l * R + B
        if l % 2 == 0:
            w_in = _ev_in_weight(ev_w_in[j])
            wuq = _pad_heads(mla_w_uq[j], MLA_HEADS, MLA_QK)
            wukv = mla_w_ukv[j].reshape(MLA_KV_LORA, MLA_HEADS, MLA_NOPE + MLA_V)
            wuk = _pad_heads(wukv[:, :, :MLA_NOPE].reshape(MLA_KV_LORA, -1), MLA_HEADS, MLA_NOPE).astype(BF16)
            wuv = _pad_heads(wukv[:, :, MLA_NOPE:].reshape(MLA_KV_LORA, -1), MLA_HEADS, MLA_V).astype(BF16)
            proj = functools.partial(_ev_in_proj, g=norm_mix[l], mods=mods, w=w_in, qn=mla_q_norm[j].reshape(1, -1),
                                     kvn=mla_kv_norm[j].reshape(1, -1), wuq=wuq.astype(BF16),
                                     wuqr=_rope_partner_weight(wuq).astype(BF16), wuk=wuk, wuv=wuv)
            ul, ql, kl, vl = proj(xl, row0=row_l, tables=_rope_tables(N, mla_q_gain[j], mla_k_gain[j], True))
            uc, qc, kc, vc = proj(xc, row0=row_c, tables=_rope_tables(B * Nc, mla_q_gain[j], mla_k_gain[j], False))
            uc, qc, kc = (t.reshape(B, Nc, -1) for t in (uc, qc, kc))
            ya_l = _mla_attention(ql, kl, vl, kc, vc)
            ya_c = None if last else _mla_attention(qc, None, None, kc, vc)
            wg, bg = _lru_gate_weights(lru_w_a[j], lru_b_a[j], lru_w_x[j], lru_b_x[j])
            yb_l, yb_c = _rglru(ul, uc, lru_conv_w[j], lru_conv_b[j], wg, bg, lru_lam[j])
        else:
            w_in = _od_in_weight(od_w_in[j])
            na_gains = (jnp.tile(na_q_gain[j] * (NA_HD ** -0.5 * LOG2_E), NA_HEADS).reshape(1, NA_HEADS * NA_HD),
                        jnp.tile(na_k_gain[j], NA_HEADS).reshape(1, NA_HEADS * NA_HD))
            ul, nvt = _norm_mod_matmul(xl, norm_mix[l], mods, row_l, w_in, na_gains)
            uc, nvtc = _norm_mod_matmul(xc, norm_mix[l], mods, row_c, w_in, na_gains)
            uc = uc.reshape(B, Nc, OD_PAD)
            if not last:
                raise NotImplementedError("context outputs of the odd-layer mixers are only needed when depth > 2")
            wa = jnp.stack([jnp.pad(gla_w_a[j, d], ((d * GLA_LR, LANE - (d + 1) * GLA_LR), (0, 0))) for d in range(2)])
            ya_l = _gla(ul, uc, wa.astype(BF16), gla_b_a[j].reshape(2, 1, GLA_QK_W), gla_o_gain[j].reshape(1, GLA_DV))
            ya_c = None
            yb_l = _natten(ul, nvt, uc, nvtc, _na_bias_tables(na_rpb[j], N // GRID_W))
            yb_c = None
        xl = _out_proj_mlp(xl, ya_l, yb_l, wo, norm_mlp[l], mods, row_l, w1, w2, l)
        if not last:
            xc = _out_proj_mlp(xc, ya_c.reshape(1, B * Nc, -1), yb_c.reshape(1, B * Nc, -1), wo, norm_mlp[l], mods,
                               row_c, w1, w2, l)
    return xl
```

```python
import functools

import numpy as np
import jax
import jax.numpy as jnp
from jax import lax
from jax.experimental import pallas as pl
from jax.experimental.pallas import tpu as pltpu

F32 = jnp.float32
BF16 = jnp.bfloat16

D_MODEL = 1024
GRID_W = 64
EPS = 1e-6
LOG2_E = 1.4426950408889634
ROPE_BASE = 10000.0
D_FF = 4 * D_MODEL

MLA_HEADS = 8
MLA_NOPE = 64
MLA_ROPE = 32
MLA_QK = MLA_NOPE + MLA_ROPE
MLA_V = 64
MLA_Q_LORA = 256
MLA_KV_LORA = 128

LRU_W = 512
LRU_BLOCKS = 8
LRU_BS = LRU_W // LRU_BLOCKS
LRU_C = 8.0
CONV_W = 4
CONV_LEFT = 2

GLA_HEADS = 4
GLA_DK = 64
GLA_DV = 128
GLA_LR = 16
GLA_TAU = 16.0
GLA_CHUNK = 64

NA_HD = 64
NA_HEADS = 8
NA_WIN_ROWS = 8
NA_WIN_COLS = 16

LANE = 128
HEAD_SLOT = 128

EV_UQ, EV_UKV, EV_UKR, EV_UX, EV_UG = 0, 256, 384, 512, 1024
EV_PAD = 1536
OD_GQ, OD_GK, OD_GV, OD_GG, OD_NQ, OD_NK, OD_NV, OD_GLR = 0, 256, 512, 1024, 1536, 2048, 2560, 3072
OD_PAD = 3200


def _cparams(semantics, vmem_mib):
    return pltpu.CompilerParams(dimension_semantics=semantics, vmem_limit_bytes=vmem_mib << 20)


def _rms(x, g):
    return x * lax.rsqrt(jnp.mean(x * x, axis=-1, keepdims=True) + EPS) * g


def _ada_kernel(c_ref, w_ref, b_ref, o_ref):
    cv = c_ref[...]
    s = cv * jax.nn.sigmoid(cv)
    o_ref[0] = jnp.dot(s.astype(BF16), w_ref[0].astype(BF16), preferred_element_type=F32) + b_ref[0]


def _modulation(cc, ada_w, ada_b):
    L, D, D6 = ada_w.shape
    R = cc.shape[0]
    tn = 1536
    return pl.pallas_call(
        _ada_kernel,
        out_shape=jax.ShapeDtypeStruct((L, R, D6), F32),
        grid=(L, D6 // tn),
        in_specs=[pl.BlockSpec((R, D), lambda l, j: (0, 0)),
                  pl.BlockSpec((1, D, tn), lambda l, j: (l, 0, j)),
                  pl.BlockSpec((1, 1, tn), lambda l, j: (l, 0, j))],
        out_specs=pl.BlockSpec((1, R, tn), lambda l, j: (l, 0, j)),
        compiler_params=_cparams(("arbitrary", "arbitrary"), 40),
        name="adaln_modulation",
    )(cc, ada_w, ada_b.reshape(L, 1, D6))


def _mod_spec(row0, k):
    return pl.BlockSpec((1, 1, D_MODEL), lambda g, i: (row0 + g, 0, k))


def _modulated_proj(x_ref, g_ref, sh_ref, sc_ref, w_ref):
    h = _rms(x_ref[0], g_ref[...]) * (1.0 + sc_ref[0]) + sh_ref[0]
    return jnp.dot(h.astype(BF16), w_ref[...], preferred_element_type=F32)


def _nmm_kernel(x_ref, g_ref, sh_ref, sc_ref, w_ref, o_ref):
    o_ref[0] = _modulated_proj(x_ref, g_ref, sh_ref, sc_ref, w_ref).astype(BF16)


def _nmm_na_kernel(x_ref, g_ref, sh_ref, sc_ref, w_ref, qg_ref, kg_ref, o_ref, vt_ref):
    u = _modulated_proj(x_ref, g_ref, sh_ref, sc_ref, w_ref)
    o_ref[0, :, :OD_NQ] = u[:, :OD_NQ].astype(BF16)
    o_ref[0, :, OD_NV:] = u[:, OD_NV:].astype(BF16)
    lane_lo = lax.broadcasted_iota(jnp.int32, (u.shape[0], LANE), 1) < NA_HD
    for base, gain_ref in ((OD_NQ, qg_ref), (OD_NK, kg_ref)):
        for p in range(NA_HEADS // 2):
            t = u[:, base + p * LANE:base + (p + 1) * LANE]
            sq = t * t
            s_lo = jnp.sum(jnp.where(lane_lo, sq, 0.0), axis=-1, keepdims=True)
            s_hi = jnp.sum(jnp.where(lane_lo, 0.0, sq), axis=-1, keepdims=True)
            ms = jnp.where(lane_lo, s_lo, s_hi) * (1.0 / NA_HD)
            gain = gain_ref[:, p * LANE:(p + 1) * LANE]
            o_ref[0, :, base + p * LANE:base + (p + 1) * LANE] = (t * lax.rsqrt(ms + EPS) * gain).astype(BF16)
    for p in range(NA_HEADS // 2):
        pair = u[:, OD_NV + p * LANE:OD_NV + (p + 1) * LANE]
        vt_ref[0, (2 * p) * HEAD_SLOT:(2 * p + 1) * HEAD_SLOT, :] = _value_slot_t(pair, NA_HD)
        vt_ref[0, (2 * p + 1) * HEAD_SLOT:(2 * p + 2) * HEAD_SLOT, :] = _value_slot_t(pltpu.roll(pair, NA_HD, 1), NA_HD)


def _norm_mod_matmul(x, g, mods, row0, w, na_gains=None, tm=512):
    G, M, D = x.shape
    assert M % tm == 0
    Nout = w.shape[1]
    in_specs = [pl.BlockSpec((1, tm, D), lambda b, i: (b, i, 0)),
                pl.BlockSpec((1, D), lambda b, i: (0, 0)),
                _mod_spec(row0, 0), _mod_spec(row0, 1),
                pl.BlockSpec((D, Nout), lambda b, i: (0, 0))]
    u_shape = jax.ShapeDtypeStruct((G, M, Nout), BF16)
    u_spec = pl.BlockSpec((1, tm, Nout), lambda b, i: (b, i, 0))
    if na_gains is None:
        kern, args, out_shape, out_specs = _nmm_kernel, (), u_shape, u_spec
    else:
        VS = NA_HEADS * HEAD_SLOT
        kern, args = _nmm_na_kernel, tuple(na_gains)
        in_specs += [pl.BlockSpec((1, NA_HEADS * NA_HD), lambda b, i: (0, 0))] * 2
        out_shape = (u_shape, jax.ShapeDtypeStruct((G, VS, M), BF16))
        out_specs = (u_spec, pl.BlockSpec((1, VS, tm), lambda b, i: (b, 0, i)))
    return pl.pallas_call(
        kern,
        out_shape=out_shape,
        grid=(G, M // tm),
        in_specs=in_specs,
        out_specs=out_specs,
        compiler_params=_cparams(("parallel", "arbitrary"), 48),
        name="norm_mod_in_proj",
    )(x, g.reshape(1, D), mods, mods, w, *args)


MLP_TF = 1024


def _out_mlp_kernel(x_ref, ya_ref, yb_ref, woa_ref, wob_ref, g_ref, m2_ref, m3_ref, m4_ref, m5_ref,
                    w1_ref, w2_ref, o_ref, h_ref, a_ref):
    y = (jnp.dot(ya_ref[0], woa_ref[...], preferred_element_type=F32)
         + jnp.dot(yb_ref[0], wob_ref[...], preferred_element_type=F32))
    x1 = x_ref[0] + m2_ref[0] * y
    o_ref[0] = x1
    h_ref[...] = (_rms(x1, g_ref[...]) * (1.0 + m4_ref[0]) + m3_ref[0]).astype(BF16)
    for f in range(a_ref.shape[1] // MLP_TF):
        cols = slice(f * MLP_TF, (f + 1) * MLP_TF)
        a = jnp.maximum(jnp.dot(h_ref[...], w1_ref[:, cols], preferred_element_type=F32), 0.0)
        a_ref[:, cols] = (a * a).astype(BF16)
    o_ref[0] += m5_ref[0] * jnp.dot(a_ref[...], w2_ref[...], preferred_element_type=F32)


def _out_proj_mlp(x, ya, yb, w_out, g_mlp, mods, row0, w1, w2, layer, tm=512):
    G, M, D = x.shape
    assert M % tm == 0
    Wa, Wb = ya.shape[-1], yb.shape[-1]
    assert Wa == Wb and Wa + Wb == w_out.shape[1]
    FF = w1.shape[2]

    def resident(shape, row_block=0):
        return pl.BlockSpec((None,) + shape, lambda b, i: (layer, row_block, 0), pipeline_mode=pl.Buffered(1))

    return pl.pallas_call(
        _out_mlp_kernel,
        out_shape=jax.ShapeDtypeStruct((G, M, D), F32),
        grid=(G, M // tm),
        in_specs=[pl.BlockSpec((1, tm, D), lambda b, i: (b, i, 0)),
                  pl.BlockSpec((1, tm, Wa), lambda b, i: (b, i, 0)),
                  pl.BlockSpec((1, tm, Wb), lambda b, i: (b, i, 0)),
                  resident((Wa, D), 0), resident((Wb, D), 1),
                  pl.BlockSpec((1, D), lambda b, i: (0, 0)),
                  _mod_spec(row0, 2), _mod_spec(row0, 3), _mod_spec(row0, 4), _mod_spec(row0, 5),
                  resident((D, FF)), resident((FF, D))],
        out_specs=pl.BlockSpec((1, tm, D), lambda b, i: (b, i, 0)),
        scratch_shapes=[pltpu.VMEM((tm, D), BF16), pltpu.VMEM((tm, FF), BF16)],
        compiler_params=_cparams(("parallel", "arbitrary"), 52),
        name="out_proj_mlp",
    )(x, ya, yb, w_out, w_out, g_mlp.reshape(1, D), mods, mods, mods, mods, w1, w2)


def _nmm_mla_kernel(x_ref, g_ref, sh_ref, sc_ref, w_ref, tq1_ref, tq2_ref, tk1_ref, tka_ref, tkb_ref, qn_ref, kvn_ref,
                    wuq_ref, wuqr_ref, wuk_ref, wuv_ref, o_ref, q_out, k_out, vt_out):
    u = _modulated_proj(x_ref, g_ref, sh_ref, sc_ref, w_ref)
    o_ref[0] = u[:, EV_UX:].astype(BF16)
    qn = _rms(u[:, EV_UQ:EV_UQ + MLA_Q_LORA], qn_ref[...]).astype(BF16)
    kvn = _rms(u[:, EV_UKV:EV_UKV + MLA_KV_LORA], kvn_ref[...]).astype(BF16)
    q_all = jnp.dot(qn, wuq_ref[...], preferred_element_type=F32)
    q_rot = jnp.dot(qn, wuqr_ref[...], preferred_element_type=F32)
    k_all = jnp.dot(kvn, wuk_ref[...], preferred_element_type=F32)
    v_all = jnp.dot(kvn, wuv_ref[...], preferred_element_type=F32)
    ukr = u[:, EV_UKR:EV_UKR + HEAD_SLOT]
    quarter = MLA_ROPE // 4
    k_rot = (pltpu.roll(ukr, HEAD_SLOT - quarter, 1) * tka_ref[...] + pltpu.roll(ukr, quarter, 1) * tkb_ref[...])
    ukr_sq = jnp.sum(ukr * ukr, axis=-1, keepdims=True)
    inv_n = 1.0 / MLA_QK
    for h in range(MLA_HEADS):
        sl = slice(h * HEAD_SLOT, (h + 1) * HEAD_SLOT)
        qh = q_all[:, sl]
        rq = lax.rsqrt(jnp.sum(qh * qh, axis=-1, keepdims=True) * inv_n + EPS)
        q_out[0, :, sl] = (rq * (qh * tq1_ref[...] + q_rot[:, sl] * tq2_ref[...])).astype(BF16)
        kn = k_all[:, sl]
        rk = lax.rsqrt((jnp.sum(kn * kn, axis=-1, keepdims=True) + ukr_sq) * inv_n + EPS)
        k_out[0, :, sl] = (rk * ((kn + ukr) * tk1_ref[...] + k_rot)).astype(BF16)
        vt_out[0, sl, :] = _value_slot_t(v_all[:, sl], MLA_V)


def _rope_partner(t):
    quarter = MLA_ROPE // 4
    t4 = t.reshape(t.shape[:-1] + (2, 2, quarter))
    return jnp.stack([t4[..., 1, :], t4[..., 0, :]], axis=-2).reshape(t.shape)


def _rope_tables(n, q_gain, k_gain, use_rope):
    quarter = MLA_ROPE // 4
    if use_rope:
        pos = jnp.arange(n)
        inv = ROPE_BASE ** (-jnp.arange(0, MLA_ROPE // 2, 2, dtype=F32) / (MLA_ROPE // 2))
        ang_r = (pos // GRID_W).astype(F32)[:, None] * inv[None, :]
        ang_c = (pos % GRID_W).astype(F32)[:, None] * inv[None, :]
        ang = jnp.concatenate([ang_r, ang_r, ang_c, ang_c], axis=-1)
        cos, sin = jnp.cos(ang), jnp.sin(ang)
    else:
        cos, sin = jnp.ones((n, MLA_ROPE), F32), jnp.zeros((n, MLA_ROPE), F32)
    first = (np.arange(MLA_ROPE) % (2 * quarter)) < quarter
    pad = jnp.zeros((n, HEAD_SLOT - MLA_QK), F32)

    def slot(nope, rope):
        return jnp.concatenate([jnp.broadcast_to(nope, (n, MLA_NOPE)), rope, pad], axis=-1)

    zero = jnp.zeros((MLA_NOPE,), F32)
    qg, kg = q_gain * (MLA_QK ** -0.5 * LOG2_E), k_gain
    tq1 = slot(qg[:MLA_NOPE], cos * qg[MLA_NOPE:])
    tq2 = slot(zero, sin * _rope_partner(qg[MLA_NOPE:]))
    tk1 = slot(kg[:MLA_NOPE], cos * kg[MLA_NOPE:])
    ksin = sin * _rope_partner(kg[MLA_NOPE:])
    tka = slot(zero, jnp.where(first, -ksin, 0.0))
    tkb = slot(zero, jnp.where(first, 0.0, ksin))
    return tq1, tq2, tk1, tka, tkb


def _rope_partner_weight(wuq):
    k = wuq.shape[0]
    quarter = MLA_ROPE // 4
    w = wuq.reshape(k, MLA_HEADS, HEAD_SLOT)
    rope = w[:, :, MLA_NOPE:MLA_QK].reshape(k, MLA_HEADS, 2, 2, quarter)
    rot = jnp.stack([-rope[:, :, :, 1, :], rope[:, :, :, 0, :]], axis=3).reshape(k, MLA_HEADS, MLA_ROPE)
    out = jnp.concatenate([jnp.zeros_like(w[:, :, :MLA_NOPE]), rot, jnp.zeros_like(w[:, :, MLA_QK:])], axis=-1)
    return out.reshape(k, MLA_HEADS * HEAD_SLOT)


def _ev_in_proj(x, g, mods, row0, w, tables, qn, kvn, wuq, wuqr, wuk, wuv, tm=512):
    G, M, D = x.shape
    assert M % tm == 0
    HS = MLA_HEADS * HEAD_SLOT
    full = lambda shape: pl.BlockSpec(shape, lambda b, i: (0,) * len(shape))
    tab = pl.BlockSpec((tm, HEAD_SLOT), lambda b, i: (i, 0))
    rows = lambda width: pl.BlockSpec((1, tm, width), lambda b, i: (b, i, 0))
    return pl.pallas_call(
        _nmm_mla_kernel,
        out_shape=(jax.ShapeDtypeStruct((G, M, 2 * LRU_W), BF16), jax.ShapeDtypeStruct((G, M, HS), BF16),
                   jax.ShapeDtypeStruct((G, M, HS), BF16), jax.ShapeDtypeStruct((G, HS, M), BF16)),
        grid=(G, M // tm),
        in_specs=[rows(D), full((1, D)), _mod_spec(row0, 0), _mod_spec(row0, 1), full((D, EV_PAD)),
                  tab, tab, tab, tab, tab,
                  full((1, MLA_Q_LORA)), full((1, MLA_KV_LORA)),
                  full((MLA_Q_LORA, HS)), full((MLA_Q_LORA, HS)), full((MLA_KV_LORA, HS)), full((MLA_KV_LORA, HS))],
        out_specs=(rows(2 * LRU_W), rows(HS), rows(HS), pl.BlockSpec((1, HS, tm), lambda b, i: (b, 0, i))),
        compiler_params=_cparams(("parallel", "arbitrary"), 48),
        name="ev_in_proj_mla_prep",
    )(x, g.reshape(1, D), mods, mods, w, *tables, qn, kvn, wuq, wuqr, wuk, wuv)


def _dot_nt(a, b):
    return lax.dot_general(a, b, (((1,), (1,)), ((), ())), preferred_element_type=F32)


def _value_slot_t(v, ones_row):
    lane = lax.broadcasted_iota(jnp.int32, v.shape, 1)
    slot = jnp.where(lane < ones_row, v, jnp.where(lane == ones_row, 1.0, 0.0))
    return jnp.transpose(slot).astype(BF16)


ATTN_GROUP = 8


def _col_max(a, rows=8):
    parts = [a[r:r + rows] for r in range(0, a.shape[0], rows)]
    while len(parts) > 1:
        parts = [jnp.maximum(parts[i], parts[i + 1]) if i + 1 < len(parts) else parts[i]
                 for i in range(0, len(parts), 2)]
    return jnp.max(parts[0], axis=0, keepdims=True)


def _attend_t(heads):
    scores = []
    for q, keys, _, biases in heads:
        s = [_dot_nt(k, q) for k in keys]
        if biases is not None:
            s = [a if b is None else a + b for a, b in zip(s, biases)]
        scores.append(s)
    maxes = [functools.reduce(jnp.maximum, [_col_max(a) for a in s]) for s in scores]
    outs = []
    for (_, _, values_t, _), s, m in zip(heads, scores, maxes):
        out_t = None
        for a, vt in zip(s, values_t):
            part = jnp.dot(vt, jnp.exp2(a - m).astype(BF16), preferred_element_type=F32)
            out_t = part if out_t is None else out_t + part
        outs.append(out_t)
    return outs


def _pair_output(slots_t, dv):
    halves = [t[:dv] / t[dv:dv + 1] for t in slots_t]
    return jnp.transpose(jnp.concatenate(halves, axis=0))


def _mla_attn_kernel(*refs, with_latent):
    if with_latent:
        q_ref, kl_ref, vtl_ref, kc_ref, vtc_ref, o_ref = refs
    else:
        q_ref, kc_ref, vtc_ref, o_ref = refs
    for h0 in range(0, MLA_HEADS, ATTN_GROUP):
        heads = []
        for h in range(h0, h0 + ATTN_GROUP):
            sl = slice(h * HEAD_SLOT, (h + 1) * HEAD_SLOT)
            keys, values_t = [kc_ref[0, :, sl]], [vtc_ref[0, sl, :]]
            if with_latent:
                keys.append(kl_ref[0, :, sl])
                values_t.append(vtl_ref[0, sl, :])
            heads.append((q_ref[0, :, sl], keys, values_t, None))
        outs = _attend_t(heads)
        for i in range(0, ATTN_GROUP, 2):
            hp = (h0 + i) // 2
            o_ref[0, :, hp * LANE:(hp + 1) * LANE] = _pair_output(outs[i:i + 2], MLA_V).astype(BF16)


def _mla_attention(q, kl, vtl, kc, vtc, tq=512):
    B, M, HS = q.shape
    tq = min(tq, M)
    assert M % tq == 0
    HV = MLA_HEADS * MLA_V
    Nc = kc.shape[1]
    with_latent = kl is not None
    whole = lambda n, w: pl.BlockSpec((1, n, w), lambda b, i: (b, 0, 0))
    in_specs = [pl.BlockSpec((1, tq, HS), lambda b, i: (b, i, 0))]
    args = [q]
    if with_latent:
        in_specs += [whole(kl.shape[1], HS), whole(HS, kl.shape[1])]
        args += [kl, vtl]
    in_specs += [whole(Nc, HS), pl.BlockSpec((1, HS, Nc), lambda b, i: (0, 0, b))]
    args += [kc, vtc]
    return pl.pallas_call(
        functools.partial(_mla_attn_kernel, with_latent=with_latent),
        out_shape=jax.ShapeDtypeStruct((B, M, HV), BF16),
        grid=(B, M // tq),
        in_specs=in_specs,
        out_specs=pl.BlockSpec((1, tq, HV), lambda b, i: (b, i, 0)),
        compiler_params=_cparams(("parallel", "arbitrary"), 60),
        name="mla_attention",
    )(*args)


LRU_CW = 256
LRU_HALO = 16
LRU_TN = 256


def _gelu_tanh(x):
    return 0.5 * x * (1.0 + jnp.tanh(0.7978845608028654 * (x + 0.044715 * (x * x * x))))


def _scan_group(a, bv, h, reverse):
    row = lax.broadcasted_iota(jnp.int32, a.shape, 0)
    for s in (1, 2, 4):
        if reverse:
            keep = row < 8 - s
            shift = 8 - s
        else:
            keep = row >= s
            shift = s
        a_s = jnp.where(keep, pltpu.roll(a, shift, 0), 1.0)
        b_s = jnp.where(keep, pltpu.roll(bv, shift, 0), 0.0)
        bv = a * b_s + bv
        a = a * a_s
    hs = a * h + bv
    return hs, (hs[0:1, :] if reverse else hs[7:8, :])


def _lru_kernel(uxl_ref, ugl_ref, uxc_ref, ugc_ref, cw_ref, cb_ref, wg_ref, bg_ref, lam_ref, yl_ref, yc_ref,
                xpl_ref, xpc_ref, xcv_ref, af_ref, bf_ref, ab_ref, bb_ref):
    N, Nc = uxl_ref.shape[1], uxc_ref.shape[1]
    NT = N + Nc
    C = LRU_CW
    H = LRU_HALO

    def conv(src_ref, pad_ref, n, row0):
        pad_ref[0:H, :] = jnp.zeros((H, C), F32)
        pad_ref[H + n:H + n + H, :] = jnp.zeros((H, C), F32)
        pad_ref[H:H + n, :] = src_ref[0].astype(F32)
        y = cb_ref[...] + pad_ref[H - CONV_LEFT:H - CONV_LEFT + n, :] * cw_ref[0:1, :]
        for j in range(1, CONV_W):
            y = y + pad_ref[H - CONV_LEFT + j:H - CONV_LEFT + j + n, :] * cw_ref[j:j + 1, :]
        xcv_ref[row0:row0 + n, :] = y

    conv(uxc_ref, xpc_ref, Nc, 0)
    conv(uxl_ref, xpl_ref, N, Nc)

    lam = lam_ref[...]
    c_half = (-0.5 * LRU_C) * (jnp.maximum(-lam, 0.0) + jnp.log1p(jnp.exp(-jnp.abs(lam))))

    def coeff_chunk(i, carry):
        r0 = pl.multiple_of(i * LRU_TN, LRU_TN)
        x = xcv_ref[pl.ds(r0, LRU_TN), :]
        t = jnp.tanh(jnp.dot(x.astype(BF16), wg_ref[0], preferred_element_type=F32) + bg_ref[0])
        hx = 0.5 * x
        for d, (a_ref, b_ref) in enumerate(((af_ref, bf_ref), (ab_ref, bb_ref))):
            c = c_half[d:d + 1, :]
            a = jnp.exp(c * t[:, (2 * d) * C:(2 * d + 1) * C] + c)
            a_ref[pl.ds(r0, LRU_TN), :] = a
            gated_x = hx * t[:, (2 * d + 1) * C:(2 * d + 2) * C] + hx
            b_ref[pl.ds(r0, LRU_TN), :] = jnp.sqrt(1.0 - a * a) * gated_x
        return carry

    lax.fori_loop(0, NT // LRU_TN, coeff_chunk, 0)

    ngc, ngt = Nc // 8, NT // 8

    def scan_step(i, carry):
        hf, hb = carry
        rf = pl.multiple_of(i * 8, 8)
        rb = pl.multiple_of(jnp.where(i < ngc, ngc - 1 - i, ngt + ngc - 1 - i) * 8, 8)
        hs_f, hf = _scan_group(af_ref[pl.ds(rf, 8), :], bf_ref[pl.ds(rf, 8), :], hf, False)
        hs_b, hb = _scan_group(ab_ref[pl.ds(rb, 8), :], bb_ref[pl.ds(rb, 8), :], hb, True)
        bf_ref[pl.ds(rf, 8), :] = hs_f
        bb_ref[pl.ds(rb, 8), :] = hs_b
        return hf, hb

    zero = jnp.zeros((1, C), F32)
    lax.fori_loop(0, ngt, scan_step, (zero, zero), unroll=4)

    def out_chunk(i, carry):
        r0 = pl.multiple_of(i * LRU_TN, LRU_TN)
        hsum = bf_ref[pl.ds(Nc + r0, LRU_TN), :] + bb_ref[pl.ds(Nc + r0, LRU_TN), :]
        gate = _gelu_tanh(ugl_ref[0, pl.ds(r0, LRU_TN), :].astype(F32))
        yl_ref[0, pl.ds(r0, LRU_TN), :] = (hsum * gate).astype(BF16)
        return carry

    lax.fori_loop(0, N // LRU_TN, out_chunk, 0)
    yc_ref[0] = ((bf_ref[0:Nc, :] + bb_ref[0:Nc, :]) * _gelu_tanh(ugc_ref[0].astype(F32))).astype(BF16)


def _rglru(ul, uc, conv_w, conv_b, wg, bg, lam):
    B, N, _ = ul.shape
    Nc = uc.shape[1]
    C = LRU_CW
    nh = LRU_W // C
    NT = N + Nc
    col = lambda base: (lambda b, j: (b, 0, base // C + j))
    par = lambda rows: pl.BlockSpec((rows, C), lambda b, j: (0, j))
    return pl.pallas_call(
        _lru_kernel,
        out_shape=(jax.ShapeDtypeStruct((B, N, LRU_W), BF16), jax.ShapeDtypeStruct((B, Nc, LRU_W), BF16)),
        grid=(B, nh),
        in_specs=[pl.BlockSpec((1, N, C), col(0)), pl.BlockSpec((1, N, C), col(LRU_W)),
                  pl.BlockSpec((1, Nc, C), col(0)), pl.BlockSpec((1, Nc, C), col(LRU_W)),
                  par(CONV_W), par(1),
                  pl.BlockSpec((1, C, 4 * C), lambda b, j: (j, 0, 0)),
                  pl.BlockSpec((1, 1, 4 * C), lambda b, j: (j, 0, 0)),
                  par(2)],
        out_specs=(pl.BlockSpec((1, N, C), lambda b, j: (b, 0, j)),
                   pl.BlockSpec((1, Nc, C), lambda b, j: (b, 0, j))),
        scratch_shapes=[pltpu.VMEM((N + 2 * LRU_HALO, C), F32), pltpu.VMEM((Nc + 2 * LRU_HALO, C), F32),
                        pltpu.VMEM((NT, C), F32)] + [pltpu.VMEM((NT, C), F32)] * 4,
        compiler_params=_cparams(("parallel", "arbitrary"), 48),
        name="rglru",
    )(ul, ul, uc, uc, conv_w, conv_b.reshape(1, LRU_W), wg, bg, lam)


def _lru_gate_weights(w_a, b_a, w_x, b_x):
    C = LRU_CW
    nh = LRU_W // C
    kb = C // LRU_BS

    def dense(w):
        w = w.reshape(nh, kb, LRU_BS, LRU_BS)
        eye = jnp.eye(kb, dtype=w.dtype)
        return jnp.einsum('hkij,kl->hkilj', w, eye).reshape(nh, C, C)

    wg = jnp.concatenate([dense(w_a[0]), dense(w_x[0]), dense(w_a[1]), dense(w_x[1])], axis=-1)
    bg = jnp.stack([b_a[0], b_x[0], b_a[1], b_x[1]], axis=0).reshape(4, nh, C)
    bg = jnp.transpose(bg, (1, 0, 2)).reshape(nh, 1, 4 * C)
    return (0.5 * wg).astype(BF16), 0.5 * bg


GLA_SC = 256
GLA_UNROLL = 2
GLA_QK_W = GLA_HEADS * GLA_DK
GLA_V_W = GLA_HEADS * GLA_DV


def _split3(x):
    hi = x.astype(BF16)
    r1 = x - hi.astype(F32)
    mid = r1.astype(BF16)
    lo = (r1 - mid.astype(F32)).astype(BF16)
    return hi, mid, lo


def _gla_kernel(ql_ref, kl_ref, vl_ref, gl_ref, lrl_ref, kc_ref, vc_ref, lrc_ref, wa_ref, ba_ref, og_ref,
                y_ref, oacc_ref, st_ref):
    N, Nc = ql_ref.shape[1], kc_ref.shape[1]
    T = GLA_SC
    CH = GLA_CHUNK
    npair = GLA_HEADS // 2
    row = lax.broadcasted_iota(jnp.int32, (T, T), 0)
    colm = lax.broadcasted_iota(jnp.int32, (T, T), 1)
    same_chunk = (row // CH) == (colm // CH)
    causal = (same_chunk & (colm <= row), same_chunk & (colm >= row))
    tri = tuple(jnp.where(c, 1.0, 0.0).astype(BF16) for c in causal)
    lane_lo = lax.broadcasted_iota(jnp.int32, (T, LANE), 1) < GLA_DK
    srow = lax.broadcasted_iota(jnp.int32, (LANE, 2 * GLA_DV), 0)
    scol = lax.broadcasted_iota(jnp.int32, (LANE, 2 * GLA_DV), 1)
    state_mask = (srow < GLA_DK) == (scol < GLA_DV)
    qscale = GLA_DK ** -0.5

    def superchunks(streams, q_ref, k_ref, v_ref, lr_ref, with_output):
        nch = T // CH
        work = []
        for d, r0 in streams:
            z = jnp.dot(lr_ref[0, pl.ds(r0, T), :], wa_ref[d], preferred_element_type=F32) + ba_ref[d]
            work.append(dict(d=d, r0=r0, split=_split3(jax.nn.log_sigmoid(z) * (1.0 / GLA_TAU))))
        for w in work:
            d = w['d']
            cum = sum(jnp.dot(tri[d], t, preferred_element_type=F32) for t in w['split'])
            last = (CH - 1) if d == 0 else 0
            tot = [cum[c * CH + last:c * CH + last + 1, :] for c in range(nch)]
            tot_rows = jnp.concatenate([jnp.broadcast_to(t, (CH, GLA_QK_W)) for t in tot], axis=0)
            k = k_ref[0, pl.ds(w['r0'], T), :].astype(F32)
            w.update(cum=cum, tot=tot, k=k, v=v_ref[0, pl.ds(w['r0'], T), :],
                     k_dec=(k * jnp.exp(tot_rows - cum)).astype(BF16))
        for w in work:
            ds, dec = {}, {}
            for c in range(nch):
                rs = slice(c * CH, (c + 1) * CH)
                for p in range(npair):
                    kd = w['k_dec'][rs, p * LANE:(p + 1) * LANE]
                    vv = w['v'][rs, p * 2 * GLA_DV:(p + 1) * 2 * GLA_DV]
                    kv = lax.dot_general(kd, vv, (((0,), (0,)), ((), ())), preferred_element_type=F32)
                    ds[c, p] = jnp.where(state_mask, kv, 0.0)
                    dec_row = jnp.exp(w['tot'][c][:, p * LANE:(p + 1) * LANE])
                    dec_col = jnp.transpose(jnp.broadcast_to(dec_row, (LANE, LANE)))
                    dec[c, p] = jnp.concatenate([dec_col, dec_col], axis=1)
            w.update(ds=ds, dec=dec)
        for w in work:
            d = w['d']
            order = range(nch) if d == 0 else range(nch - 1, -1, -1)
            s_in = {}
            for p in range(npair):
                s = st_ref[d, p]
                for c in order:
                    s_in[c, p] = s.astype(BF16)
                    s = w['dec'][c, p] * s + w['ds'][c, p]
                st_ref[d, p] = s
            w.update(s_in=s_in)
        if not with_output:
            return
        for w in work:
            q = q_ref[0, pl.ds(w['r0'], T), :].astype(F32)
            w.update(q_dec=((q * qscale) * jnp.exp(w['cum'])).astype(BF16),
                     k_inv=(w['k'] * jnp.exp(-w['cum'])).astype(BF16))
        for w in work:
            d, r0 = w['d'], w['r0']
            for p in range(npair):
                qp = w['q_dec'][:, p * LANE:(p + 1) * LANE]
                kp = w['k_inv'][:, p * LANE:(p + 1) * LANE]
                o_inter = jnp.concatenate(
                    [jnp.dot(qp[c * CH:(c + 1) * CH], w['s_in'][c, p], preferred_element_type=F32)
                     for c in range(nch)], axis=0)
                for j in range(2):
                    h = 2 * p + j
                    qm = jnp.where(lane_lo if j == 0 else jnp.logical_not(lane_lo), qp, jnp.zeros_like(qp))
                    att = jnp.where(causal[d], _dot_nt(qm, kp), 0.0).astype(BF16)
                    o = (jnp.dot(att, w['v'][:, h * GLA_DV:(h + 1) * GLA_DV], preferred_element_type=F32)
                         + o_inter[:, j * GLA_DV:(j + 1) * GLA_DV])
                    oacc_ref[d, pl.ds(r0, T), h * GLA_DV:(h + 1) * GLA_DV] = o

    st_ref[...] = jnp.zeros_like(st_ref)
    ncs, nls = Nc // T, N // T

    def ctx_body(i, carry):
        streams = [(0, pl.multiple_of(i * T, T)), (1, pl.multiple_of((ncs - 1 - i) * T, T))]
        superchunks(streams, None, kc_ref, vc_ref, lrc_ref, False)
        return carry

    def lat_body(i, carry):
        streams = []
        for u in range(GLA_UNROLL):
            streams += [(0, pl.multiple_of((GLA_UNROLL * i + u) * T, T)),
                        (1, pl.multiple_of((nls - 1 - GLA_UNROLL * i - u) * T, T))]
        superchunks(streams, ql_ref, kl_ref, vl_ref, lrl_ref, True)
        return carry

    lax.fori_loop(0, ncs, ctx_body, 0)
    lax.fori_loop(0, nls // GLA_UNROLL, lat_body, 0)

    def fin(i, carry):
        r0 = pl.multiple_of(i * T, T)
        g = gl_ref[0, pl.ds(r0, T), :].astype(F32)
        for h in range(GLA_HEADS):
            sl = slice(h * GLA_DV, (h + 1) * GLA_DV)
            gh = g[:, sl]
            o = oacc_ref[0, pl.ds(r0, T), sl] + oacc_ref[1, pl.ds(r0, T), sl]
            y_ref[0, pl.ds(r0, T), sl] = (_rms(o, og_ref[...]) * (gh * jax.nn.sigmoid(gh))).astype(BF16)
        return carry

    lax.fori_loop(0, N // T, fin, 0)


def _gla(ul, uc, wa, ba, o_gain):
    B, N, _ = ul.shape
    Nc = uc.shape[1]
    blk = lambda n, w, base: pl.BlockSpec((1, n, w), lambda b: (b, 0, base // w))
    full = lambda shape: pl.BlockSpec(shape, lambda b: (0,) * len(shape))
    return pl.pallas_call(
        _gla_kernel,
        out_shape=jax.ShapeDtypeStruct((B, N, GLA_V_W), BF16),
        grid=(B,),
        in_specs=[blk(N, GLA_QK_W, OD_GQ), blk(N, GLA_QK_W, OD_GK), blk(N, GLA_V_W, OD_GV),
                  blk(N, GLA_V_W, OD_GG), blk(N, LANE, OD_GLR),
                  blk(Nc, GLA_QK_W, OD_GK), blk(Nc, GLA_V_W, OD_GV), blk(Nc, LANE, OD_GLR),
                  full((2, LANE, GLA_QK_W)), full((2, 1, GLA_QK_W)), full((1, GLA_DV))],
        out_specs=pl.BlockSpec((1, N, GLA_V_W), lambda b: (b, 0, 0)),
        scratch_shapes=[pltpu.VMEM((2, N, GLA_V_W), F32),
                        pltpu.VMEM((2, GLA_HEADS // 2, LANE, 2 * GLA_DV), F32)],
        compiler_params=_cparams(("parallel",), 48),
        name="gla",
    )(ul, ul, ul, ul, ul, uc, uc, uc, wa, ba, o_gain)


NA_W = NA_HEADS * NA_HD
NA_QROWS = 4
NA_KROWS = 12
NA_TQ = NA_QROWS * GRID_W
NA_TK = NA_KROWS * GRID_W
NA_NEG = -1e30
NA_BATCH_BLOCK = 2


def _na_kernel(q_ref, k0_ref, k1_ref, k2_ref, vt0_ref, vt1_ref, vt2_ref, kc_ref, vtc_ref, bias_ref, o_ref):
    lane_lo = lax.broadcasted_iota(jnp.int32, (NA_TQ, LANE), 1) < NA_HD
    k_refs = (k0_ref, k1_ref, k2_ref, kc_ref)
    nc = kc_ref.shape[1]
    for bb in range(q_ref.shape[0]):
        for h0 in range(0, NA_HEADS, ATTN_GROUP):
            heads = []
            for h in range(h0, h0 + ATTN_GROUP):
                sl = slice((h // 2) * LANE, (h // 2 + 1) * LANE)
                hs = slice(h * HEAD_SLOT, (h + 1) * HEAD_SLOT)
                qp = q_ref[bb, :, sl]
                qm = jnp.where(lane_lo if h % 2 == 0 else jnp.logical_not(lane_lo), qp, jnp.zeros_like(qp))
                biases = [bias_ref[0, h, t * NA_TQ:(t + 1) * NA_TQ, :] for t in range(3)] + [None]
                values_t = [r[bb, hs, :] for r in (vt0_ref, vt1_ref, vt2_ref)] + [vtc_ref[0, hs, bb * nc:(bb + 1) * nc]]
                heads.append((qm, [r[bb, :, sl] for r in k_refs], values_t, biases))
            outs = _attend_t(heads)
            for i in range(0, ATTN_GROUP, 2):
                hp = (h0 + i) // 2
                o_ref[bb, :, hp * LANE:(hp + 1) * LANE] = _pair_output(outs[i:i + 2], NA_HD).astype(BF16)


def _na_bias_tables(rpb, rows):
    H, ndi, ndj = rpb.shape
    rpb = rpb * LOG2_E
    W = GRID_W
    half = NA_WIN_COLS - 1
    zeros = jnp.zeros((H, ndi, W - 1 - half), rpb.dtype)
    vec = jnp.concatenate([rpb[:, :, half:], zeros, zeros, rpb[:, :, :half]], axis=-1)
    toep = jnp.tile(vec, (1, 1, W))[:, :, :W * (2 * W - 2)].reshape(H, ndi, W, 2 * W - 2)[..., :W]
    qc = np.arange(W)[:, None]
    kc = np.arange(W)[None, :]
    c0 = np.clip(qc - NA_WIN_COLS // 2, 0, W - NA_WIN_COLS)
    toep = jnp.where(jnp.asarray((kc >= c0) & (kc < c0 + NA_WIN_COLS)), toep, NA_NEG)
    toep = jnp.swapaxes(toep, 2, 3)
    nblk = rows // NA_QROWS
    guard = NA_KROWS - NA_WIN_ROWS
    toep = jnp.pad(toep, ((0, 0), (guard, guard), (0, 0), (0, 0)), constant_values=NA_NEG)
    tabs = []
    for m in (0, 1, nblk - 1):
        kb = min(max(m - 1, 0), nblk - NA_KROWS // NA_QROWS)
        qr = m * NA_QROWS + np.arange(NA_QROWS)[None, :]
        kr = kb * NA_QROWS + np.arange(NA_KROWS)[:, None]
        r0 = np.clip(qr - NA_WIN_ROWS // 2, 0, rows - NA_WIN_ROWS)
        row_ok = (kr >= r0) & (kr < r0 + NA_WIN_ROWS)
        di0 = (kb - m) * NA_QROWS + NA_WIN_ROWS - 1 + guard
        per_a = [toep[:, di0 - a:di0 - a + NA_KROWS] for a in range(NA_QROWS)]
        ok = np.repeat(row_ok, W, axis=1)[None, :, None, :]
        tab = jnp.where(jnp.asarray(ok), jnp.concatenate(per_a, axis=-1), NA_NEG)
        tabs.append(tab.reshape(H, NA_TK, NA_TQ))
    return jnp.stack(tabs, axis=0)


def _natten(ul, vt, uc, vtc, bias):
    B, N, _ = ul.shape
    Nc = uc.shape[1]
    nblk = N // NA_TQ
    kmax = nblk - NA_KROWS // NA_QROWS
    VS = NA_HEADS * HEAD_SLOT
    qcol, kcol = OD_NQ // NA_W, OD_NK // NA_W

    bb = NA_BATCH_BLOCK
    assert B % bb == 0

    def kspec(t):
        return pl.BlockSpec((bb, NA_TQ, NA_W), lambda m, b: (b, jnp.clip(m - 1, 0, kmax) + t, kcol))

    def vspec(t):
        return pl.BlockSpec((bb, VS, NA_TQ), lambda m, b: (b, 0, jnp.clip(m - 1, 0, kmax) + t))

    return pl.pallas_call(
        _na_kernel,
        out_shape=jax.ShapeDtypeStruct((B, N, NA_W), BF16),
        grid=(nblk, B // bb),
        in_specs=[pl.BlockSpec((bb, NA_TQ, NA_W), lambda m, b: (b, m, qcol)),
                  kspec(0), kspec(1), kspec(2), vspec(0), vspec(1), vspec(2),
                  pl.BlockSpec((bb, Nc, NA_W), lambda m, b: (b, 0, kcol)),
                  pl.BlockSpec((1, VS, bb * Nc), lambda m, b: (0, 0, b)),
                  pl.BlockSpec((1, NA_HEADS, NA_TK, NA_TQ),
                               lambda m, b: (jnp.where(m == 0, 0, jnp.where(m == nblk - 1, 2, 1)), 0, 0, 0))],
        out_specs=pl.BlockSpec((bb, NA_TQ, NA_W), lambda m, b: (b, m, 0)),
        compiler_params=_cparams(("arbitrary", "arbitrary"), 48),
        name="natten",
    )(ul, ul, ul, ul, vt, vt, vt, uc, vtc, bias)


def _place(w, layout, total):
    pieces, pos = [], 0
    for src, width, dst in sorted(layout, key=lambda item: item[2]):
        if dst > pos:
            pieces.append(jnp.zeros((w.shape[0], dst - pos), w.dtype))
        pieces.append(w[:, src:src + width])
        pos = dst + width
    if pos < total:
        pieces.append(jnp.zeros((w.shape[0], total - pos), w.dtype))
    return jnp.concatenate(pieces, axis=1)


def _ev_in_weight(w):
    lay = [(0, 256, EV_UQ), (256, 128, EV_UKV), (384, MLA_ROPE, EV_UKR + MLA_NOPE),
           (416, LRU_W, EV_UX), (416 + LRU_W, LRU_W, EV_UG)]
    return _place(w, lay, EV_PAD).astype(BF16)


def _od_in_weight(w):
    src = np.cumsum([0, 256, 256, 512, 512, 2 * GLA_LR, 512, 512, 512])
    dst = [OD_GQ, OD_GK, OD_GV, OD_GG, OD_GLR, OD_NQ, OD_NK, OD_NV]
    lay = [(int(src[i]), int(src[i + 1] - src[i]), dst[i]) for i in range(8)]
    return _place(w, lay, OD_PAD).astype(BF16)


def _pad_heads(w, heads, width):
    k = w.shape[0]
    return jnp.pad(w.reshape(k, heads, width), ((0, 0), (0, 0), (0, HEAD_SLOT - width))).reshape(k, heads * HEAD_SLOT)


def kernel(x, c, ctx, c_ctx, ada_w, ada_b, norm_mix, norm_mlp, w_out, mlp_w1, mlp_w2,
           ev_w_in, mla_q_norm, mla_w_uq, mla_kv_norm, mla_w_ukv, mla_q_gain, mla_k_gain,
           lru_conv_w, lru_conv_b, lru_w_a, lru_b_a, lru_w_x, lru_b_x, lru_lam,
           od_w_in, gla_w_a, gla_b_a, gla_o_gain, na_q_gain, na_k_gain, na_rpb):
    B, N, D = x.shape
    Nc = ctx.shape[1]
    depth = ada_w.shape[0]

    R = -(-(B + 1) // 8) * 8
    cc = jnp.concatenate([c, c_ctx[None], jnp.zeros((R - B - 1, D), c.dtype)], axis=0)
    mods = _modulation(cc, ada_w, ada_b).reshape(depth * R, 1, 6 * D)

    w1, w2, wo = mlp_w1.astype(BF16), mlp_w2.astype(BF16), w_out.astype(BF16)
    xl = x
    xc = ctx.reshape(1, B * Nc, D)
    for l in range(depth):
        last = l == depth - 1
        j = l // 2
        row_l, row_c = l * R, l * R + B
        if l % 2 == 0:
            w_in = _ev_in_weight(ev_w_in[j])
            wuq = _pad_heads(mla_w_uq[j], MLA_HEADS, MLA_QK)
            wukv = mla_w_ukv[j].reshape(MLA_KV_LORA, MLA_HEADS, MLA_NOPE + MLA_V)
            wuk = _pad_heads(wukv[:, :, :MLA_NOPE].reshape(MLA_KV_LORA, -1), MLA_HEADS, MLA_NOPE).astype(BF16)
            wuv = _pad_heads(wukv[:, :, MLA_NOPE:].reshape(MLA_KV_LORA, -1), MLA_HEADS, MLA_V).astype(BF16)
            proj = functools.partial(_ev_in_proj, g=norm_mix[l], mods=mods, w=w_in, qn=mla_q_norm[j].reshape(1, -1),
                                     kvn=mla_kv_norm[j].reshape(1, -1), wuq=wuq.astype(BF16),
                                     wuqr=_rope_partner_weight(wuq).astype(BF16), wuk=wuk, wuv=wuv)
            ul, ql, kl, vl = proj(xl, row0=row_l, tables=_rope_tables(N, mla_q_gain[j], mla_k_gain[j], True))
            uc, qc, kc, vc = proj(xc, row0=row_c, tables=_rope_tables(B * Nc, mla_q_gain[j], mla_k_gain[j], False))
            uc, qc, kc = (t.reshape(B, Nc, -1) for t in (uc, qc, kc))
            ya_l = _mla_attention(ql, kl, vl, kc, vc)
            ya_c = None if last else _mla_attention(qc, None, None, kc, vc)
            wg, bg = _lru_gate_weights(lru_w_a[j], lru_b_a[j], lru_w_x[j], lru_b_x[j])
            yb_l, yb_c = _rglru(ul, uc, lru_conv_w[j], lru_conv_b[j], wg, bg, lru_lam[j])
        else:
            w_in = _od_in_weight(od_w_in[j])
            na_gains = (jnp.tile(na_q_gain[j] * (NA_HD ** -0.5 * LOG2_E), NA_HEADS).reshape(1, NA_HEADS * NA_HD),
                        jnp.tile(na_k_gain[j], NA_HEADS).reshape(1, NA_HEADS * NA_HD))
            ul, nvt = _norm_mod_matmul(xl, norm_mix[l], mods, row_l, w_in, na_gains)
            uc, nvtc = _norm_mod_matmul(xc, norm_mix[l], mods, row_c, w_in, na_gains)
            uc = uc.reshape(B, Nc, OD_PAD)
            if not last:
                raise NotImplementedError("context outputs of the odd-layer mixers are only needed when depth > 2")
            wa = jnp.stack([jnp.pad(gla_w_a[j, d], ((d * GLA_LR, LANE - (d + 1) * GLA_LR), (0, 0))) for d in range(2)])
            ya_l = _gla(ul, uc, wa.astype(BF16), gla_b_a[j].reshape(2, 1, GLA_QK_W), gla_o_gain[j].reshape(1, GLA_DV))
            ya_c = None
            yb_l = _natten(ul, nvt, uc, nvtc, _na_bias_tables(na_rpb[j], N // GRID_W))
            yb_c = None
        xl = _out_proj_mlp(xl, ya_l, yb_l, wo, norm_mlp[l], mods, row_l, w1, w2, l)
        if not last:
            xc = _out_proj_mlp(xc, ya_c.reshape(1, B * Nc, -1), yb_c.reshape(1, B * Nc, -1), wo, norm_mlp[l], mods,
                               row_c, w1, w2, l)
    return xl
```

```python
import functools

import numpy as np
import jax
import jax.numpy as jnp
from jax import lax
from jax.experimental import pallas as pl
from jax.experimental.pallas import tpu as pltpu

F32 = jnp.float32
BF16 = jnp.bfloat16

D_MODEL = 1024
GRID_W = 64
EPS = 1e-6
LOG2_E = 1.4426950408889634
ROPE_BASE = 10000.0
D_FF = 4 * D_MODEL

MLA_HEADS = 8
MLA_NOPE = 64
MLA_ROPE = 32
MLA_QK = MLA_NOPE + MLA_ROPE
MLA_V = 64
MLA_Q_LORA = 256
MLA_KV_LORA = 128

LRU_W = 512
LRU_BLOCKS = 8
LRU_BS = LRU_W // LRU_BLOCKS
LRU_C = 8.0
CONV_W = 4
CONV_LEFT = 2

GLA_HEADS = 4
GLA_DK = 64
GLA_DV = 128
GLA_LR = 16
GLA_TAU = 16.0
GLA_CHUNK = 64

NA_HD = 64
NA_HEADS = 8
NA_WIN_ROWS = 8
NA_WIN_COLS = 16

LANE = 128
HEAD_SLOT = 128

EV_UQ, EV_UKV, EV_UKR, EV_UX, EV_UG = 0, 256, 384, 512, 1024
EV_PAD = 1536
OD_GQ, OD_GK, OD_GV, OD_GG, OD_NQ, OD_NK, OD_NV, OD_GLR = 0, 256, 512, 1024, 1536, 2048, 2560, 3072
OD_PAD = 3200


def _cparams(semantics, vmem_mib):
    return pltpu.CompilerParams(dimension_semantics=semantics, vmem_limit_bytes=vmem_mib << 20)


def _rms(x, g):
    return x * lax.rsqrt(jnp.mean(x * x, axis=-1, keepdims=True) + EPS) * g


def _ada_kernel(c_ref, w_ref, b_ref, o_ref):
    cv = c_ref[...]
    s = cv * jax.nn.sigmoid(cv)
    o_ref[0] = jnp.dot(s.astype(BF16), w_ref[0].astype(BF16), preferred_element_type=F32) + b_ref[0]


def _modulation(cc, ada_w, ada_b):
    L, D, D6 = ada_w.shape
    R = cc.shape[0]
    tn = 1536
    return pl.pallas_call(
        _ada_kernel,
        out_shape=jax.ShapeDtypeStruct((L, R, D6), F32),
        grid=(L, D6 // tn),
        in_specs=[pl.BlockSpec((R, D), lambda l, j: (0, 0)),
                  pl.BlockSpec((1, D, tn), lambda l, j: (l, 0, j)),
                  pl.BlockSpec((1, 1, tn), lambda l, j: (l, 0, j))],
        out_specs=pl.BlockSpec((1, R, tn), lambda l, j: (l, 0, j)),
        compiler_params=_cparams(("arbitrary", "arbitrary"), 40),
        name="adaln_modulation",
    )(cc, ada_w, ada_b.reshape(L, 1, D6))


def _mod_spec(row0, k):
    return pl.BlockSpec((1, 1, D_MODEL), lambda g, i: (row0 + g, 0, k))


def _modulated_proj(x_ref, g_ref, sh_ref, sc_ref, w_ref):
    h = _rms(x_ref[0], g_ref[...]) * (1.0 + sc_ref[0]) + sh_ref[0]
    return jnp.dot(h.astype(BF16), w_ref[...], preferred_element_type=F32)


def _nmm_kernel(x_ref, g_ref, sh_ref, sc_ref, w_ref, o_ref):
    o_ref[0] = _modulated_proj(x_ref, g_ref, sh_ref, sc_ref, w_ref).astype(BF16)


def _nmm_na_kernel(x_ref, g_ref, sh_ref, sc_ref, w_ref, qg_ref, kg_ref, o_ref, vt_ref):
    u = _modulated_proj(x_ref, g_ref, sh_ref, sc_ref, w_ref)
    o_ref[0, :, :OD_NQ] = u[:, :OD_NQ].astype(BF16)
    o_ref[0, :, OD_NV:] = u[:, OD_NV:].astype(BF16)
    lane_lo = lax.broadcasted_iota(jnp.int32, (u.shape[0], LANE), 1) < NA_HD
    for base, gain_ref in ((OD_NQ, qg_ref), (OD_NK, kg_ref)):
        for p in range(NA_HEADS // 2):
            t = u[:, base + p * LANE:base + (p + 1) * LANE]
            sq = t * t
            s_lo = jnp.sum(jnp.where(lane_lo, sq, 0.0), axis=-1, keepdims=True)
            s_hi = jnp.sum(jnp.where(lane_lo, 0.0, sq), axis=-1, keepdims=True)
            ms = jnp.where(lane_lo, s_lo, s_hi) * (1.0 / NA_HD)
            gain = gain_ref[:, p * LANE:(p + 1) * LANE]
            o_ref[0, :, base + p * LANE:base + (p + 1) * LANE] = (t * lax.rsqrt(ms + EPS) * gain).astype(BF16)
    for p in range(NA_HEADS // 2):
        pair = u[:, OD_NV + p * LANE:OD_NV + (p + 1) * LANE]
        vt_ref[0, (2 * p) * HEAD_SLOT:(2 * p + 1) * HEAD_SLOT, :] = _value_slot_t(pair, NA_HD)
        vt_ref[0, (2 * p + 1) * HEAD_SLOT:(2 * p + 2) * HEAD_SLOT, :] = _value_slot_t(pltpu.roll(pair, NA_HD, 1), NA_HD)


def _norm_mod_matmul(x, g, mods, row0, w, na_gains=None, tm=512):
    G, M, D = x.shape
    assert M % tm == 0
    Nout = w.shape[1]
    in_specs = [pl.BlockSpec((1, tm, D), lambda b, i: (b, i, 0)),
                pl.BlockSpec((1, D), lambda b, i: (0, 0)),
                _mod_spec(row0, 0), _mod_spec(row0, 1),
                pl.BlockSpec((D, Nout), lambda b, i: (0, 0))]
    u_shape = jax.ShapeDtypeStruct((G, M, Nout), BF16)
    u_spec = pl.BlockSpec((1, tm, Nout), lambda b, i: (b, i, 0))
    if na_gains is None:
        kern, args, out_shape, out_specs = _nmm_kernel, (), u_shape, u_spec
    else:
        VS = NA_HEADS * HEAD_SLOT
        kern, args = _nmm_na_kernel, tuple(na_gains)
        in_specs += [pl.BlockSpec((1, NA_HEADS * NA_HD), lambda b, i: (0, 0))] * 2
        out_shape = (u_shape, jax.ShapeDtypeStruct((G, VS, M), BF16))
        out_specs = (u_spec, pl.BlockSpec((1, VS, tm), lambda b, i: (b, 0, i)))
    return pl.pallas_call(
        kern,
        out_shape=out_shape,
        grid=(G, M // tm),
        in_specs=in_specs,
        out_specs=out_specs,
        compiler_params=_cparams(("parallel", "arbitrary"), 48),
        name="norm_mod_in_proj",
    )(x, g.reshape(1, D), mods, mods, w, *args)


MLP_TF = 1024


def _out_mlp_kernel(x_ref, ya_ref, yb_ref, woa_ref, wob_ref, g_ref, m2_ref, m3_ref, m4_ref, m5_ref,
                    w1_ref, w2_ref, o_ref, h_ref, a_ref):
    y = (jnp.dot(ya_ref[0], woa_ref[...], preferred_element_type=F32)
         + jnp.dot(yb_ref[0], wob_ref[...], preferred_element_type=F32))
    x1 = x_ref[0] + m2_ref[0] * y
    o_ref[0] = x1
    h_ref[...] = (_rms(x1, g_ref[...]) * (1.0 + m4_ref[0]) + m3_ref[0]).astype(BF16)
    for f in range(a_ref.shape[1] // MLP_TF):
        cols = slice(f * MLP_TF, (f + 1) * MLP_TF)
        a = jnp.maximum(jnp.dot(h_ref[...], w1_ref[:, cols], preferred_element_type=F32), 0.0)
        a_ref[:, cols] = (a * a).astype(BF16)
    o_ref[0] += m5_ref[0] * jnp.dot(a_ref[...], w2_ref[...], preferred_element_type=F32)


def _out_proj_mlp(x, ya, yb, w_out, g_mlp, mods, row0, w1, w2, layer, tm=512):
    G, M, D = x.shape
    assert M % tm == 0
    Wa, Wb = ya.shape[-1], yb.shape[-1]
    assert Wa == Wb and Wa + Wb == w_out.shape[1]
    FF = w1.shape[2]

    def resident(shape, row_block=0):
        return pl.BlockSpec((None,) + shape, lambda b, i: (layer, row_block, 0), pipeline_mode=pl.Buffered(1))

    return pl.pallas_call(
        _out_mlp_kernel,
        out_shape=jax.ShapeDtypeStruct((G, M, D), F32),
        grid=(G, M // tm),
        in_specs=[pl.BlockSpec((1, tm, D), lambda b, i: (b, i, 0)),
                  pl.BlockSpec((1, tm, Wa), lambda b, i: (b, i, 0)),
                  pl.BlockSpec((1, tm, Wb), lambda b, i: (b, i, 0)),
                  resident((Wa, D), 0), resident((Wb, D), 1),
                  pl.BlockSpec((1, D), lambda b, i: (0, 0)),
                  _mod_spec(row0, 2), _mod_spec(row0, 3), _mod_spec(row0, 4), _mod_spec(row0, 5),
                  resident((D, FF)), resident((FF, D))],
        out_specs=pl.BlockSpec((1, tm, D), lambda b, i: (b, i, 0)),
        scratch_shapes=[pltpu.VMEM((tm, D), BF16), pltpu.VMEM((tm, FF), BF16)],
        compiler_params=_cparams(("parallel", "arbitrary"), 52),
        name="out_proj_mlp",
    )(x, ya, yb, w_out, w_out, g_mlp.reshape(1, D), mods, mods, mods, mods, w1, w2)


def _nmm_mla_kernel(x_ref, g_ref, sh_ref, sc_ref, w_ref, tq1_ref, tq2_ref, tk1_ref, tka_ref, tkb_ref, qn_ref, kvn_ref,
                    wuq_ref, wuqr_ref, wuk_ref, wuv_ref, o_ref, q_out, k_out, vt_out):
    u = _modulated_proj(x_ref, g_ref, sh_ref, sc_ref, w_ref)
    o_ref[0] = u[:, EV_UX:].astype(BF16)
    qn = _rms(u[:, EV_UQ:EV_UQ + MLA_Q_LORA], qn_ref[...]).astype(BF16)
    kvn = _rms(u[:, EV_UKV:EV_UKV + MLA_KV_LORA], kvn_ref[...]).astype(BF16)
    q_all = jnp.dot(qn, wuq_ref[...], preferred_element_type=F32)
    q_rot = jnp.dot(qn, wuqr_ref[...], preferred_element_type=F32)
    k_all = jnp.dot(kvn, wuk_ref[...], preferred_element_type=F32)
    v_all = jnp.dot(kvn, wuv_ref[...], preferred_element_type=F32)
    ukr = u[:, EV_UKR:EV_UKR + HEAD_SLOT]
    quarter = MLA_ROPE // 4
    k_rot = (pltpu.roll(ukr, HEAD_SLOT - quarter, 1) * tka_ref[...] + pltpu.roll(ukr, quarter, 1) * tkb_ref[...])
    ukr_sq = jnp.sum(ukr * ukr, axis=-1, keepdims=True)
    inv_n = 1.0 / MLA_QK
    for h in range(MLA_HEADS):
        sl = slice(h * HEAD_SLOT, (h + 1) * HEAD_SLOT)
        qh = q_all[:, sl]
        rq = lax.rsqrt(jnp.sum(qh * qh, axis=-1, keepdims=True) * inv_n + EPS)
        q_out[0, :, sl] = (rq * (qh * tq1_ref[...] + q_rot[:, sl] * tq2_ref[...])).astype(BF16)
        kn = k_all[:, sl]
        rk = lax.rsqrt((jnp.sum(kn * kn, axis=-1, keepdims=True) + ukr_sq) * inv_n + EPS)
        k_out[0, :, sl] = (rk * ((kn + ukr) * tk1_ref[...] + k_rot)).astype(BF16)
        vt_out[0, sl, :] = _value_slot_t(v_all[:, sl], MLA_V)


def _rope_partner(t):
    quarter = MLA_ROPE // 4
    t4 = t.reshape(t.shape[:-1] + (2, 2, quarter))
    return jnp.stack([t4[..., 1, :], t4[..., 0, :]], axis=-2).reshape(t.shape)


def _rope_tables(n, q_gain, k_gain, use_rope):
    quarter = MLA_ROPE // 4
    if use_rope:
        pos = jnp.arange(n)
        inv = ROPE_BASE ** (-jnp.arange(0, MLA_ROPE // 2, 2, dtype=F32) / (MLA_ROPE // 2))
        ang_r = (pos // GRID_W).astype(F32)[:, None] * inv[None, :]
        ang_c = (pos % GRID_W).astype(F32)[:, None] * inv[None, :]
        ang = jnp.concatenate([ang_r, ang_r, ang_c, ang_c], axis=-1)
        cos, sin = jnp.cos(ang), jnp.sin(ang)
    else:
        cos, sin = jnp.ones((n, MLA_ROPE), F32), jnp.zeros((n, MLA_ROPE), F32)
    first = (np.arange(MLA_ROPE) % (2 * quarter)) < quarter
    pad = jnp.zeros((n, HEAD_SLOT - MLA_QK), F32)

    def slot(nope, rope):
        return jnp.concatenate([jnp.broadcast_to(nope, (n, MLA_NOPE)), rope, pad], axis=-1)

    zero = jnp.zeros((MLA_NOPE,), F32)
    qg, kg = q_gain * (MLA_QK ** -0.5 * LOG2_E), k_gain
    tq1 = slot(qg[:MLA_NOPE], cos * qg[MLA_NOPE:])
    tq2 = slot(zero, sin * _rope_partner(qg[MLA_NOPE:]))
    tk1 = slot(kg[:MLA_NOPE], cos * kg[MLA_NOPE:])
    ksin = sin * _rope_partner(kg[MLA_NOPE:])
    tka = slot(zero, jnp.where(first, -ksin, 0.0))
    tkb = slot(zero, jnp.where(first, 0.0, ksin))
    return tq1, tq2, tk1, tka, tkb


def _rope_partner_weight(wuq):
    k = wuq.shape[0]
    quarter = MLA_ROPE // 4
    w = wuq.reshape(k, MLA_HEADS, HEAD_SLOT)
    rope = w[:, :, MLA_NOPE:MLA_QK].reshape(k, MLA_HEADS, 2, 2, quarter)
    rot = jnp.stack([-rope[:, :, :, 1, :], rope[:, :, :, 0, :]], axis=3).reshape(k, MLA_HEADS, MLA_ROPE)
    out = jnp.concatenate([jnp.zeros_like(w[:, :, :MLA_NOPE]), rot, jnp.zeros_like(w[:, :, MLA_QK:])], axis=-1)
    return out.reshape(k, MLA_HEADS * HEAD_SLOT)


def _ev_in_proj(x, g, mods, row0, w, tables, qn, kvn, wuq, wuqr, wuk, wuv, tm=512):
    G, M, D = x.shape
    assert M % tm == 0
    HS = MLA_HEADS * HEAD_SLOT
    full = lambda shape: pl.BlockSpec(shape, lambda b, i: (0,) * len(shape))
    tab = pl.BlockSpec((tm, HEAD_SLOT), lambda b, i: (i, 0))
    rows = lambda width: pl.BlockSpec((1, tm, width), lambda b, i: (b, i, 0))
    return pl.pallas_call(
        _nmm_mla_kernel,
        out_shape=(jax.ShapeDtypeStruct((G, M, 2 * LRU_W), BF16), jax.ShapeDtypeStruct((G, M, HS), BF16),
                   jax.ShapeDtypeStruct((G, M, HS), BF16), jax.ShapeDtypeStruct((G, HS, M), BF16)),
        grid=(G, M // tm),
        in_specs=[rows(D), full((1, D)), _mod_spec(row0, 0), _mod_spec(row0, 1), full((D, EV_PAD)),
                  tab, tab, tab, tab, tab,
                  full((1, MLA_Q_LORA)), full((1, MLA_KV_LORA)),
                  full((MLA_Q_LORA, HS)), full((MLA_Q_LORA, HS)), full((MLA_KV_LORA, HS)), full((MLA_KV_LORA, HS))],
        out_specs=(rows(2 * LRU_W), rows(HS), rows(HS), pl.BlockSpec((1, HS, tm), lambda b, i: (b, 0, i))),
        compiler_params=_cparams(("parallel", "arbitrary"), 48),
        name="ev_in_proj_mla_prep",
    )(x, g.reshape(1, D), mods, mods, w, *tables, qn, kvn, wuq, wuqr, wuk, wuv)


def _dot_nt(a, b):
    return lax.dot_general(a, b, (((1,), (1,)), ((), ())), preferred_element_type=F32)


def _value_slot_t(v, ones_row):
    lane = lax.broadcasted_iota(jnp.int32, v.shape, 1)
    slot = jnp.where(lane < ones_row, v, jnp.where(lane == ones_row, 1.0, 0.0))
    return jnp.transpose(slot).astype(BF16)


ATTN_GROUP = 8


def _col_max(a, rows=8):
    parts = [a[r:r + rows] for r in range(0, a.shape[0], rows)]
    while len(parts) > 1:
        parts = [jnp.maximum(parts[i], parts[i + 1]) if i + 1 < len(parts) else parts[i]
                 for i in range(0, len(parts), 2)]
    return jnp.max(parts[0], axis=0, keepdims=True)


def _attend_t(heads):
    scores = []
    for q, keys, _, biases in heads:
        s = [_dot_nt(k, q) for k in keys]
        if biases is not None:
            s = [a if b is None else a + b for a, b in zip(s, biases)]
        scores.append(s)
    maxes = [functools.reduce(jnp.maximum, [_col_max(a) for a in s]) for s in scores]
    outs = []
    for (_, _, values_t, _), s, m in zip(heads, scores, maxes):
        out_t = None
        for a, vt in zip(s, values_t):
            part = jnp.dot(vt, jnp.exp2(a - m).astype(BF16), preferred_element_type=F32)
            out_t = part if out_t is None else out_t + part
        outs.append(out_t)
    return outs


def _pair_output(slots_t, dv):
    halves = [t[:dv] / t[dv:dv + 1] for t in slots_t]
    return jnp.transpose(jnp.concatenate(halves, axis=0))


def _mla_attn_kernel(*refs, with_latent):
    if with_latent:
        q_ref, kl_ref, vtl_ref, kc_ref, vtc_ref, o_ref = refs
    else:
        q_ref, kc_ref, vtc_ref, o_ref = refs
    for h0 in range(0, MLA_HEADS, ATTN_GROUP):
        heads = []
        for h in range(h0, h0 + ATTN_GROUP):
            sl = slice(h * HEAD_SLOT, (h + 1) * HEAD_SLOT)
            keys, values_t = [kc_ref[0, :, sl]], [vtc_ref[0, sl, :]]
            if with_latent:
                keys.append(kl_ref[0, :, sl])
                values_t.append(vtl_ref[0, sl, :])
            heads.append((q_ref[0, :, sl], keys, values_t, None))
        outs = _attend_t(heads)
        for i in range(0, ATTN_GROUP, 2):
            hp = (h0 + i) // 2
            o_ref[0, :, hp * LANE:(hp + 1) * LANE] = _pair_output(outs[i:i + 2], MLA_V).astype(BF16)


def _mla_attention(q, kl, vtl, kc, vtc, tq=512):
    B, M, HS = q.shape
    tq = min(tq, M)
    assert M % tq == 0
    HV = MLA_HEADS * MLA_V
    Nc = kc.shape[1]
    with_latent = kl is not None
    whole = lambda n, w: pl.BlockSpec((1, n, w), lambda b, i: (b, 0, 0))
    in_specs = [pl.BlockSpec((1, tq, HS), lambda b, i: (b, i, 0))]
    args = [q]
    if with_latent:
        in_specs += [whole(kl.shape[1], HS), whole(HS, kl.shape[1])]
        args += [kl, vtl]
    in_specs += [whole(Nc, HS), pl.BlockSpec((1, HS, Nc), lambda b, i: (0, 0, b))]
    args += [kc, vtc]
    return pl.pallas_call(
        functools.partial(_mla_attn_kernel, with_latent=with_latent),
        out_shape=jax.ShapeDtypeStruct((B, M, HV), BF16),
        grid=(B, M // tq),
        in_specs=in_specs,
        out_specs=pl.BlockSpec((1, tq, HV), lambda b, i: (b, i, 0)),
        compiler_params=_cparams(("parallel", "arbitrary"), 60),
        name="mla_attention",
    )(*args)


LRU_CW = 512
LRU_HALO = 16
LRU_TN = 256


def _gelu_tanh(x):
    return 0.5 * x * (1.0 + jnp.tanh(0.7978845608028654 * (x + 0.044715 * (x * x * x))))


def _scan_group(a, bv, h, reverse):
    row = lax.broadcasted_iota(jnp.int32, a.shape, 0)
    for s in (1, 2, 4):
        if reverse:
            keep = row < 8 - s
            shift = 8 - s
        else:
            keep = row >= s
            shift = s
        a_s = jnp.where(keep, pltpu.roll(a, shift, 0), 1.0)
        b_s = jnp.where(keep, pltpu.roll(bv, shift, 0), 0.0)
        bv = a * b_s + bv
        a = a * a_s
    hs = a * h + bv
    return hs, (hs[0:1, :] if reverse else hs[7:8, :])


def _lru_kernel(uxl_ref, ugl_ref, uxc_ref, ugc_ref, cw_ref, cb_ref, wg_ref, bg_ref, lam_ref, yl_ref, yc_ref,
                xpl_ref, xpc_ref, xcv_ref, af_ref, bf_ref, ab_ref, bb_ref):
    N, Nc = uxl_ref.shape[1], uxc_ref.shape[1]
    NT = N + Nc
    C = LRU_CW
    H = LRU_HALO

    def conv(src_ref, pad_ref, n, row0):
        pad_ref[0:H, :] = jnp.zeros((H, C), F32)
        pad_ref[H + n:H + n + H, :] = jnp.zeros((H, C), F32)
        pad_ref[H:H + n, :] = src_ref[0].astype(F32)
        y = cb_ref[...] + pad_ref[H - CONV_LEFT:H - CONV_LEFT + n, :] * cw_ref[0:1, :]
        for j in range(1, CONV_W):
            y = y + pad_ref[H - CONV_LEFT + j:H - CONV_LEFT + j + n, :] * cw_ref[j:j + 1, :]
        xcv_ref[row0:row0 + n, :] = y

    conv(uxc_ref, xpc_ref, Nc, 0)
    conv(uxl_ref, xpl_ref, N, Nc)

    lam = lam_ref[...]
    c_half = (-0.5 * LRU_C) * (jnp.maximum(-lam, 0.0) + jnp.log1p(jnp.exp(-jnp.abs(lam))))

    def coeff_chunk(i, carry):
        r0 = pl.multiple_of(i * LRU_TN, LRU_TN)
        x = xcv_ref[pl.ds(r0, LRU_TN), :]
        t = jnp.tanh(jnp.dot(x.astype(BF16), wg_ref[0], preferred_element_type=F32) + bg_ref[0])
        hx = 0.5 * x
        for d, (a_ref, b_ref) in enumerate(((af_ref, bf_ref), (ab_ref, bb_ref))):
            c = c_half[d:d + 1, :]
            a = jnp.exp(c * t[:, (2 * d) * C:(2 * d + 1) * C] + c)
            a_ref[pl.ds(r0, LRU_TN), :] = a
            gated_x = hx * t[:, (2 * d + 1) * C:(2 * d + 2) * C] + hx
            b_ref[pl.ds(r0, LRU_TN), :] = jnp.sqrt(1.0 - a * a) * gated_x
        return carry

    lax.fori_loop(0, NT // LRU_TN, coeff_chunk, 0)

    ngc, ngt = Nc // 8, NT // 8

    def scan_step(i, carry):
        hf, hb = carry
        rf = pl.multiple_of(i * 8, 8)
        rb = pl.multiple_of(jnp.where(i < ngc, ngc - 1 - i, ngt + ngc - 1 - i) * 8, 8)
        hs_f, hf = _scan_group(af_ref[pl.ds(rf, 8), :], bf_ref[pl.ds(rf, 8), :], hf, False)
        hs_b, hb = _scan_group(ab_ref[pl.ds(rb, 8), :], bb_ref[pl.ds(rb, 8), :], hb, True)
        bf_ref[pl.ds(rf, 8), :] = hs_f
        bb_ref[pl.ds(rb, 8), :] = hs_b
        return hf, hb

    zero = jnp.zeros((1, C), F32)
    lax.fori_loop(0, ngt, scan_step, (zero, zero), unroll=4)

    def out_chunk(i, carry):
        r0 = pl.multiple_of(i * LRU_TN, LRU_TN)
        hsum = bf_ref[pl.ds(Nc + r0, LRU_TN), :] + bb_ref[pl.ds(Nc + r0, LRU_TN), :]
        gate = _gelu_tanh(ugl_ref[0, pl.ds(r0, LRU_TN), :].astype(F32))
        yl_ref[0, pl.ds(r0, LRU_TN), :] = (hsum * gate).astype(BF16)
        return carry

    lax.fori_loop(0, N // LRU_TN, out_chunk, 0)
    yc_ref[0] = ((bf_ref[0:Nc, :] + bb_ref[0:Nc, :]) * _gelu_tanh(ugc_ref[0].astype(F32))).astype(BF16)


def _rglru(ul, uc, conv_w, conv_b, wg, bg, lam):
    B, N, _ = ul.shape
    Nc = uc.shape[1]
    C = LRU_CW
    nh = LRU_W // C
    NT = N + Nc
    col = lambda base: (lambda b, j: (b, 0, base // C + j))
    par = lambda rows: pl.BlockSpec((rows, C), lambda b, j: (0, j))
    return pl.pallas_call(
        _lru_kernel,
        out_shape=(jax.ShapeDtypeStruct((B, N, LRU_W), BF16), jax.ShapeDtypeStruct((B, Nc, LRU_W), BF16)),
        grid=(B, nh),
        in_specs=[pl.BlockSpec((1, N, C), col(0)), pl.BlockSpec((1, N, C), col(LRU_W)),
                  pl.BlockSpec((1, Nc, C), col(0)), pl.BlockSpec((1, Nc, C), col(LRU_W)),
                  par(CONV_W), par(1),
                  pl.BlockSpec((1, C, 4 * C), lambda b, j: (j, 0, 0)),
                  pl.BlockSpec((1, 1, 4 * C), lambda b, j: (j, 0, 0)),
                  par(2)],
        out_specs=(pl.BlockSpec((1, N, C), lambda b, j: (b, 0, j)),
                   pl.BlockSpec((1, Nc, C), lambda b, j: (b, 0, j))),
        scratch_shapes=[pltpu.VMEM((N + 2 * LRU_HALO, C), F32), pltpu.VMEM((Nc + 2 * LRU_HALO, C), F32),
                        pltpu.VMEM((NT, C), F32)] + [pltpu.VMEM((NT, C), F32)] * 4,
        compiler_params=_cparams(("parallel", "arbitrary"), 58),
        name="rglru",
    )(ul, ul, uc, uc, conv_w, conv_b.reshape(1, LRU_W), wg, bg, lam)


def _lru_gate_weights(w_a, b_a, w_x, b_x):
    C = LRU_CW
    nh = LRU_W // C
    kb = C // LRU_BS

    def dense(w):
        w = w.reshape(nh, kb, LRU_BS, LRU_BS)
        eye = jnp.eye(kb, dtype=w.dtype)
        return jnp.einsum('hkij,kl->hkilj', w, eye).reshape(nh, C, C)

    wg = jnp.concatenate([dense(w_a[0]), dense(w_x[0]), dense(w_a[1]), dense(w_x[1])], axis=-1)
    bg = jnp.stack([b_a[0], b_x[0], b_a[1], b_x[1]], axis=0).reshape(4, nh, C)
    bg = jnp.transpose(bg, (1, 0, 2)).reshape(nh, 1, 4 * C)
    return (0.5 * wg).astype(BF16), 0.5 * bg


GLA_SC = 256
GLA_UNROLL = 2
GLA_QK_W = GLA_HEADS * GLA_DK
GLA_V_W = GLA_HEADS * GLA_DV


def _split3(x):
    hi = x.astype(BF16)
    r1 = x - hi.astype(F32)
    mid = r1.astype(BF16)
    lo = (r1 - mid.astype(F32)).astype(BF16)
    return hi, mid, lo


def _gla_kernel(ql_ref, kl_ref, vl_ref, gl_ref, lrl_ref, kc_ref, vc_ref, lrc_ref, wa_ref, ba_ref, og_ref,
                y_ref, oacc_ref, st_ref):
    N, Nc = ql_ref.shape[1], kc_ref.shape[1]
    T = GLA_SC
    CH = GLA_CHUNK
    npair = GLA_HEADS // 2
    row = lax.broadcasted_iota(jnp.int32, (T, T), 0)
    colm = lax.broadcasted_iota(jnp.int32, (T, T), 1)
    same_chunk = (row // CH) == (colm // CH)
    causal = (same_chunk & (colm <= row), same_chunk & (colm >= row))
    tri = tuple(jnp.where(c, 1.0, 0.0).astype(BF16) for c in causal)
    lane_lo = lax.broadcasted_iota(jnp.int32, (T, LANE), 1) < GLA_DK
    srow = lax.broadcasted_iota(jnp.int32, (LANE, 2 * GLA_DV), 0)
    scol = lax.broadcasted_iota(jnp.int32, (LANE, 2 * GLA_DV), 1)
    state_mask = (srow < GLA_DK) == (scol < GLA_DV)
    qscale = GLA_DK ** -0.5

    def superchunks(streams, q_ref, k_ref, v_ref, lr_ref, with_output):
        nch = T // CH
        work = []
        for d, r0 in streams:
            z = jnp.dot(lr_ref[0, pl.ds(r0, T), :], wa_ref[d], preferred_element_type=F32) + ba_ref[d]
            work.append(dict(d=d, r0=r0, split=_split3(jax.nn.log_sigmoid(z) * (1.0 / GLA_TAU))))
        for w in work:
            d = w['d']
            cum = sum(jnp.dot(tri[d], t, preferred_element_type=F32) for t in w['split'])
            last = (CH - 1) if d == 0 else 0
            tot = [cum[c * CH + last:c * CH + last + 1, :] for c in range(nch)]
            tot_rows = jnp.concatenate([jnp.broadcast_to(t, (CH, GLA_QK_W)) for t in tot], axis=0)
            k = k_ref[0, pl.ds(w['r0'], T), :].astype(F32)
            w.update(cum=cum, tot=tot, k=k, v=v_ref[0, pl.ds(w['r0'], T), :],
                     k_dec=(k * jnp.exp(tot_rows - cum)).astype(BF16))
        for w in work:
            ds, dec = {}, {}
            for c in range(nch):
                rs = slice(c * CH, (c + 1) * CH)
                for p in range(npair):
                    kd = w['k_dec'][rs, p * LANE:(p + 1) * LANE]
                    vv = w['v'][rs, p * 2 * GLA_DV:(p + 1) * 2 * GLA_DV]
                    kv = lax.dot_general(kd, vv, (((0,), (0,)), ((), ())), preferred_element_type=F32)
                    ds[c, p] = jnp.where(state_mask, kv, 0.0)
                    dec_row = jnp.exp(w['tot'][c][:, p * LANE:(p + 1) * LANE])
                    dec_col = jnp.transpose(jnp.broadcast_to(dec_row, (LANE, LANE)))
                    dec[c, p] = jnp.concatenate([dec_col, dec_col], axis=1)
            w.update(ds=ds, dec=dec)
        for w in work:
            d = w['d']
            order = range(nch) if d == 0 else range(nch - 1, -1, -1)
            s_in = {}
            for p in range(npair):
                s = st_ref[d, p]
                for c in order:
                    s_in[c, p] = s.astype(BF16)
                    s = w['dec'][c, p] * s + w['ds'][c, p]
                st_ref[d, p] = s
            w.update(s_in=s_in)
        if not with_output:
            return
        for w in work:
            q = q_ref[0, pl.ds(w['r0'], T), :].astype(F32)
            w.update(q_dec=((q * qscale) * jnp.exp(w['cum'])).astype(BF16),
                     k_inv=(w['k'] * jnp.exp(-w['cum'])).astype(BF16))
        for w in work:
            d, r0 = w['d'], w['r0']
            for p in range(npair):
                qp = w['q_dec'][:, p * LANE:(p + 1) * LANE]
                kp = w['k_inv'][:, p * LANE:(p + 1) * LANE]
                o_inter = jnp.concatenate(
                    [jnp.dot(qp[c * CH:(c + 1) * CH], w['s_in'][c, p], preferred_element_type=F32)
                     for c in range(nch)], axis=0)
                for j in range(2):
                    h = 2 * p + j
                    qm = jnp.where(lane_lo if j == 0 else jnp.logical_not(lane_lo), qp, jnp.zeros_like(qp))
                    att = jnp.where(causal[d], _dot_nt(qm, kp), 0.0).astype(BF16)
                    o = (jnp.dot(att, w['v'][:, h * GLA_DV:(h + 1) * GLA_DV], preferred_element_type=F32)
                         + o_inter[:, j * GLA_DV:(j + 1) * GLA_DV])
                    oacc_ref[d, pl.ds(r0, T), h * GLA_DV:(h + 1) * GLA_DV] = o

    st_ref[...] = jnp.zeros_like(st_ref)
    ncs, nls = Nc // T, N // T

    def ctx_body(i, carry):
        streams = [(0, pl.multiple_of(i * T, T)), (1, pl.multiple_of((ncs - 1 - i) * T, T))]
        superchunks(streams, None, kc_ref, vc_ref, lrc_ref, False)
        return carry

    def lat_body(i, carry):
        streams = []
        for u in range(GLA_UNROLL):
            streams += [(0, pl.multiple_of((GLA_UNROLL * i + u) * T, T)),
                        (1, pl.multiple_of((nls - 1 - GLA_UNROLL * i - u) * T, T))]
        superchunks(streams, ql_ref, kl_ref, vl_ref, lrl_ref, True)
        return carry

    lax.fori_loop(0, ncs, ctx_body, 0)
    lax.fori_loop(0, nls // GLA_UNROLL, lat_body, 0)

    def fin(i, carry):
        r0 = pl.multiple_of(i * T, T)
        g = gl_ref[0, pl.ds(r0, T), :].astype(F32)
        for h in range(GLA_HEADS):
            sl = slice(h * GLA_DV, (h + 1) * GLA_DV)
            gh = g[:, sl]
            o = oacc_ref[0, pl.ds(r0, T), sl] + oacc_ref[1, pl.ds(r0, T), sl]
            y_ref[0, pl.ds(r0, T), sl] = (_rms(o, og_ref[...]) * (gh * jax.nn.sigmoid(gh))).astype(BF16)
        return carry

    lax.fori_loop(0, N // T, fin, 0)


def _gla(ul, uc, wa, ba, o_gain):
    B, N, _ = ul.shape
    Nc = uc.shape[1]
    blk = lambda n, w, base: pl.BlockSpec((1, n, w), lambda b: (b, 0, base // w))
    full = lambda shape: pl.BlockSpec(shape, lambda b: (0,) * len(shape))
    return pl.pallas_call(
        _gla_kernel,
        out_shape=jax.ShapeDtypeStruct((B, N, GLA_V_W), BF16),
        grid=(B,),
        in_specs=[blk(N, GLA_QK_W, OD_GQ), blk(N, GLA_QK_W, OD_GK), blk(N, GLA_V_W, OD_GV),
                  blk(N, GLA_V_W, OD_GG), blk(N, LANE, OD_GLR),
                  blk(Nc, GLA_QK_W, OD_GK), blk(Nc, GLA_V_W, OD_GV), blk(Nc, LANE, OD_GLR),
                  full((2, LANE, GLA_QK_W)), full((2, 1, GLA_QK_W)), full((1, GLA_DV))],
        out_specs=pl.BlockSpec((1, N, GLA_V_W), lambda b: (b, 0, 0)),
        scratch_shapes=[pltpu.VMEM((2, N, GLA_V_W), F32),
                        pltpu.VMEM((2, GLA_HEADS // 2, LANE, 2 * GLA_DV), F32)],
        compiler_params=_cparams(("parallel",), 48),
        name="gla",
    )(ul, ul, ul, ul, ul, uc, uc, uc, wa, ba, o_gain)


NA_W = NA_HEADS * NA_HD
NA_QROWS = 4
NA_KROWS = 12
NA_TQ = NA_QROWS * GRID_W
NA_TK = NA_KROWS * GRID_W
NA_NEG = -1e30
NA_BATCH_BLOCK = 2


def _na_kernel(q_ref, k0_ref, k1_ref, k2_ref, vt0_ref, vt1_ref, vt2_ref, kc_ref, vtc_ref, bias_ref, o_ref):
    lane_lo = lax.broadcasted_iota(jnp.int32, (NA_TQ, LANE), 1) < NA_HD
    k_refs = (k0_ref, k1_ref, k2_ref, kc_ref)
    nc = kc_ref.shape[1]
    for bb in range(q_ref.shape[0]):
        for h0 in range(0, NA_HEADS, ATTN_GROUP):
            heads = []
            for h in range(h0, h0 + ATTN_GROUP):
                sl = slice((h // 2) * LANE, (h // 2 + 1) * LANE)
                hs = slice(h * HEAD_SLOT, (h + 1) * HEAD_SLOT)
                qp = q_ref[bb, :, sl]
                qm = jnp.where(lane_lo if h % 2 == 0 else jnp.logical_not(lane_lo), qp, jnp.zeros_like(qp))
                biases = [bias_ref[0, h, t * NA_TQ:(t + 1) * NA_TQ, :] for t in range(3)] + [None]
                values_t = [r[bb, hs, :] for r in (vt0_ref, vt1_ref, vt2_ref)] + [vtc_ref[0, hs, bb * nc:(bb + 1) * nc]]
                heads.append((qm, [r[bb, :, sl] for r in k_refs], values_t, biases))
            outs = _attend_t(heads)
            for i in range(0, ATTN_GROUP, 2):
                hp = (h0 + i) // 2
                o_ref[bb, :, hp * LANE:(hp + 1) * LANE] = _pair_output(outs[i:i + 2], NA_HD).astype(BF16)


def _na_bias_tables(rpb, rows):
    H, ndi, ndj = rpb.shape
    rpb = rpb * LOG2_E
    W = GRID_W
    half = NA_WIN_COLS - 1
    zeros = jnp.zeros((H, ndi, W - 1 - half), rpb.dtype)
    vec = jnp.concatenate([rpb[:, :, half:], zeros, zeros, rpb[:, :, :half]], axis=-1)
    toep = jnp.tile(vec, (1, 1, W))[:, :, :W * (2 * W - 2)].reshape(H, ndi, W, 2 * W - 2)[..., :W]
    qc = np.arange(W)[:, None]
    kc = np.arange(W)[None, :]
    c0 = np.clip(qc - NA_WIN_COLS // 2, 0, W - NA_WIN_COLS)
    toep = jnp.where(jnp.asarray((kc >= c0) & (kc < c0 + NA_WIN_COLS)), toep, NA_NEG)
    toep = jnp.swapaxes(toep, 2, 3)
    nblk = rows // NA_QROWS
    plan = []
    for m in (0, 1, nblk - 1):
        kb = min(max(m - 1, 0), nblk - NA_KROWS // NA_QROWS)
        blocks = []
        for i in range(NA_KROWS):
            for a in range(NA_QROWS):
                kr, qr = kb * NA_QROWS + i, m * NA_QROWS + a
                r0 = min(max(qr - NA_WIN_ROWS // 2, 0), rows - NA_WIN_ROWS)
                blocks.append((i, a, kr - qr + NA_WIN_ROWS - 1 if r0 <= kr < r0 + NA_WIN_ROWS else None))
        plan.append(blocks)

    def expand_kernel(toep_ref, out_ref):
        for v, blocks in enumerate(plan):
            for i, a, di in blocks:
                blk = jnp.full((W, W), NA_NEG, F32) if di is None else toep_ref[0, di]
                out_ref[v, 0, i * W:(i + 1) * W, a * W:(a + 1) * W] = blk

    return pl.pallas_call(
        expand_kernel,
        out_shape=jax.ShapeDtypeStruct((len(plan), H, NA_TK, NA_TQ), F32),
        grid=(H,),
        in_specs=[pl.BlockSpec((1, ndi, W, W), lambda h: (h, 0, 0, 0))],
        out_specs=pl.BlockSpec((len(plan), 1, NA_TK, NA_TQ), lambda h: (0, h, 0, 0)),
        compiler_params=_cparams(("arbitrary",), 32),
        name="na_bias_expand",
    )(toep)


def _natten(ul, vt, uc, vtc, bias):
    B, N, _ = ul.shape
    Nc = uc.shape[1]
    nblk = N // NA_TQ
    kmax = nblk - NA_KROWS // NA_QROWS
    VS = NA_HEADS * HEAD_SLOT
    qcol, kcol = OD_NQ // NA_W, OD_NK // NA_W

    bb = NA_BATCH_BLOCK
    assert B % bb == 0

    def kspec(t):
        return pl.BlockSpec((bb, NA_TQ, NA_W), lambda m, b: (b, jnp.clip(m - 1, 0, kmax) + t, kcol))

    def vspec(t):
        return pl.BlockSpec((bb, VS, NA_TQ), lambda m, b: (b, 0, jnp.clip(m - 1, 0, kmax) + t))

    return pl.pallas_call(
        _na_kernel,
        out_shape=jax.ShapeDtypeStruct((B, N, NA_W), BF16),
        grid=(nblk, B // bb),
        in_specs=[pl.BlockSpec((bb, NA_TQ, NA_W), lambda m, b: (b, m, qcol)),
                  kspec(0), kspec(1), kspec(2), vspec(0), vspec(1), vspec(2),
                  pl.BlockSpec((bb, Nc, NA_W), lambda m, b: (b, 0, kcol)),
                  pl.BlockSpec((1, VS, bb * Nc), lambda m, b: (0, 0, b)),
                  pl.BlockSpec((1, NA_HEADS, NA_TK, NA_TQ),
                               lambda m, b: (jnp.where(m == 0, 0, jnp.where(m == nblk - 1, 2, 1)), 0, 0, 0))],
        out_specs=pl.BlockSpec((bb, NA_TQ, NA_W), lambda m, b: (b, m, 0)),
        compiler_params=_cparams(("arbitrary", "arbitrary"), 48),
        name="natten",
    )(ul, ul, ul, ul, vt, vt, vt, uc, vtc, bias)


def _place(w, layout, total):
    pieces, pos = [], 0
    for src, width, dst in sorted(layout, key=lambda item: item[2]):
        if dst > pos:
            pieces.append(jnp.zeros((w.shape[0], dst - pos), w.dtype))
        pieces.append(w[:, src:src + width])
        pos = dst + width
    if pos < total:
        pieces.append(jnp.zeros((w.shape[0], total - pos), w.dtype))
    return jnp.concatenate(pieces, axis=1)


def _ev_in_weight(w):
    lay = [(0, 256, EV_UQ), (256, 128, EV_UKV), (384, MLA_ROPE, EV_UKR + MLA_NOPE),
           (416, LRU_W, EV_UX), (416 + LRU_W, LRU_W, EV_UG)]
    return _place(w, lay, EV_PAD).astype(BF16)


def _od_in_weight(w):
    src = np.cumsum([0, 256, 256, 512, 512, 2 * GLA_LR, 512, 512, 512])
    dst = [OD_GQ, OD_GK, OD_GV, OD_GG, OD_GLR, OD_NQ, OD_NK, OD_NV]
    lay = [(int(src[i]), int(src[i + 1] - src[i]), dst[i]) for i in range(8)]
    return _place(w, lay, OD_PAD).astype(BF16)


def _pad_heads(w, heads, width):
    k = w.shape[0]
    return jnp.pad(w.reshape(k, heads, width), ((0, 0), (0, 0), (0, HEAD_SLOT - width))).reshape(k, heads * HEAD_SLOT)


def kernel(x, c, ctx, c_ctx, ada_w, ada_b, norm_mix, norm_mlp, w_out, mlp_w1, mlp_w2,
           ev_w_in, mla_q_norm, mla_w_uq, mla_kv_norm, mla_w_ukv, mla_q_gain, mla_k_gain,
           lru_conv_w, lru_conv_b, lru_w_a, lru_b_a, lru_w_x, lru_b_x, lru_lam,
           od_w_in, gla_w_a, gla_b_a, gla_o_gain, na_q_gain, na_k_gain, na_rpb):
    B, N, D = x.shape
    Nc = ctx.shape[1]
    depth = ada_w.shape[0]

    R = -(-(B + 1) // 8) * 8
    cc = jnp.concatenate([c, c_ctx[None], jnp.zeros((R - B - 1, D), c.dtype)], axis=0)
    mods = _modulation(cc, ada_w, ada_b).reshape(depth * R, 1, 6 * D)

    w1, w2, wo = mlp_w1.astype(BF16), mlp_w2.astype(BF16), w_out.astype(BF16)
    xl = x
    xc = ctx.reshape(1, B * Nc, D)
    for l in range(depth):
        last = l == depth - 1
        j = l // 2
        row_l, row_c = l * R, l * R + B
        if l % 2 == 0:
            w_in = _ev_in_weight(ev_w_in[j])
            wuq = _pad_heads(mla_w_uq[j], MLA_HEADS, MLA_QK)
            wukv = mla_w_ukv[j].reshape(MLA_KV_LORA, MLA_HEADS, MLA_NOPE + MLA_V)
            wuk = _pad_heads(wukv[:, :, :MLA_NOPE].reshape(MLA_KV_LORA, -1), MLA_HEADS, MLA_NOPE).astype(BF16)
            wuv = _pad_heads(wukv[:, :, MLA_NOPE:].reshape(MLA_KV_LORA, -1), MLA_HEADS, MLA_V).astype(BF16)
            proj = functools.partial(_ev_in_proj, g=norm_mix[l], mods=mods, w=w_in, qn=mla_q_norm[j].reshape(1, -1),
                                     kvn=mla_kv_norm[j].reshape(1, -1), wuq=wuq.astype(BF16),
                                     wuqr=_rope_partner_weight(wuq).astype(BF16), wuk=wuk, wuv=wuv)
            ul, ql, kl, vl = proj(xl, row0=row_l, tables=_rope_tables(N, mla_q_gain[j], mla_k_gain[j], True))
            uc, qc, kc, vc = proj(xc, row0=row_c, tables=_rope_tables(B * Nc, mla_q_gain[j], mla_k_gain[j], False))
            uc, qc, kc = (t.reshape(B, Nc, -1) for t in (uc, qc, kc))
            ya_l = _mla_attention(ql, kl, vl, kc, vc)
            ya_c = None if last else _mla_attention(qc, None, None, kc, vc)
            wg, bg = _lru_gate_weights(lru_w_a[j], lru_b_a[j], lru_w_x[j], lru_b_x[j])
            yb_l, yb_c = _rglru(ul, uc, lru_conv_w[j], lru_conv_b[j], wg, bg, lru_lam[j])
        else:
            w_in = _od_in_weight(od_w_in[j])
            na_gains = (jnp.tile(na_q_gain[j] * (NA_HD ** -0.5 * LOG2_E), NA_HEADS).reshape(1, NA_HEADS * NA_HD),
                        jnp.tile(na_k_gain[j], NA_HEADS).reshape(1, NA_HEADS * NA_HD))
            ul, nvt = _norm_mod_matmul(xl, norm_mix[l], mods, row_l, w_in, na_gains)
            uc, nvtc = _norm_mod_matmul(xc, norm_mix[l], mods, row_c, w_in, na_gains)
            uc = uc.reshape(B, Nc, OD_PAD)
            if not last:
                raise NotImplementedError("context outputs of the odd-layer mixers are only needed when depth > 2")
            wa = jnp.stack([jnp.pad(gla_w_a[j, d], ((d * GLA_LR, LANE - (d + 1) * GLA_LR), (0, 0))) for d in range(2)])
            ya_l = _gla(ul, uc, wa.astype(BF16), gla_b_a[j].reshape(2, 1, GLA_QK_W), gla_o_gain[j].reshape(1, GLA_DV))
            ya_c = None
            yb_l = _natten(ul, nvt, uc, nvtc, _na_bias_tables(na_rpb[j], N // GRID_W))
            yb_c = None
        xl = _out_proj_mlp(xl, ya_l, yb_l, wo, norm_mlp[l], mods, row_l, w1, w2, l)
        if not last:
            xc = _out_proj_mlp(xc, ya_c.reshape(1, B * Nc, -1), yb_c.reshape(1, B * Nc, -1), wo, norm_mlp[l], mods,
                               row_c, w1, w2, l)
    return xl
```

```python
import functools

import numpy as np
import jax
import jax.numpy as jnp
from jax import lax
from jax.experimental import pallas as pl
from jax.experimental.pallas import tpu as pltpu

F32 = jnp.float32
BF16 = jnp.bfloat16

D_MODEL = 1024
GRID_W = 64
EPS = 1e-6
LOG2_E = 1.4426950408889634
ROPE_BASE = 10000.0

MLA_HEADS = 8
MLA_NOPE = 64
MLA_ROPE = 32
MLA_QK = MLA_NOPE + MLA_ROPE
MLA_V = 64
MLA_Q_LORA = 256
MLA_KV_LORA = 128

LRU_W = 512
LRU_BLOCKS = 8
LRU_BS = LRU_W // LRU_BLOCKS
LRU_C = 8.0
CONV_W = 4
CONV_LEFT = 2

GLA_HEADS = 4
GLA_DK = 64
GLA_DV = 128
GLA_LR = 16
GLA_TAU = 16.0
GLA_CHUNK = 64

NA_HD = 64
NA_HEADS = 8
NA_WIN_ROWS = 8
NA_WIN_COLS = 16

LANE = 128
HEAD_SLOT = 128

EV_UQ, EV_UKV, EV_UKR, EV_UX, EV_UG = 0, 256, 384, 512, 1024
EV_PAD = 1536
OD_NQ, OD_NK, OD_NV, OD_GQ, OD_GK, OD_GV, OD_GG, OD_GLR = 0, 512, 1024, 1536, 1792, 2048, 2560, 3072
OD_PAD = 3200


def _cparams(semantics, vmem_mib):
    return pltpu.CompilerParams(dimension_semantics=semantics, vmem_limit_bytes=vmem_mib << 20)


def _rms(x, g):
    return x * lax.rsqrt(jnp.mean(x * x, axis=-1, keepdims=True) + EPS) * g


def _ada_kernel(c_ref, w_ref, b_ref, o_ref):
    cv = c_ref[...]
    s = cv * jax.nn.sigmoid(cv)
    o_ref[0] = jnp.dot(s.astype(BF16), w_ref[0].astype(BF16), preferred_element_type=F32) + b_ref[0]


def _modulation(cc, ada_w, ada_b):
    L, D, D6 = ada_w.shape
    R = cc.shape[0]
    tn = 1536
    return pl.pallas_call(
        _ada_kernel,
        out_shape=jax.ShapeDtypeStruct((L, R, D6), F32),
        grid=(L, D6 // tn),
        in_specs=[pl.BlockSpec((R, D), lambda l, j: (0, 0)),
                  pl.BlockSpec((1, D, tn), lambda l, j: (l, 0, j)),
                  pl.BlockSpec((1, 1, tn), lambda l, j: (l, 0, j))],
        out_specs=pl.BlockSpec((1, R, tn), lambda l, j: (l, 0, j)),
        compiler_params=_cparams(("arbitrary", "arbitrary"), 40),
        name="adaln_modulation",
    )(cc, ada_w, ada_b.reshape(L, 1, D6))


def _mod_spec(row0, k):
    return pl.BlockSpec((1, 1, D_MODEL), lambda g, i: (row0 + g, 0, k))


def _modulated_proj(x_ref, g_ref, sh_ref, sc_ref, w_ref):
    h = _rms(x_ref[0], g_ref[...]) * (1.0 + sc_ref[0]) + sh_ref[0]
    return jnp.dot(h.astype(BF16), w_ref[...], preferred_element_type=F32)


def _nmm_kernel(x_ref, g_ref, sh_ref, sc_ref, w_ref, o_ref):
    o_ref[0] = _modulated_proj(x_ref, g_ref, sh_ref, sc_ref, w_ref).astype(BF16)


def _nmm_na_kernel(x_ref, g_ref, sh_ref, sc_ref, w_ref, qg_ref, kg_ref, o_ref, vt_ref):
    u = _modulated_proj(x_ref, g_ref, sh_ref, sc_ref, w_ref)
    plain = OD_NK + NA_HEADS * NA_HD
    o_ref[0, :, plain:] = u[:, plain:].astype(BF16)
    lane_lo = lax.broadcasted_iota(jnp.int32, (u.shape[0], LANE), 1) < NA_HD
    for base, gain_ref in ((OD_NQ, qg_ref), (OD_NK, kg_ref)):
        for p in range(NA_HEADS // 2):
            t = u[:, base + p * LANE:base + (p + 1) * LANE]
            sq = t * t
            s_lo = jnp.sum(jnp.where(lane_lo, sq, 0.0), axis=-1, keepdims=True)
            s_hi = jnp.sum(jnp.where(lane_lo, 0.0, sq), axis=-1, keepdims=True)
            ms = jnp.where(lane_lo, s_lo, s_hi) * (1.0 / NA_HD)
            gain = gain_ref[:, p * LANE:(p + 1) * LANE]
            o_ref[0, :, base + p * LANE:base + (p + 1) * LANE] = (t * lax.rsqrt(ms + EPS) * gain).astype(BF16)
    for p in range(NA_HEADS // 2):
        pair = u[:, OD_NV + p * LANE:OD_NV + (p + 1) * LANE]
        vt_ref[0, (2 * p) * HEAD_SLOT:(2 * p + 1) * HEAD_SLOT, :] = _value_slot_t(pair, NA_HD)
        vt_ref[0, (2 * p + 1) * HEAD_SLOT:(2 * p + 2) * HEAD_SLOT, :] = _value_slot_t(pltpu.roll(pair, NA_HD, 1), NA_HD)


def _norm_mod_matmul(x, g, mods, row0, w, na_gains=None, tm=512):
    G, M, D = x.shape
    assert M % tm == 0
    Nout = w.shape[1]
    in_specs = [pl.BlockSpec((1, tm, D), lambda b, i: (b, i, 0)),
                pl.BlockSpec((1, D), lambda b, i: (0, 0)),
                _mod_spec(row0, 0), _mod_spec(row0, 1),
                pl.BlockSpec((D, Nout), lambda b, i: (0, 0))]
    u_shape = jax.ShapeDtypeStruct((G, M, Nout), BF16)
    u_spec = pl.BlockSpec((1, tm, Nout), lambda b, i: (b, i, 0))
    if na_gains is None:
        kern, args, out_shape, out_specs = _nmm_kernel, (), u_shape, u_spec
    else:
        VS = NA_HEADS * HEAD_SLOT
        kern, args = _nmm_na_kernel, tuple(na_gains)
        in_specs += [pl.BlockSpec((1, NA_HEADS * NA_HD), lambda b, i: (0, 0))] * 2
        out_shape = (u_shape, jax.ShapeDtypeStruct((G, VS, M), BF16))
        out_specs = (u_spec, pl.BlockSpec((1, VS, tm), lambda b, i: (b, 0, i)))
    return pl.pallas_call(
        kern,
        out_shape=out_shape,
        grid=(G, M // tm),
        in_specs=in_specs,
        out_specs=out_specs,
        compiler_params=_cparams(("parallel", "arbitrary"), 48),
        name="norm_mod_in_proj",
    )(x, g.reshape(1, D), mods, mods, w, *args)


MLP_TF = 1024


def _out_mlp_kernel(x_ref, ya_ref, yb_ref, woa_ref, wob_ref, g_ref, m2_ref, m3_ref, m4_ref, m5_ref,
                    w1_ref, w2_ref, o_ref, h_ref, a_ref):
    y = (jnp.dot(ya_ref[0], woa_ref[...], preferred_element_type=F32)
         + jnp.dot(yb_ref[0], wob_ref[...], preferred_element_type=F32))
    x1 = x_ref[0] + m2_ref[0] * y
    o_ref[0] = x1
    h_ref[...] = (_rms(x1, g_ref[...]) * (1.0 + m4_ref[0]) + m3_ref[0]).astype(BF16)
    for f in range(a_ref.shape[1] // MLP_TF):
        cols = slice(f * MLP_TF, (f + 1) * MLP_TF)
        a = jnp.maximum(jnp.dot(h_ref[...], w1_ref[:, cols], preferred_element_type=F32), 0.0)
        a_ref[:, cols] = (a * a).astype(BF16)
    o_ref[0] += m5_ref[0] * jnp.dot(a_ref[...], w2_ref[...], preferred_element_type=F32)


def _out_proj_mlp(x, ya, yb, w_out, g_mlp, mods, row0, w1, w2, layer, tm=512):
    G, M, D = x.shape
    assert M % tm == 0
    Wa, Wb = ya.shape[-1], yb.shape[-1]
    assert Wa == Wb and Wa + Wb == w_out.shape[1]
    FF = w1.shape[2]

    def resident(shape, row_block=0):
        return pl.BlockSpec((None,) + shape, lambda b, i: (layer, row_block, 0), pipeline_mode=pl.Buffered(1))

    return pl.pallas_call(
        _out_mlp_kernel,
        out_shape=jax.ShapeDtypeStruct((G, M, D), F32),
        grid=(G, M // tm),
        in_specs=[pl.BlockSpec((1, tm, D), lambda b, i: (b, i, 0)),
                  pl.BlockSpec((1, tm, Wa), lambda b, i: (b, i, 0)),
                  pl.BlockSpec((1, tm, Wb), lambda b, i: (b, i, 0)),
                  resident((Wa, D), 0), resident((Wb, D), 1),
                  pl.BlockSpec((1, D), lambda b, i: (0, 0)),
                  _mod_spec(row0, 2), _mod_spec(row0, 3), _mod_spec(row0, 4), _mod_spec(row0, 5),
                  resident((D, FF)), resident((FF, D))],
        out_specs=pl.BlockSpec((1, tm, D), lambda b, i: (b, i, 0)),
        scratch_shapes=[pltpu.VMEM((tm, D), BF16), pltpu.VMEM((tm, FF), BF16)],
        compiler_params=_cparams(("parallel", "arbitrary"), 52),
        name="out_proj_mlp",
    )(x, ya, yb, w_out, w_out, g_mlp.reshape(1, D), mods, mods, mods, mods, w1, w2)


def _nmm_mla_kernel(x_ref, g_ref, sh_ref, sc_ref, w_ref, tq1_ref, tq2_ref, tk1_ref, tka_ref, tkb_ref, qn_ref, kvn_ref,
                    wuq_ref, wuqr_ref, wuk_ref, wuv_ref, o_ref, q_out, k_out, vt_out):
    u = _modulated_proj(x_ref, g_ref, sh_ref, sc_ref, w_ref)
    o_ref[0] = u[:, EV_UX:].astype(BF16)
    qn = _rms(u[:, EV_UQ:EV_UQ + MLA_Q_LORA], qn_ref[...]).astype(BF16)
    kvn = _rms(u[:, EV_UKV:EV_UKV + MLA_KV_LORA], kvn_ref[...]).astype(BF16)
    q_all = jnp.dot(qn, wuq_ref[...], preferred_element_type=F32)
    q_rot = jnp.dot(qn, wuqr_ref[...], preferred_element_type=F32)
    k_all = jnp.dot(kvn, wuk_ref[...], preferred_element_type=F32)
    v_all = jnp.dot(kvn, wuv_ref[...], preferred_element_type=F32)
    ukr = u[:, EV_UKR:EV_UKR + HEAD_SLOT]
    quarter = MLA_ROPE // 4
    k_rot = (pltpu.roll(ukr, HEAD_SLOT - quarter, 1) * tka_ref[...] + pltpu.roll(ukr, quarter, 1) * tkb_ref[...])
    ukr_sq = jnp.sum(ukr * ukr, axis=-1, keepdims=True)
    inv_n = 1.0 / MLA_QK
    for h in range(MLA_HEADS):
        sl = slice(h * HEAD_SLOT, (h + 1) * HEAD_SLOT)
        qh = q_all[:, sl]
        rq = lax.rsqrt(jnp.sum(qh * qh, axis=-1, keepdims=True) * inv_n + EPS)
        q_out[0, :, sl] = (rq * (qh * tq1_ref[...] + q_rot[:, sl] * tq2_ref[...])).astype(BF16)
        kn = k_all[:, sl]
        rk = lax.rsqrt((jnp.sum(kn * kn, axis=-1, keepdims=True) + ukr_sq) * inv_n + EPS)
        k_out[0, :, sl] = (rk * ((kn + ukr) * tk1_ref[...] + k_rot)).astype(BF16)
        vt_out[0, sl, :] = _value_slot_t(v_all[:, sl], MLA_V)


def _rope_partner(t):
    quarter = MLA_ROPE // 4
    t4 = t.reshape(t.shape[:-1] + (2, 2, quarter))
    return jnp.stack([t4[..., 1, :], t4[..., 0, :]], axis=-2).reshape(t.shape)


def _rope_tables(n, q_gain, k_gain, use_rope):
    quarter = MLA_ROPE // 4
    if use_rope:
        pos = jnp.arange(n)
        inv = ROPE_BASE ** (-jnp.arange(0, MLA_ROPE // 2, 2, dtype=F32) / (MLA_ROPE // 2))
        ang_r = (pos // GRID_W).astype(F32)[:, None] * inv[None, :]
        ang_c = (pos % GRID_W).astype(F32)[:, None] * inv[None, :]
        ang = jnp.concatenate([ang_r, ang_r, ang_c, ang_c], axis=-1)
        cos, sin = jnp.cos(ang), jnp.sin(ang)
    else:
        cos, sin = jnp.ones((n, MLA_ROPE), F32), jnp.zeros((n, MLA_ROPE), F32)
    first = (np.arange(MLA_ROPE) % (2 * quarter)) < quarter
    pad = jnp.zeros((n, HEAD_SLOT - MLA_QK), F32)

    def slot(nope, rope):
        return jnp.concatenate([jnp.broadcast_to(nope, (n, MLA_NOPE)), rope, pad], axis=-1)

    zero = jnp.zeros((MLA_NOPE,), F32)
    qg, kg = q_gain * (MLA_QK ** -0.5 * LOG2_E), k_gain
    tq1 = slot(qg[:MLA_NOPE], cos * qg[MLA_NOPE:])
    tq2 = slot(zero, sin * _rope_partner(qg[MLA_NOPE:]))
    tk1 = slot(kg[:MLA_NOPE], cos * kg[MLA_NOPE:])
    ksin = sin * _rope_partner(kg[MLA_NOPE:])
    tka = slot(zero, jnp.where(first, -ksin, 0.0))
    tkb = slot(zero, jnp.where(first, 0.0, ksin))
    return tq1, tq2, tk1, tka, tkb


def _rope_partner_weight(wuq):
    k = wuq.shape[0]
    quarter = MLA_ROPE // 4
    w = wuq.reshape(k, MLA_HEADS, HEAD_SLOT)
    rope = w[:, :, MLA_NOPE:MLA_QK].reshape(k, MLA_HEADS, 2, 2, quarter)
    rot = jnp.stack([-rope[:, :, :, 1, :], rope[:, :, :, 0, :]], axis=3).reshape(k, MLA_HEADS, MLA_ROPE)
    out = jnp.concatenate([jnp.zeros_like(w[:, :, :MLA_NOPE]), rot, jnp.zeros_like(w[:, :, MLA_QK:])], axis=-1)
    return out.reshape(k, MLA_HEADS * HEAD_SLOT)


def _ev_in_proj(x, g, mods, row0, w, tables, qn, kvn, wuq, wuqr, wuk, wuv, tm=512):
    G, M, D = x.shape
    assert M % tm == 0
    HS = MLA_HEADS * HEAD_SLOT
    full = lambda shape: pl.BlockSpec(shape, lambda b, i: (0,) * len(shape))
    tab = pl.BlockSpec((tm, HEAD_SLOT), lambda b, i: (i, 0))
    rows = lambda width: pl.BlockSpec((1, tm, width), lambda b, i: (b, i, 0))
    return pl.pallas_call(
        _nmm_mla_kernel,
        out_shape=(jax.ShapeDtypeStruct((G, M, 2 * LRU_W), BF16), jax.ShapeDtypeStruct((G, M, HS), BF16),
                   jax.ShapeDtypeStruct((G, M, HS), BF16), jax.ShapeDtypeStruct((G, HS, M), BF16)),
        grid=(G, M // tm),
        in_specs=[rows(D), full((1, D)), _mod_spec(row0, 0), _mod_spec(row0, 1), full((D, EV_PAD)),
                  tab, tab, tab, tab, tab,
                  full((1, MLA_Q_LORA)), full((1, MLA_KV_LORA)),
                  full((MLA_Q_LORA, HS)), full((MLA_Q_LORA, HS)), full((MLA_KV_LORA, HS)), full((MLA_KV_LORA, HS))],
        out_specs=(rows(2 * LRU_W), rows(HS), rows(HS), pl.BlockSpec((1, HS, tm), lambda b, i: (b, 0, i))),
        compiler_params=_cparams(("parallel", "arbitrary"), 48),
        name="ev_in_proj_mla_prep",
    )(x, g.reshape(1, D), mods, mods, w, *tables, qn, kvn, wuq, wuqr, wuk, wuv)


def _dot_nt(a, b):
    return lax.dot_general(a, b, (((1,), (1,)), ((), ())), preferred_element_type=F32)


def _value_slot_t(v, ones_row):
    lane = lax.broadcasted_iota(jnp.int32, v.shape, 1)
    slot = jnp.where(lane < ones_row, v, jnp.where(lane == ones_row, 1.0, 0.0))
    return jnp.transpose(slot).astype(BF16)


ATTN_GROUP = 8


def _col_max(a, rows=8):
    parts = [a[r:r + rows] for r in range(0, a.shape[0], rows)]
    while len(parts) > 1:
        parts = [jnp.maximum(parts[i], parts[i + 1]) if i + 1 < len(parts) else parts[i]
                 for i in range(0, len(parts), 2)]
    return jnp.max(parts[0], axis=0, keepdims=True)


def _attend_t(heads):
    scores = []
    for q, keys, _, biases in heads:
        s = [_dot_nt(k, q) for k in keys]
        if biases is not None:
            s = [a if b is None else a + b for a, b in zip(s, biases)]
        scores.append(s)
    maxes = [functools.reduce(jnp.maximum, [_col_max(a) for a in s]) for s in scores]
    outs = []
    for (_, _, values_t, _), s, m in zip(heads, scores, maxes):
        out_t = None
        for a, vt in zip(s, values_t):
            part = jnp.dot(vt, jnp.exp2(a - m).astype(BF16), preferred_element_type=F32)
            out_t = part if out_t is None else out_t + part
        outs.append(out_t)
    return outs


def _pair_output(slots_t, dv):
    halves = [t[:dv] / t[dv:dv + 1] for t in slots_t]
    return jnp.transpose(jnp.concatenate(halves, axis=0))


def _mla_attn_kernel(*refs, with_latent):
    if with_latent:
        q_ref, kl_ref, vtl_ref, kc_ref, vtc_ref, o_ref = refs
    else:
        q_ref, kc_ref, vtc_ref, o_ref = refs
    for h0 in range(0, MLA_HEADS, ATTN_GROUP):
        heads = []
        for h in range(h0, h0 + ATTN_GROUP):
            sl = slice(h * HEAD_SLOT, (h + 1) * HEAD_SLOT)
            keys, values_t = [kc_ref[0, :, sl]], [vtc_ref[0, sl, :]]
            if with_latent:
                keys.append(kl_ref[0, :, sl])
                values_t.append(vtl_ref[0, sl, :])
            heads.append((q_ref[0, :, sl], keys, values_t, None))
        outs = _attend_t(heads)
        for i in range(0, ATTN_GROUP, 2):
            hp = (h0 + i) // 2
            o_ref[0, :, hp * LANE:(hp + 1) * LANE] = _pair_output(outs[i:i + 2], MLA_V).astype(BF16)


def _mla_attention(q, kl, vtl, kc, vtc, tq=512):
    B, M, HS = q.shape
    tq = min(tq, M)
    assert M % tq == 0
    HV = MLA_HEADS * MLA_V
    Nc = kc.shape[1]
    with_latent = kl is not None
    whole = lambda n, w: pl.BlockSpec((1, n, w), lambda b, i: (b, 0, 0))
    in_specs = [pl.BlockSpec((1, tq, HS), lambda b, i: (b, i, 0))]
    args = [q]
    if with_latent:
        in_specs += [whole(kl.shape[1], HS), whole(HS, kl.shape[1])]
        args += [kl, vtl]
    in_specs += [whole(Nc, HS), pl.BlockSpec((1, HS, Nc), lambda b, i: (0, 0, b))]
    args += [kc, vtc]
    return pl.pallas_call(
        functools.partial(_mla_attn_kernel, with_latent=with_latent),
        out_shape=jax.ShapeDtypeStruct((B, M, HV), BF16),
        grid=(B, M // tq),
        in_specs=in_specs,
        out_specs=pl.BlockSpec((1, tq, HV), lambda b, i: (b, i, 0)),
        compiler_params=_cparams(("parallel", "arbitrary"), 60),
        name="mla_attention",
    )(*args)


LRU_CW = 512
LRU_HALO = 16
LRU_TN = 256


def _gelu_tanh(x):
    return 0.5 * x * (1.0 + jnp.tanh(0.7978845608028654 * (x + 0.044715 * (x * x * x))))


def _scan_group(a, bv, h, reverse):
    row = lax.broadcasted_iota(jnp.int32, a.shape, 0)
    for s in (1, 2, 4):
        if reverse:
            keep = row < 8 - s
            shift = 8 - s
        else:
            keep = row >= s
            shift = s
        a_s = jnp.where(keep, pltpu.roll(a, shift, 0), 1.0)
        b_s = jnp.where(keep, pltpu.roll(bv, shift, 0), 0.0)
        bv = a * b_s + bv
        a = a * a_s
    hs = a * h + bv
    return hs, (hs[0:1, :] if reverse else hs[7:8, :])


def _lru_kernel(uxl_ref, ugl_ref, uxc_ref, ugc_ref, cw_ref, cb_ref, wg_ref, bg_ref, lam_ref, yl_ref, yc_ref,
                xpl_ref, xpc_ref, xcv_ref, af_ref, bf_ref, ab_ref, bb_ref):
    N, Nc = uxl_ref.shape[1], uxc_ref.shape[1]
    NT = N + Nc
    C = LRU_CW
    H = LRU_HALO

    def conv(src_ref, pad_ref, n, row0):
        pad_ref[0:H, :] = jnp.zeros((H, C), F32)
        pad_ref[H + n:H + n + H, :] = jnp.zeros((H, C), F32)
        pad_ref[H:H + n, :] = src_ref[0].astype(F32)
        y = cb_ref[...] + pad_ref[H - CONV_LEFT:H - CONV_LEFT + n, :] * cw_ref[0:1, :]
        for j in range(1, CONV_W):
            y = y + pad_ref[H - CONV_LEFT + j:H - CONV_LEFT + j + n, :] * cw_ref[j:j + 1, :]
        xcv_ref[row0:row0 + n, :] = y

    conv(uxc_ref, xpc_ref, Nc, 0)
    conv(uxl_ref, xpl_ref, N, Nc)

    lam = lam_ref[...]
    c_half = (-0.5 * LRU_C) * (jnp.maximum(-lam, 0.0) + jnp.log1p(jnp.exp(-jnp.abs(lam))))

    def coeff_chunk(i, carry):
        r0 = pl.multiple_of(i * LRU_TN, LRU_TN)
        x = xcv_ref[pl.ds(r0, LRU_TN), :]
        t = jnp.tanh(jnp.dot(x.astype(BF16), wg_ref[0], preferred_element_type=F32) + bg_ref[0])
        hx = 0.5 * x
        for d, (a_ref, b_ref) in enumerate(((af_ref, bf_ref), (ab_ref, bb_ref))):
            c = c_half[d:d + 1, :]
            a = jnp.exp(c * t[:, (2 * d) * C:(2 * d + 1) * C] + c)
            a_ref[pl.ds(r0, LRU_TN), :] = a
            gated_x = hx * t[:, (2 * d + 1) * C:(2 * d + 2) * C] + hx
            b_ref[pl.ds(r0, LRU_TN), :] = jnp.sqrt(1.0 - a * a) * gated_x
        return carry

    lax.fori_loop(0, NT // LRU_TN, coeff_chunk, 0)

    ngc, ngt = Nc // 8, NT // 8

    def scan_step(i, carry):
        hf, hb = carry
        rf = pl.multiple_of(i * 8, 8)
        rb = pl.multiple_of(jnp.where(i < ngc, ngc - 1 - i, ngt + ngc - 1 - i) * 8, 8)
        hs_f, hf = _scan_group(af_ref[pl.ds(rf, 8), :], bf_ref[pl.ds(rf, 8), :], hf, False)
        hs_b, hb = _scan_group(ab_ref[pl.ds(rb, 8), :], bb_ref[pl.ds(rb, 8), :], hb, True)
        bf_ref[pl.ds(rf, 8), :] = hs_f
        bb_ref[pl.ds(rb, 8), :] = hs_b
        return hf, hb

    zero = jnp.zeros((1, C), F32)
    lax.fori_loop(0, ngt, scan_step, (zero, zero), unroll=4)

    def out_chunk(i, carry):
        r0 = pl.multiple_of(i * LRU_TN, LRU_TN)
        hsum = bf_ref[pl.ds(Nc + r0, LRU_TN), :] + bb_ref[pl.ds(Nc + r0, LRU_TN), :]
        gate = _gelu_tanh(ugl_ref[0, pl.ds(r0, LRU_TN), :].astype(F32))
        yl_ref[0, pl.ds(r0, LRU_TN), :] = (hsum * gate).astype(BF16)
        return carry

    lax.fori_loop(0, N // LRU_TN, out_chunk, 0)
    yc_ref[0] = ((bf_ref[0:Nc, :] + bb_ref[0:Nc, :]) * _gelu_tanh(ugc_ref[0].astype(F32))).astype(BF16)


def _rglru(ul, uc, conv_w, conv_b, wg, bg, lam):
    B, N, _ = ul.shape
    Nc = uc.shape[1]
    C = LRU_CW
    nh = LRU_W // C
    NT = N + Nc
    col = lambda base: (lambda b, j: (b, 0, base // C + j))
    par = lambda rows: pl.BlockSpec((rows, C), lambda b, j: (0, j))
    return pl.pallas_call(
        _lru_kernel,
        out_shape=(jax.ShapeDtypeStruct((B, N, LRU_W), BF16), jax.ShapeDtypeStruct((B, Nc, LRU_W), BF16)),
        grid=(B, nh),
        in_specs=[pl.BlockSpec((1, N, C), col(0)), pl.BlockSpec((1, N, C), col(LRU_W)),
                  pl.BlockSpec((1, Nc, C), col(0)), pl.BlockSpec((1, Nc, C), col(LRU_W)),
                  par(CONV_W), par(1),
                  pl.BlockSpec((1, C, 4 * C), lambda b, j: (j, 0, 0)),
                  pl.BlockSpec((1, 1, 4 * C), lambda b, j: (j, 0, 0)),
                  par(2)],
        out_specs=(pl.BlockSpec((1, N, C), lambda b, j: (b, 0, j)),
                   pl.BlockSpec((1, Nc, C), lambda b, j: (b, 0, j))),
        scratch_shapes=[pltpu.VMEM((N + 2 * LRU_HALO, C), F32), pltpu.VMEM((Nc + 2 * LRU_HALO, C), F32),
                        pltpu.VMEM((NT, C), F32)] + [pltpu.VMEM((NT, C), F32)] * 4,
        compiler_params=_cparams(("parallel", "arbitrary"), 58),
        name="rglru",
    )(ul, ul, uc, uc, conv_w, conv_b.reshape(1, LRU_W), wg, bg, lam)


def _lru_gate_weights(w_a, b_a, w_x, b_x):
    C = LRU_CW
    nh = LRU_W // C
    kb = C // LRU_BS

    def dense(w):
        w = w.reshape(nh, kb, LRU_BS, LRU_BS)
        eye = jnp.eye(kb, dtype=w.dtype)
        return jnp.einsum('hkij,kl->hkilj', w, eye).reshape(nh, C, C)

    wg = jnp.concatenate([dense(w_a[0]), dense(w_x[0]), dense(w_a[1]), dense(w_x[1])], axis=-1)
    bg = jnp.stack([b_a[0], b_x[0], b_a[1], b_x[1]], axis=0).reshape(4, nh, C)
    bg = jnp.transpose(bg, (1, 0, 2)).reshape(nh, 1, 4 * C)
    return (0.5 * wg).astype(BF16), 0.5 * bg


GLA_SC = 256
GLA_UNROLL = 2
GLA_QK_W = GLA_HEADS * GLA_DK
GLA_V_W = GLA_HEADS * GLA_DV


def _split3(x):
    hi = x.astype(BF16)
    r1 = x - hi.astype(F32)
    mid = r1.astype(BF16)
    lo = (r1 - mid.astype(F32)).astype(BF16)
    return hi, mid, lo


def _gla_kernel(ql_ref, kl_ref, vl_ref, gl_ref, lrl_ref, kc_ref, vc_ref, lrc_ref, wa_ref, ba_ref, og_ref,
                y_ref, oacc_ref, st_ref):
    N, Nc = ql_ref.shape[1], kc_ref.shape[1]
    T = GLA_SC
    CH = GLA_CHUNK
    npair = GLA_HEADS // 2
    row = lax.broadcasted_iota(jnp.int32, (T, T), 0)
    colm = lax.broadcasted_iota(jnp.int32, (T, T), 1)
    same_chunk = (row // CH) == (colm // CH)
    causal = (same_chunk & (colm <= row), same_chunk & (colm >= row))
    tri = tuple(jnp.where(c, 1.0, 0.0).astype(BF16) for c in causal)
    lane_lo = lax.broadcasted_iota(jnp.int32, (T, LANE), 1) < GLA_DK
    srow = lax.broadcasted_iota(jnp.int32, (LANE, 2 * GLA_DV), 0)
    scol = lax.broadcasted_iota(jnp.int32, (LANE, 2 * GLA_DV), 1)
    state_mask = (srow < GLA_DK) == (scol < GLA_DV)
    qscale = GLA_DK ** -0.5

    def superchunks(streams, q_ref, k_ref, v_ref, lr_ref, with_output):
        nch = T // CH
        work = []
        for d, r0 in streams:
            z = jnp.dot(lr_ref[0, pl.ds(r0, T), :], wa_ref[d], preferred_element_type=F32) + ba_ref[d]
            work.append(dict(d=d, r0=r0, split=_split3(jax.nn.log_sigmoid(z) * (1.0 / GLA_TAU))))
        for w in work:
            d = w['d']
            cum = sum(jnp.dot(tri[d], t, preferred_element_type=F32) for t in w['split'])
            last = (CH - 1) if d == 0 else 0
            tot = [cum[c * CH + last:c * CH + last + 1, :] for c in range(nch)]
            tot_rows = jnp.concatenate([jnp.broadcast_to(t, (CH, GLA_QK_W)) for t in tot], axis=0)
            k = k_ref[0, pl.ds(w['r0'], T), :].astype(F32)
            w.update(cum=cum, tot=tot, k=k, v=v_ref[0, pl.ds(w['r0'], T), :],
                     k_dec=(k * jnp.exp(tot_rows - cum)).astype(BF16))
        for w in work:
            ds, dec = {}, {}
            for c in range(nch):
                rs = slice(c * CH, (c + 1) * CH)
                for p in range(npair):
                    kd = w['k_dec'][rs, p * LANE:(p + 1) * LANE]
                    vv = w['v'][rs, p * 2 * GLA_DV:(p + 1) * 2 * GLA_DV]
                    kv = lax.dot_general(kd, vv, (((0,), (0,)), ((), ())), preferred_element_type=F32)
                    ds[c, p] = jnp.where(state_mask, kv, 0.0)
                    dec_row = jnp.exp(w['tot'][c][:, p * LANE:(p + 1) * LANE])
                    dec_col = jnp.transpose(jnp.broadcast_to(dec_row, (LANE, LANE)))
                    dec[c, p] = jnp.concatenate([dec_col, dec_col], axis=1)
            w.update(ds=ds, dec=dec)
        for w in work:
            d = w['d']
            order = range(nch) if d == 0 else range(nch - 1, -1, -1)
            s_in = {}
            for p in range(npair):
                s = st_ref[d, p]
                for c in order:
                    s_in[c, p] = s.astype(BF16)
                    s = w['dec'][c, p] * s + w['ds'][c, p]
                st_ref[d, p] = s
            w.update(s_in=s_in)
        if not with_output:
            return
        for w in work:
            q = q_ref[0, pl.ds(w['r0'], T), :].astype(F32)
            w.update(q_dec=((q * qscale) * jnp.exp(w['cum'])).astype(BF16),
                     k_inv=(w['k'] * jnp.exp(-w['cum'])).astype(BF16))
        for w in work:
            d, r0 = w['d'], w['r0']
            for p in range(npair):
                qp = w['q_dec'][:, p * LANE:(p + 1) * LANE]
                kp = w['k_inv'][:, p * LANE:(p + 1) * LANE]
                o_inter = jnp.concatenate(
                    [jnp.dot(qp[c * CH:(c + 1) * CH], w['s_in'][c, p], preferred_element_type=F32)
                     for c in range(nch)], axis=0)
                for j in range(2):
                    h = 2 * p + j
                    qm = jnp.where(lane_lo if j == 0 else jnp.logical_not(lane_lo), qp, jnp.zeros_like(qp))
                    att = jnp.where(causal[d], _dot_nt(qm, kp), 0.0).astype(BF16)
                    o = (jnp.dot(att, w['v'][:, h * GLA_DV:(h + 1) * GLA_DV], preferred_element_type=F32)
                         + o_inter[:, j * GLA_DV:(j + 1) * GLA_DV])
                    oacc_ref[d, pl.ds(r0, T), h * GLA_DV:(h + 1) * GLA_DV] = o

    st_ref[...] = jnp.zeros_like(st_ref)
    ncs, nls = Nc // T, N // T

    def ctx_body(i, carry):
        streams = [(0, pl.multiple_of(i * T, T)), (1, pl.multiple_of((ncs - 1 - i) * T, T))]
        superchunks(streams, None, kc_ref, vc_ref, lrc_ref, False)
        return carry

    def lat_body(i, carry):
        streams = []
        for u in range(GLA_UNROLL):
            streams += [(0, pl.multiple_of((GLA_UNROLL * i + u) * T, T)),
                        (1, pl.multiple_of((nls - 1 - GLA_UNROLL * i - u) * T, T))]
        superchunks(streams, ql_ref, kl_ref, vl_ref, lrl_ref, True)
        return carry

    lax.fori_loop(0, ncs, ctx_body, 0)
    lax.fori_loop(0, nls // GLA_UNROLL, lat_body, 0)

    def fin(i, carry):
        r0 = pl.multiple_of(i * T, T)
        g = gl_ref[0, pl.ds(r0, T), :].astype(F32)
        for h in range(GLA_HEADS):
            sl = slice(h * GLA_DV, (h + 1) * GLA_DV)
            gh = g[:, sl]
            o = oacc_ref[0, pl.ds(r0, T), sl] + oacc_ref[1, pl.ds(r0, T), sl]
            y_ref[0, pl.ds(r0, T), sl] = (_rms(o, og_ref[...]) * (gh * jax.nn.sigmoid(gh))).astype(BF16)
        return carry

    lax.fori_loop(0, N // T, fin, 0)


def _gla(ul, uc, wa, ba, o_gain):
    B, N, _ = ul.shape
    Nc = uc.shape[1]
    blk = lambda n, w, base: pl.BlockSpec((1, n, w), lambda b: (b, 0, base // w))
    full = lambda shape: pl.BlockSpec(shape, lambda b: (0,) * len(shape))
    return pl.pallas_call(
        _gla_kernel,
        out_shape=jax.ShapeDtypeStruct((B, N, GLA_V_W), BF16),
        grid=(B,),
        in_specs=[blk(N, GLA_QK_W, OD_GQ), blk(N, GLA_QK_W, OD_GK), blk(N, GLA_V_W, OD_GV),
                  blk(N, GLA_V_W, OD_GG), blk(N, LANE, OD_GLR),
                  blk(Nc, GLA_QK_W, OD_GK), blk(Nc, GLA_V_W, OD_GV), blk(Nc, LANE, OD_GLR),
                  full((2, LANE, GLA_QK_W)), full((2, 1, GLA_QK_W)), full((1, GLA_DV))],
        out_specs=pl.BlockSpec((1, N, GLA_V_W), lambda b: (b, 0, 0)),
        scratch_shapes=[pltpu.VMEM((2, N, GLA_V_W), F32),
                        pltpu.VMEM((2, GLA_HEADS // 2, LANE, 2 * GLA_DV), F32)],
        compiler_params=_cparams(("parallel",), 48),
        name="gla",
    )(ul, ul, ul, ul, ul, uc, uc, uc, wa, ba, o_gain)


NA_W = NA_HEADS * NA_HD
NA_QROWS = 4
NA_KROWS = 12
NA_TQ = NA_QROWS * GRID_W
NA_TK = NA_KROWS * GRID_W
NA_NEG = -1e30
NA_BATCH_BLOCK = 4


def _na_kernel(q_ref, k0_ref, k1_ref, k2_ref, vt0_ref, vt1_ref, vt2_ref, kc_ref, vtc_ref, bias_ref, o_ref):
    lane_lo = lax.broadcasted_iota(jnp.int32, (NA_TQ, LANE), 1) < NA_HD
    k_refs = (k0_ref, k1_ref, k2_ref, kc_ref)
    nc = kc_ref.shape[1]
    for bb in range(q_ref.shape[0]):
        for h0 in range(0, NA_HEADS, ATTN_GROUP):
            heads = []
            for h in range(h0, h0 + ATTN_GROUP):
                sl = slice((h // 2) * LANE, (h // 2 + 1) * LANE)
                hs = slice(h * HEAD_SLOT, (h + 1) * HEAD_SLOT)
                qp = q_ref[bb, :, sl]
                qm = jnp.where(lane_lo if h % 2 == 0 else jnp.logical_not(lane_lo), qp, jnp.zeros_like(qp))
                biases = [bias_ref[0, h, t * NA_TQ:(t + 1) * NA_TQ, :] for t in range(3)] + [None]
                values_t = [r[bb, hs, :] for r in (vt0_ref, vt1_ref, vt2_ref)] + [vtc_ref[0, hs, bb * nc:(bb + 1) * nc]]
                heads.append((qm, [r[bb, :, sl] for r in k_refs], values_t, biases))
            outs = _attend_t(heads)
            for i in range(0, ATTN_GROUP, 2):
                hp = (h0 + i) // 2
                o_ref[bb, :, hp * LANE:(hp + 1) * LANE] = _pair_output(outs[i:i + 2], NA_HD).astype(BF16)


def _na_bias_tables(rpb, rows):
    H, ndi, ndj = rpb.shape
    rpb = rpb * LOG2_E
    W = GRID_W
    half = NA_WIN_COLS - 1
    zeros = jnp.zeros((H, ndi, W - 1 - half), rpb.dtype)
    vec = jnp.concatenate([rpb[:, :, half:], zeros, zeros, rpb[:, :, :half]], axis=-1)
    toep = jnp.tile(vec, (1, 1, W))[:, :, :W * (2 * W - 2)].reshape(H, ndi, W, 2 * W - 2)[..., :W]
    qc = np.arange(W)[:, None]
    kc = np.arange(W)[None, :]
    c0 = np.clip(qc - NA_WIN_COLS // 2, 0, W - NA_WIN_COLS)
    toep = jnp.where(jnp.asarray((kc >= c0) & (kc < c0 + NA_WIN_COLS)), toep, NA_NEG)
    toep = jnp.swapaxes(toep, 2, 3)
    nblk = rows // NA_QROWS
    plan = []
    for m in (0, 1, nblk - 1):
        kb = min(max(m - 1, 0), nblk - NA_KROWS // NA_QROWS)
        blocks = []
        for i in range(NA_KROWS):
            for a in range(NA_QROWS):
                kr, qr = kb * NA_QROWS + i, m * NA_QROWS + a
                r0 = min(max(qr - NA_WIN_ROWS // 2, 0), rows - NA_WIN_ROWS)
                blocks.append((i, a, kr - qr + NA_WIN_ROWS - 1 if r0 <= kr < r0 + NA_WIN_ROWS else None))
        plan.append(blocks)

    def expand_kernel(toep_ref, out_ref):
        for v, blocks in enumerate(plan):
            for i, a, di in blocks:
                blk = jnp.full((W, W), NA_NEG, F32) if di is None else toep_ref[0, di]
                out_ref[v, 0, i * W:(i + 1) * W, a * W:(a + 1) * W] = blk

    return pl.pallas_call(
        expand_kernel,
        out_shape=jax.ShapeDtypeStruct((len(plan), H, NA_TK, NA_TQ), F32),
        grid=(H,),
        in_specs=[pl.BlockSpec((1, ndi, W, W), lambda h: (h, 0, 0, 0))],
        out_specs=pl.BlockSpec((len(plan), 1, NA_TK, NA_TQ), lambda h: (0, h, 0, 0)),
        compiler_params=_cparams(("arbitrary",), 32),
        name="na_bias_expand",
    )(toep)


def _natten(ul, vt, uc, vtc, bias):
    B, N, _ = ul.shape
    Nc = uc.shape[1]
    nblk = N // NA_TQ
    kmax = nblk - NA_KROWS // NA_QROWS
    VS = NA_HEADS * HEAD_SLOT
    qcol, kcol = OD_NQ // NA_W, OD_NK // NA_W

    bb = np.gcd(B, NA_BATCH_BLOCK).item()

    def kspec(t):
        return pl.BlockSpec((bb, NA_TQ, NA_W), lambda m, b: (b, jnp.clip(m - 1, 0, kmax) + t, kcol))

    def vspec(t):
        return pl.BlockSpec((bb, VS, NA_TQ), lambda m, b: (b, 0, jnp.clip(m - 1, 0, kmax) + t))

    return pl.pallas_call(
        _na_kernel,
        out_shape=jax.ShapeDtypeStruct((B, N, NA_W), BF16),
        grid=(nblk, B // bb),
        in_specs=[pl.BlockSpec((bb, NA_TQ, NA_W), lambda m, b: (b, m, qcol)),
                  kspec(0), kspec(1), kspec(2), vspec(0), vspec(1), vspec(2),
                  pl.BlockSpec((bb, Nc, NA_W), lambda m, b: (b, 0, kcol)),
                  pl.BlockSpec((1, VS, bb * Nc), lambda m, b: (0, 0, b)),
                  pl.BlockSpec((1, NA_HEADS, NA_TK, NA_TQ),
                               lambda m, b: (jnp.where(m == 0, 0, jnp.where(m == nblk - 1, 2, 1)), 0, 0, 0))],
        out_specs=pl.BlockSpec((bb, NA_TQ, NA_W), lambda m, b: (b, m, 0)),
        compiler_params=_cparams(("arbitrary", "arbitrary"), 48),
        name="natten",
    )(ul, ul, ul, ul, vt, vt, vt, uc, vtc, bias)


def _place(w, layout, total):
    pieces, pos = [], 0
    for src, width, dst in sorted(layout, key=lambda item: item[2]):
        if dst > pos:
            pieces.append(jnp.zeros((w.shape[0], dst - pos), w.dtype))
        pieces.append(w[:, src:src + width])
        pos = dst + width
    if pos < total:
        pieces.append(jnp.zeros((w.shape[0], total - pos), w.dtype))
    return jnp.concatenate(pieces, axis=1)


def _ev_in_weight(w):
    lay = [(0, 256, EV_UQ), (256, 128, EV_UKV), (384, MLA_ROPE, EV_UKR + MLA_NOPE),
           (416, LRU_W, EV_UX), (416 + LRU_W, LRU_W, EV_UG)]
    return _place(w, lay, EV_PAD).astype(BF16)


def _od_in_weight(w):
    src = np.cumsum([0, 256, 256, 512, 512, 2 * GLA_LR, 512, 512, 512])
    dst = [OD_GQ, OD_GK, OD_GV, OD_GG, OD_GLR, OD_NQ, OD_NK, OD_NV]
    lay = [(int(src[i]), int(src[i + 1] - src[i]), dst[i]) for i in range(8)]
    return _place(w, lay, OD_PAD).astype(BF16)


def _pad_heads(w, heads, width):
    k = w.shape[0]
    return jnp.pad(w.reshape(k, heads, width), ((0, 0), (0, 0), (0, HEAD_SLOT - width))).reshape(k, heads * HEAD_SLOT)


def kernel(x, c, ctx, c_ctx, ada_w, ada_b, norm_mix, norm_mlp, w_out, mlp_w1, mlp_w2,
           ev_w_in, mla_q_norm, mla_w_uq, mla_kv_norm, mla_w_ukv, mla_q_gain, mla_k_gain,
           lru_conv_w, lru_conv_b, lru_w_a, lru_b_a, lru_w_x, lru_b_x, lru_lam,
           od_w_in, gla_w_a, gla_b_a, gla_o_gain, na_q_gain, na_k_gain, na_rpb):
    B, N, D = x.shape
    Nc = ctx.shape[1]
    depth = ada_w.shape[0]

    R = -(-(B + 1) // 8) * 8
    cc = jnp.concatenate([c, c_ctx[None], jnp.zeros((R - B - 1, D), c.dtype)], axis=0)
    mods = _modulation(cc, ada_w, ada_b).reshape(depth * R, 1, 6 * D)

    w1, w2, wo = mlp_w1.astype(BF16), mlp_w2.astype(BF16), w_out.astype(BF16)
    xl = x
    xc = ctx.reshape(1, B * Nc, D)
    for l in range(depth):
        last = l == depth - 1
        j = l // 2
        row_l, row_c = l * R, l * R + B
        if l % 2 == 0:
            w_in = _ev_in_weight(ev_w_in[j])
            wuq = _pad_heads(mla_w_uq[j], MLA_HEADS, MLA_QK)
            wukv = mla_w_ukv[j].reshape(MLA_KV_LORA, MLA_HEADS, MLA_NOPE + MLA_V)
            wuk = _pad_heads(wukv[:, :, :MLA_NOPE].reshape(MLA_KV_LORA, -1), MLA_HEADS, MLA_NOPE).astype(BF16)
            wuv = _pad_heads(wukv[:, :, MLA_NOPE:].reshape(MLA_KV_LORA, -1), MLA_HEADS, MLA_V).astype(BF16)
            proj = functools.partial(_ev_in_proj, g=norm_mix[l], mods=mods, w=w_in, qn=mla_q_norm[j].reshape(1, -1),
                                     kvn=mla_kv_norm[j].reshape(1, -1), wuq=wuq.astype(BF16),
                                     wuqr=_rope_partner_weight(wuq).astype(BF16), wuk=wuk, wuv=wuv)
            ul, ql, kl, vl = proj(xl, row0=row_l, tables=_rope_tables(N, mla_q_gain[j], mla_k_gain[j], True))
            uc, qc, kc, vc = proj(xc, row0=row_c, tables=_rope_tables(B * Nc, mla_q_gain[j], mla_k_gain[j], False))
            uc, qc, kc = (t.reshape(B, Nc, -1) for t in (uc, qc, kc))
            ya_l = _mla_attention(ql, kl, vl, kc, vc)
            ya_c = None if last else _mla_attention(qc, None, None, kc, vc)
            wg, bg = _lru_gate_weights(lru_w_a[j], lru_b_a[j], lru_w_x[j], lru_b_x[j])
            yb_l, yb_c = _rglru(ul, uc, lru_conv_w[j], lru_conv_b[j], wg, bg, lru_lam[j])
        else:
            w_in = _od_in_weight(od_w_in[j])
            na_gains = (jnp.tile(na_q_gain[j] * (NA_HD ** -0.5 * LOG2_E), NA_HEADS).reshape(1, NA_HEADS * NA_HD),
                        jnp.tile(na_k_gain[j], NA_HEADS).reshape(1, NA_HEADS * NA_HD))
            ul, nvt = _norm_mod_matmul(xl, norm_mix[l], mods, row_l, w_in, na_gains)
            uc, nvtc = _norm_mod_matmul(xc, norm_mix[l], mods, row_c, w_in, na_gains)
            uc = uc.reshape(B, Nc, OD_PAD)
            if not last:
                raise NotImplementedError("context outputs of the odd-layer mixers are only needed when depth > 2")
            wa = jnp.stack([jnp.pad(gla_w_a[j, d], ((d * GLA_LR, LANE - (d + 1) * GLA_LR), (0, 0))) for d in range(2)])
            ya_l = _gla(ul, uc, wa.astype(BF16), gla_b_a[j].reshape(2, 1, GLA_QK_W), gla_o_gain[j].reshape(1, GLA_DV))
            ya_c = None
            yb_l = _natten(ul, nvt, uc, nvtc, _na_bias_tables(na_rpb[j], N // GRID_W))
            yb_c = None
        xl = _out_proj_mlp(xl, ya_l, yb_l, wo, norm_mlp[l], mods, row_l, w1, w2, l)
        if not last:
            xc = _out_proj_mlp(xc, ya_c.reshape(1, B * Nc, -1), yb_c.reshape(1, B * Nc, -1), wo, norm_mlp[l], mods,
                               row_c, w1, w2, l)
    return xl
```

```python
import functools

import numpy as np
import jax
import jax.numpy as jnp
from jax import lax
from jax.experimental import pallas as pl
from jax.experimental.pallas import tpu as pltpu

F32 = jnp.float32
BF16 = jnp.bfloat16

D_MODEL = 1024
GRID_W = 64
EPS = 1e-6
LOG2_E = 1.4426950408889634
ROPE_BASE = 10000.0

MLA_HEADS = 8
MLA_NOPE = 64
MLA_ROPE = 32
MLA_QK = MLA_NOPE + MLA_ROPE
MLA_V = 64
MLA_Q_LORA = 256
MLA_KV_LORA = 128

LRU_W = 512
LRU_BLOCKS = 8
LRU_BS = LRU_W // LRU_BLOCKS
LRU_C = 8.0
CONV_W = 4
CONV_LEFT = 2

GLA_HEADS = 4
GLA_DK = 64
GLA_DV = 128
GLA_LR = 16
GLA_TAU = 16.0
GLA_CHUNK = 64

NA_HD = 64
NA_HEADS = 8
NA_WIN_ROWS = 8
NA_WIN_COLS = 16

LANE = 128
HEAD_SLOT = 128

EV_UQ, EV_UKV, EV_UKR, EV_UX, EV_UG = 0, 256, 384, 512, 1024
EV_PAD = 1536
OD_NQ, OD_NK, OD_NV, OD_GQ, OD_GK, OD_GV, OD_GG, OD_GLR = 0, 512, 1024, 1536, 1792, 2048, 2560, 3072
OD_PAD = 3200


def _cparams(semantics, vmem_mib):
    return pltpu.CompilerParams(dimension_semantics=semantics, vmem_limit_bytes=vmem_mib << 20)


def _rms(x, g):
    return x * lax.rsqrt(jnp.mean(x * x, axis=-1, keepdims=True) + EPS) * g


def _ada_kernel(c_ref, w_ref, b_ref, o_ref):
    cv = c_ref[...]
    s = cv * jax.nn.sigmoid(cv)
    o_ref[0] = jnp.dot(s.astype(BF16), w_ref[0].astype(BF16), preferred_element_type=F32) + b_ref[0]


def _modulation(cc, ada_w, ada_b):
    L, D, D6 = ada_w.shape
    R = cc.shape[0]
    tn = 1536
    return pl.pallas_call(
        _ada_kernel,
        out_shape=jax.ShapeDtypeStruct((L, R, D6), F32),
        grid=(L, D6 // tn),
        in_specs=[pl.BlockSpec((R, D), lambda l, j: (0, 0)),
                  pl.BlockSpec((1, D, tn), lambda l, j: (l, 0, j)),
                  pl.BlockSpec((1, 1, tn), lambda l, j: (l, 0, j))],
        out_specs=pl.BlockSpec((1, R, tn), lambda l, j: (l, 0, j)),
        compiler_params=_cparams(("arbitrary", "arbitrary"), 40),
        name="adaln_modulation",
    )(cc, ada_w, ada_b.reshape(L, 1, D6))


def _mod_spec(row0, k):
    return pl.BlockSpec((1, 1, D_MODEL), lambda g, i: (row0 + g, 0, k))


def _modulated_proj(x_ref, g_ref, sh_ref, sc_ref, w_ref):
    h = _rms(x_ref[0], g_ref[...]) * (1.0 + sc_ref[0]) + sh_ref[0]
    return jnp.dot(h.astype(BF16), w_ref[...], preferred_element_type=F32)


def _nmm_kernel(x_ref, g_ref, sh_ref, sc_ref, w_ref, o_ref):
    o_ref[0] = _modulated_proj(x_ref, g_ref, sh_ref, sc_ref, w_ref).astype(BF16)


def _nmm_na_kernel(x_ref, g_ref, sh_ref, sc_ref, w_ref, qg_ref, kg_ref, o_ref, vt_ref):
    u = _modulated_proj(x_ref, g_ref, sh_ref, sc_ref, w_ref)
    plain = OD_NK + NA_HEADS * NA_HD
    o_ref[0, :, plain:] = u[:, plain:].astype(BF16)
    lane_lo = lax.broadcasted_iota(jnp.int32, (u.shape[0], LANE), 1) < NA_HD
    for base, gain_ref in ((OD_NQ, qg_ref), (OD_NK, kg_ref)):
        for p in range(NA_HEADS // 2):
            t = u[:, base + p * LANE:base + (p + 1) * LANE]
            sq = t * t
            s_lo = jnp.sum(jnp.where(lane_lo, sq, 0.0), axis=-1, keepdims=True)
            s_hi = jnp.sum(jnp.where(lane_lo, 0.0, sq), axis=-1, keepdims=True)
            ms = jnp.where(lane_lo, s_lo, s_hi) * (1.0 / NA_HD)
            gain = gain_ref[:, p * LANE:(p + 1) * LANE]
            o_ref[0, :, base + p * LANE:base + (p + 1) * LANE] = (t * lax.rsqrt(ms + EPS) * gain).astype(BF16)
    for p in range(NA_HEADS // 2):
        pair = u[:, OD_NV + p * LANE:OD_NV + (p + 1) * LANE]
        vt_ref[0, (2 * p) * HEAD_SLOT:(2 * p + 1) * HEAD_SLOT, :] = _value_slot_t(pair, NA_HD)
        vt_ref[0, (2 * p + 1) * HEAD_SLOT:(2 * p + 2) * HEAD_SLOT, :] = _value_slot_t(pltpu.roll(pair, NA_HD, 1), NA_HD)


def _norm_mod_matmul(x, g, mods, row0, w, na_gains=None, tm=512):
    G, M, D = x.shape
    assert M % tm == 0
    Nout = w.shape[1]
    in_specs = [pl.BlockSpec((1, tm, D), lambda b, i: (b, i, 0)),
                pl.BlockSpec((1, D), lambda b, i: (0, 0)),
                _mod_spec(row0, 0), _mod_spec(row0, 1),
                pl.BlockSpec((D, Nout), lambda b, i: (0, 0))]
    u_shape = jax.ShapeDtypeStruct((G, M, Nout), BF16)
    u_spec = pl.BlockSpec((1, tm, Nout), lambda b, i: (b, i, 0))
    if na_gains is None:
        kern, args, out_shape, out_specs = _nmm_kernel, (), u_shape, u_spec
    else:
        VS = NA_HEADS * HEAD_SLOT
        kern, args = _nmm_na_kernel, tuple(na_gains)
        in_specs += [pl.BlockSpec((1, NA_HEADS * NA_HD), lambda b, i: (0, 0))] * 2
        out_shape = (u_shape, jax.ShapeDtypeStruct((G, VS, M), BF16))
        out_specs = (u_spec, pl.BlockSpec((1, VS, tm), lambda b, i: (b, 0, i)))
    return pl.pallas_call(
        kern,
        out_shape=out_shape,
        grid=(G, M // tm),
        in_specs=in_specs,
        out_specs=out_specs,
        compiler_params=_cparams(("parallel", "arbitrary"), 48),
        name="norm_mod_in_proj",
    )(x, g.reshape(1, D), mods, mods, w, *args)


MLP_TF = 1024


def _out_mlp_kernel(x_ref, ya_ref, yb_ref, woa_ref, wob_ref, g_ref, m2_ref, m3_ref, m4_ref, m5_ref,
                    w1_ref, w2_ref, o_ref, h_ref, a_ref):
    y = (jnp.dot(ya_ref[0], woa_ref[...], preferred_element_type=F32)
         + jnp.dot(yb_ref[0], wob_ref[...], preferred_element_type=F32))
    x1 = x_ref[0] + m2_ref[0] * y
    o_ref[0] = x1
    h_ref[...] = (_rms(x1, g_ref[...]) * (1.0 + m4_ref[0]) + m3_ref[0]).astype(BF16)
    for f in range(a_ref.shape[1] // MLP_TF):
        cols = slice(f * MLP_TF, (f + 1) * MLP_TF)
        a = jnp.maximum(jnp.dot(h_ref[...], w1_ref[:, cols], preferred_element_type=F32), 0.0)
        a_ref[:, cols] = (a * a).astype(BF16)
    o_ref[0] += m5_ref[0] * jnp.dot(a_ref[...], w2_ref[...], preferred_element_type=F32)


def _out_proj_mlp(x, ya, yb, w_out, g_mlp, mods, row0, w1, w2, layer, tm=512):
    G, M, D = x.shape
    assert M % tm == 0
    Wa, Wb = ya.shape[-1], yb.shape[-1]
    assert Wa == Wb and Wa + Wb == w_out.shape[1]
    FF = w1.shape[2]

    def resident(shape, row_block=0):
        return pl.BlockSpec((None,) + shape, lambda b, i: (layer, row_block, 0), pipeline_mode=pl.Buffered(1))

    return pl.pallas_call(
        _out_mlp_kernel,
        out_shape=jax.ShapeDtypeStruct((G, M, D), F32),
        grid=(G, M // tm),
        in_specs=[pl.BlockSpec((1, tm, D), lambda b, i: (b, i, 0)),
                  pl.BlockSpec((1, tm, Wa), lambda b, i: (b, i, 0)),
                  pl.BlockSpec((1, tm, Wb), lambda b, i: (b, i, 0)),
                  resident((Wa, D), 0), resident((Wb, D), 1),
                  pl.BlockSpec((1, D), lambda b, i: (0, 0)),
                  _mod_spec(row0, 2), _mod_spec(row0, 3), _mod_spec(row0, 4), _mod_spec(row0, 5),
                  resident((D, FF)), resident((FF, D))],
        out_specs=pl.BlockSpec((1, tm, D), lambda b, i: (b, i, 0)),
        scratch_shapes=[pltpu.VMEM((tm, D), BF16), pltpu.VMEM((tm, FF), BF16)],
        compiler_params=_cparams(("parallel", "arbitrary"), 52),
        name="out_proj_mlp",
    )(x, ya, yb, w_out, w_out, g_mlp.reshape(1, D), mods, mods, mods, mods, w1, w2)


def _nmm_mla_kernel(x_ref, g_ref, sh_ref, sc_ref, w_ref, tq1_ref, tq2_ref, tk1_ref, tka_ref, tkb_ref, qn_ref, kvn_ref,
                    wuq_ref, wuqr_ref, wuk_ref, wuv_ref, o_ref, q_out, k_out, vt_out):
    xn = (_rms(x_ref[0], g_ref[...]) * (1.0 + sc_ref[0]) + sh_ref[0]).astype(BF16)
    u = jnp.dot(xn, w_ref[:, :EV_UX], preferred_element_type=F32)
    qn = _rms(u[:, EV_UQ:EV_UQ + MLA_Q_LORA], qn_ref[...]).astype(BF16)
    kvn = _rms(u[:, EV_UKV:EV_UKV + MLA_KV_LORA], kvn_ref[...]).astype(BF16)
    q_all = jnp.dot(qn, wuq_ref[...], preferred_element_type=F32)
    q_rot = jnp.dot(qn, wuqr_ref[...], preferred_element_type=F32)
    k_all = jnp.dot(kvn, wuk_ref[...], preferred_element_type=F32)
    v_all = jnp.dot(kvn, wuv_ref[...], preferred_element_type=F32)
    ukr = u[:, EV_UKR:EV_UKR + HEAD_SLOT]
    quarter = MLA_ROPE // 4
    k_rot = (pltpu.roll(ukr, HEAD_SLOT - quarter, 1) * tka_ref[...] + pltpu.roll(ukr, quarter, 1) * tkb_ref[...])
    ukr_sq = jnp.sum(ukr * ukr, axis=-1, keepdims=True)
    inv_n = 1.0 / MLA_QK
    for h in range(MLA_HEADS):
        sl = slice(h * HEAD_SLOT, (h + 1) * HEAD_SLOT)
        qh = q_all[:, sl]
        rq = lax.rsqrt(jnp.sum(qh * qh, axis=-1, keepdims=True) * inv_n + EPS)
        q_out[0, :, sl] = (rq * (qh * tq1_ref[...] + q_rot[:, sl] * tq2_ref[...])).astype(BF16)
        kn = k_all[:, sl]
        rk = lax.rsqrt((jnp.sum(kn * kn, axis=-1, keepdims=True) + ukr_sq) * inv_n + EPS)
        k_out[0, :, sl] = (rk * ((kn + ukr) * tk1_ref[...] + k_rot)).astype(BF16)
        vt_out[0, sl, :] = _value_slot_t(v_all[:, sl], MLA_V)
    o_ref[0] = jnp.dot(xn, w_ref[:, EV_UX:], preferred_element_type=F32).astype(BF16)


def _rope_partner(t):
    quarter = MLA_ROPE // 4
    t4 = t.reshape(t.shape[:-1] + (2, 2, quarter))
    return jnp.stack([t4[..., 1, :], t4[..., 0, :]], axis=-2).reshape(t.shape)


def _rope_tables(n, q_gain, k_gain, use_rope):
    quarter = MLA_ROPE // 4
    if use_rope:
        pos = jnp.arange(n)
        inv = ROPE_BASE ** (-jnp.arange(0, MLA_ROPE // 2, 2, dtype=F32) / (MLA_ROPE // 2))
        ang_r = (pos // GRID_W).astype(F32)[:, None] * inv[None, :]
        ang_c = (pos % GRID_W).astype(F32)[:, None] * inv[None, :]
        ang = jnp.concatenate([ang_r, ang_r, ang_c, ang_c], axis=-1)
        cos, sin = jnp.cos(ang), jnp.sin(ang)
    else:
        cos, sin = jnp.ones((n, MLA_ROPE), F32), jnp.zeros((n, MLA_ROPE), F32)
    first = (np.arange(MLA_ROPE) % (2 * quarter)) < quarter
    pad = jnp.zeros((n, HEAD_SLOT - MLA_QK), F32)

    def slot(nope, rope):
        return jnp.concatenate([jnp.broadcast_to(nope, (n, MLA_NOPE)), rope, pad], axis=-1)

    zero = jnp.zeros((MLA_NOPE,), F32)
    qg, kg = q_gain * (MLA_QK ** -0.5 * LOG2_E), k_gain
    tq1 = slot(qg[:MLA_NOPE], cos * qg[MLA_NOPE:])
    tq2 = slot(zero, sin * _rope_partner(qg[MLA_NOPE:]))
    tk1 = slot(kg[:MLA_NOPE], cos * kg[MLA_NOPE:])
    ksin = sin * _rope_partner(kg[MLA_NOPE:])
    tka = slot(zero, jnp.where(first, -ksin, 0.0))
    tkb = slot(zero, jnp.where(first, 0.0, ksin))
    return tq1, tq2, tk1, tka, tkb


def _rope_partner_weight(wuq):
    k = wuq.shape[0]
    quarter = MLA_ROPE // 4
    w = wuq.reshape(k, MLA_HEADS, HEAD_SLOT)
    rope = w[:, :, MLA_NOPE:MLA_QK].reshape(k, MLA_HEADS, 2, 2, quarter)
    rot = jnp.stack([-rope[:, :, :, 1, :], rope[:, :, :, 0, :]], axis=3).reshape(k, MLA_HEADS, MLA_ROPE)
    out = jnp.concatenate([jnp.zeros_like(w[:, :, :MLA_NOPE]), rot, jnp.zeros_like(w[:, :, MLA_QK:])], axis=-1)
    return out.reshape(k, MLA_HEADS * HEAD_SLOT)


def _ev_in_proj(x, g, mods, row0, w, tables, qn, kvn, wuq, wuqr, wuk, wuv, tm=512):
    G, M, D = x.shape
    assert M % tm == 0
    HS = MLA_HEADS * HEAD_SLOT
    full = lambda shape: pl.BlockSpec(shape, lambda b, i: (0,) * len(shape))
    tab = pl.BlockSpec((tm, HEAD_SLOT), lambda b, i: (i, 0))
    rows = lambda width: pl.BlockSpec((1, tm, width), lambda b, i: (b, i, 0))
    return pl.pallas_call(
        _nmm_mla_kernel,
        out_shape=(jax.ShapeDtypeStruct((G, M, 2 * LRU_W), BF16), jax.ShapeDtypeStruct((G, M, HS), BF16),
                   jax.ShapeDtypeStruct((G, M, HS), BF16), jax.ShapeDtypeStruct((G, HS, M), BF16)),
        grid=(G, M // tm),
        in_specs=[rows(D), full((1, D)), _mod_spec(row0, 0), _mod_spec(row0, 1), full((D, EV_PAD)),
                  tab, tab, tab, tab, tab,
                  full((1, MLA_Q_LORA)), full((1, MLA_KV_LORA)),
                  full((MLA_Q_LORA, HS)), full((MLA_Q_LORA, HS)), full((MLA_KV_LORA, HS)), full((MLA_KV_LORA, HS))],
        out_specs=(rows(2 * LRU_W), rows(HS), rows(HS), pl.BlockSpec((1, HS, tm), lambda b, i: (b, 0, i))),
        compiler_params=_cparams(("parallel", "arbitrary"), 48),
        name="ev_in_proj_mla_prep",
    )(x, g.reshape(1, D), mods, mods, w, *tables, qn, kvn, wuq, wuqr, wuk, wuv)


def _dot_nt(a, b):
    return lax.dot_general(a, b, (((1,), (1,)), ((), ())), preferred_element_type=F32)


def _value_slot_t(v, ones_row):
    lane = lax.broadcasted_iota(jnp.int32, v.shape, 1)
    slot = jnp.where(lane < ones_row, v, jnp.where(lane == ones_row, 1.0, 0.0))
    return jnp.transpose(slot).astype(BF16)


ATTN_GROUP = 8


def _col_max(a, rows=8):
    parts = [a[r:r + rows] for r in range(0, a.shape[0], rows)]
    while len(parts) > 1:
        parts = [jnp.maximum(parts[i], parts[i + 1]) if i + 1 < len(parts) else parts[i]
                 for i in range(0, len(parts), 2)]
    return jnp.max(parts[0], axis=0, keepdims=True)


def _attend_t(heads):
    scores = []
    for q, keys, _, biases in heads:
        s = [_dot_nt(k, q) for k in keys]
        if biases is not None:
            s = [a if b is None else a + b for a, b in zip(s, biases)]
        scores.append(s)
    maxes = [functools.reduce(jnp.maximum, [_col_max(a) for a in s]) for s in scores]
    outs = []
    for (_, _, values_t, _), s, m in zip(heads, scores, maxes):
        out_t = None
        for a, vt in zip(s, values_t):
            part = jnp.dot(vt, jnp.exp2(a - m).astype(BF16), preferred_element_type=F32)
            out_t = part if out_t is None else out_t + part
        outs.append(out_t)
    return outs


def _pair_output(slots_t, dv):
    halves = [t[:dv] / t[dv:dv + 1] for t in slots_t]
    return jnp.transpose(jnp.concatenate(halves, axis=0))


def _mla_attn_kernel(*refs, with_latent):
    if with_latent:
        q_ref, kl_ref, vtl_ref, kc_ref, vtc_ref, o_ref = refs
    else:
        q_ref, kc_ref, vtc_ref, o_ref = refs
    for h0 in range(0, MLA_HEADS, ATTN_GROUP):
        heads = []
        for h in range(h0, h0 + ATTN_GROUP):
            sl = slice(h * HEAD_SLOT, (h + 1) * HEAD_SLOT)
            keys, values_t = [kc_ref[0, :, sl]], [vtc_ref[0, sl, :]]
            if with_latent:
                keys.append(kl_ref[0, :, sl])
                values_t.append(vtl_ref[0, sl, :])
            heads.append((q_ref[0, :, sl], keys, values_t, None))
        outs = _attend_t(heads)
        for i in range(0, ATTN_GROUP, 2):
            hp = (h0 + i) // 2
            o_ref[0, :, hp * LANE:(hp + 1) * LANE] = _pair_output(outs[i:i + 2], MLA_V).astype(BF16)


def _mla_attention(q, kl, vtl, kc, vtc, tq=512):
    B, M, HS = q.shape
    tq = min(tq, M)
    assert M % tq == 0
    HV = MLA_HEADS * MLA_V
    Nc = kc.shape[1]
    with_latent = kl is not None
    whole = lambda n, w: pl.BlockSpec((1, n, w), lambda b, i: (b, 0, 0))
    in_specs = [pl.BlockSpec((1, tq, HS), lambda b, i: (b, i, 0))]
    args = [q]
    if with_latent:
        in_specs += [whole(kl.shape[1], HS), whole(HS, kl.shape[1])]
        args += [kl, vtl]
    in_specs += [whole(Nc, HS), pl.BlockSpec((1, HS, Nc), lambda b, i: (0, 0, b))]
    args += [kc, vtc]
    return pl.pallas_call(
        functools.partial(_mla_attn_kernel, with_latent=with_latent),
        out_shape=jax.ShapeDtypeStruct((B, M, HV), BF16),
        grid=(B, M // tq),
        in_specs=in_specs,
        out_specs=pl.BlockSpec((1, tq, HV), lambda b, i: (b, i, 0)),
        compiler_params=_cparams(("parallel", "arbitrary"), 60),
        name="mla_attention",
    )(*args)


LRU_CW = 512
LRU_HALO = 16
LRU_TN = 256


def _gelu_tanh(x):
    return 0.5 * x * (1.0 + jnp.tanh(0.7978845608028654 * (x + 0.044715 * (x * x * x))))


def _scan_group(a, bv, h, reverse):
    row = lax.broadcasted_iota(jnp.int32, a.shape, 0)
    for s in (1, 2, 4):
        if reverse:
            keep = row < 8 - s
            shift = 8 - s
        else:
            keep = row >= s
            shift = s
        a_s = jnp.where(keep, pltpu.roll(a, shift, 0), 1.0)
        b_s = jnp.where(keep, pltpu.roll(bv, shift, 0), 0.0)
        bv = a * b_s + bv
        a = a * a_s
    hs = a * h + bv
    return hs, (hs[0:1, :] if reverse else hs[7:8, :])


def _lru_kernel(uxl_ref, ugl_ref, uxc_ref, ugc_ref, cw_ref, cb_ref, wg_ref, bg_ref, lam_ref, yl_ref, yc_ref,
                xpl_ref, xpc_ref, xcv_ref, af_ref, bf_ref, ab_ref, bb_ref):
    N, Nc = uxl_ref.shape[1], uxc_ref.shape[1]
    NT = N + Nc
    C = LRU_CW
    H = LRU_HALO

    def conv(src_ref, pad_ref, n, row0):
        pad_ref[0:H, :] = jnp.zeros((H, C), F32)
        pad_ref[H + n:H + n + H, :] = jnp.zeros((H, C), F32)
        pad_ref[H:H + n, :] = src_ref[0].astype(F32)
        y = cb_ref[...] + pad_ref[H - CONV_LEFT:H - CONV_LEFT + n, :] * cw_ref[0:1, :]
        for j in range(1, CONV_W):
            y = y + pad_ref[H - CONV_LEFT + j:H - CONV_LEFT + j + n, :] * cw_ref[j:j + 1, :]
        xcv_ref[row0:row0 + n, :] = y

    conv(uxc_ref, xpc_ref, Nc, 0)
    conv(uxl_ref, xpl_ref, N, Nc)

    lam = lam_ref[...]
    c_half = (-0.5 * LRU_C) * (jnp.maximum(-lam, 0.0) + jnp.log1p(jnp.exp(-jnp.abs(lam))))

    def coeff_chunk(i, carry):
        r0 = pl.multiple_of(i * LRU_TN, LRU_TN)
        x = xcv_ref[pl.ds(r0, LRU_TN), :]
        t = jnp.tanh(jnp.dot(x.astype(BF16), wg_ref[0], preferred_element_type=F32) + bg_ref[0])
        hx = 0.5 * x
        for d, (a_ref, b_ref) in enumerate(((af_ref, bf_ref), (ab_ref, bb_ref))):
            c = c_half[d:d + 1, :]
            a = jnp.exp(c * t[:, (2 * d) * C:(2 * d + 1) * C] + c)
            a_ref[pl.ds(r0, LRU_TN), :] = a
            gated_x = hx * t[:, (2 * d + 1) * C:(2 * d + 2) * C] + hx
            b_ref[pl.ds(r0, LRU_TN), :] = jnp.sqrt(1.0 - a * a) * gated_x
        return carry

    lax.fori_loop(0, NT // LRU_TN, coeff_chunk, 0)

    ngc, ngt = Nc // 8, NT // 8

    def scan_step(i, carry):
        hf, hb = carry
        rf = pl.multiple_of(i * 8, 8)
        rb = pl.multiple_of(jnp.where(i < ngc, ngc - 1 - i, ngt + ngc - 1 - i) * 8, 8)
        hs_f, hf = _scan_group(af_ref[pl.ds(rf, 8), :], bf_ref[pl.ds(rf, 8), :], hf, False)
        hs_b, hb = _scan_group(ab_ref[pl.ds(rb, 8), :], bb_ref[pl.ds(rb, 8), :], hb, True)
        bf_ref[pl.ds(rf, 8), :] = hs_f
        bb_ref[pl.ds(rb, 8), :] = hs_b
        return hf, hb

    zero = jnp.zeros((1, C), F32)
    lax.fori_loop(0, ngt, scan_step, (zero, zero), unroll=4)

    def out_chunk(i, carry):
        r0 = pl.multiple_of(i * LRU_TN, LRU_TN)
        hsum = bf_ref[pl.ds(Nc + r0, LRU_TN), :] + bb_ref[pl.ds(Nc + r0, LRU_TN), :]
        gate = _gelu_tanh(ugl_ref[0, pl.ds(r0, LRU_TN), :].astype(F32))
        yl_ref[0, pl.ds(r0, LRU_TN), :] = (hsum * gate).astype(BF16)
        return carry

    lax.fori_loop(0, N // LRU_TN, out_chunk, 0)
    yc_ref[0] = ((bf_ref[0:Nc, :] + bb_ref[0:Nc, :]) * _gelu_tanh(ugc_ref[0].astype(F32))).astype(BF16)


def _rglru(ul, uc, conv_w, conv_b, wg, bg, lam):
    B, N, _ = ul.shape
    Nc = uc.shape[1]
    C = LRU_CW
    nh = LRU_W // C
    NT = N + Nc
    col = lambda base: (lambda b, j: (b, 0, base // C + j))
    par = lambda rows: pl.BlockSpec((rows, C), lambda b, j: (0, j))
    return pl.pallas_call(
        _lru_kernel,
        out_shape=(jax.ShapeDtypeStruct((B, N, LRU_W), BF16), jax.ShapeDtypeStruct((B, Nc, LRU_W), BF16)),
        grid=(B, nh),
        in_specs=[pl.BlockSpec((1, N, C), col(0)), pl.BlockSpec((1, N, C), col(LRU_W)),
                  pl.BlockSpec((1, Nc, C), col(0)), pl.BlockSpec((1, Nc, C), col(LRU_W)),
                  par(CONV_W), par(1),
                  pl.BlockSpec((1, C, 4 * C), lambda b, j: (j, 0, 0)),
                  pl.BlockSpec((1, 1, 4 * C), lambda b, j: (j, 0, 0)),
                  par(2)],
        out_specs=(pl.BlockSpec((1, N, C), lambda b, j: (b, 0, j)),
                   pl.BlockSpec((1, Nc, C), lambda b, j: (b, 0, j))),
        scratch_shapes=[pltpu.VMEM((N + 2 * LRU_HALO, C), F32), pltpu.VMEM((Nc + 2 * LRU_HALO, C), F32),
                        pltpu.VMEM((NT, C), F32)] + [pltpu.VMEM((NT, C), F32)] * 4,
        compiler_params=_cparams(("parallel", "arbitrary"), 58),
        name="rglru",
    )(ul, ul, uc, uc, conv_w, conv_b.reshape(1, LRU_W), wg, bg, lam)


def _lru_gate_weights(w_a, b_a, w_x, b_x):
    C = LRU_CW
    nh = LRU_W // C
    kb = C // LRU_BS

    def dense(w):
        w = w.reshape(nh, kb, LRU_BS, LRU_BS)
        eye = jnp.eye(kb, dtype=w.dtype)
        return jnp.einsum('hkij,kl->hkilj', w, eye).reshape(nh, C, C)

    wg = jnp.concatenate([dense(w_a[0]), dense(w_x[0]), dense(w_a[1]), dense(w_x[1])], axis=-1)
    bg = jnp.stack([b_a[0], b_x[0], b_a[1], b_x[1]], axis=0).reshape(4, nh, C)
    bg = jnp.transpose(bg, (1, 0, 2)).reshape(nh, 1, 4 * C)
    return (0.5 * wg).astype(BF16), 0.5 * bg


GLA_SC = 256
GLA_UNROLL = 2
GLA_QK_W = GLA_HEADS * GLA_DK
GLA_V_W = GLA_HEADS * GLA_DV


def _split3(x):
    hi = x.astype(BF16)
    r1 = x - hi.astype(F32)
    mid = r1.astype(BF16)
    lo = (r1 - mid.astype(F32)).astype(BF16)
    return hi, mid, lo


def _gla_kernel(ql_ref, kl_ref, vl_ref, gl_ref, lrl_ref, kc_ref, vc_ref, lrc_ref, wa_ref, ba_ref, og_ref,
                y_ref, oacc_ref, st_ref):
    N, Nc = ql_ref.shape[1], kc_ref.shape[1]
    T = GLA_SC
    CH = GLA_CHUNK
    npair = GLA_HEADS // 2
    row = lax.broadcasted_iota(jnp.int32, (T, T), 0)
    colm = lax.broadcasted_iota(jnp.int32, (T, T), 1)
    same_chunk = (row // CH) == (colm // CH)
    causal = (same_chunk & (colm <= row), same_chunk & (colm >= row))
    tri = tuple(jnp.where(c, 1.0, 0.0).astype(BF16) for c in causal)
    lane_lo = lax.broadcasted_iota(jnp.int32, (T, LANE), 1) < GLA_DK
    srow = lax.broadcasted_iota(jnp.int32, (LANE, 2 * GLA_DV), 0)
    scol = lax.broadcasted_iota(jnp.int32, (LANE, 2 * GLA_DV), 1)
    state_mask = (srow < GLA_DK) == (scol < GLA_DV)
    qscale = GLA_DK ** -0.5

    def superchunks(streams, q_ref, k_ref, v_ref, lr_ref, with_output):
        nch = T // CH
        work = []
        for d, r0 in streams:
            z = jnp.dot(lr_ref[0, pl.ds(r0, T), :], wa_ref[d], preferred_element_type=F32) + ba_ref[d]
            work.append(dict(d=d, r0=r0, split=_split3(jax.nn.log_sigmoid(z) * (1.0 / GLA_TAU))))
        for w in work:
            d = w['d']
            cum = sum(jnp.dot(tri[d], t, preferred_element_type=F32) for t in w['split'])
            last = (CH - 1) if d == 0 else 0
            tot = [cum[c * CH + last:c * CH + last + 1, :] for c in range(nch)]
            tot_rows = jnp.concatenate([jnp.broadcast_to(t, (CH, GLA_QK_W)) for t in tot], axis=0)
            k = k_ref[0, pl.ds(w['r0'], T), :].astype(F32)
            w.update(cum=cum, tot=tot, k=k, v=v_ref[0, pl.ds(w['r0'], T), :],
                     k_dec=(k * jnp.exp(tot_rows - cum)).astype(BF16))
        for w in work:
            ds, dec = {}, {}
            for c in range(nch):
                rs = slice(c * CH, (c + 1) * CH)
                for p in range(npair):
                    kd = w['k_dec'][rs, p * LANE:(p + 1) * LANE]
                    vv = w['v'][rs, p * 2 * GLA_DV:(p + 1) * 2 * GLA_DV]
                    kv = lax.dot_general(kd, vv, (((0,), (0,)), ((), ())), preferred_element_type=F32)
                    ds[c, p] = jnp.where(state_mask, kv, 0.0)
                    dec_row = jnp.exp(w['tot'][c][:, p * LANE:(p + 1) * LANE])
                    dec_col = jnp.transpose(jnp.broadcast_to(dec_row, (LANE, LANE)))
                    dec[c, p] = jnp.concatenate([dec_col, dec_col], axis=1)
            w.update(ds=ds, dec=dec)
        for w in work:
            d = w['d']
            order = range(nch) if d == 0 else range(nch - 1, -1, -1)
            s_in = {}
            for p in range(npair):
                s = st_ref[d, p]
                for c in order:
                    s_in[c, p] = s.astype(BF16)
                    s = w['dec'][c, p] * s + w['ds'][c, p]
                st_ref[d, p] = s
            w.update(s_in=s_in)
        if not with_output:
            return
        for w in work:
            q = q_ref[0, pl.ds(w['r0'], T), :].astype(F32)
            w.update(q_dec=((q * qscale) * jnp.exp(w['cum'])).astype(BF16),
                     k_inv=(w['k'] * jnp.exp(-w['cum'])).astype(BF16))
        for w in work:
            d, r0 = w['d'], w['r0']
            for p in range(npair):
                qp = w['q_dec'][:, p * LANE:(p + 1) * LANE]
                kp = w['k_inv'][:, p * LANE:(p + 1) * LANE]
                o_inter = jnp.concatenate(
                    [jnp.dot(qp[c * CH:(c + 1) * CH], w['s_in'][c, p], preferred_element_type=F32)
                     for c in range(nch)], axis=0)
                for j in range(2):
                    h = 2 * p + j
                    qm = jnp.where(lane_lo if j == 0 else jnp.logical_not(lane_lo), qp, jnp.zeros_like(qp))
                    att = jnp.where(causal[d], _dot_nt(qm, kp), 0.0).astype(BF16)
                    o = (jnp.dot(att, w['v'][:, h * GLA_DV:(h + 1) * GLA_DV], preferred_element_type=F32)
                         + o_inter[:, j * GLA_DV:(j + 1) * GLA_DV])
                    oacc_ref[d, pl.ds(r0, T), h * GLA_DV:(h + 1) * GLA_DV] = o

    st_ref[...] = jnp.zeros_like(st_ref)
    ncs, nls = Nc // T, N // T

    def ctx_body(i, carry):
        streams = [(0, pl.multiple_of(i * T, T)), (1, pl.multiple_of((ncs - 1 - i) * T, T))]
        superchunks(streams, None, kc_ref, vc_ref, lrc_ref, False)
        return carry

    def lat_body(i, carry):
        streams = []
        for u in range(GLA_UNROLL):
            streams += [(0, pl.multiple_of((GLA_UNROLL * i + u) * T, T)),
                        (1, pl.multiple_of((nls - 1 - GLA_UNROLL * i - u) * T, T))]
        superchunks(streams, ql_ref, kl_ref, vl_ref, lrl_ref, True)
        return carry

    lax.fori_loop(0, ncs, ctx_body, 0)
    lax.fori_loop(0, nls // GLA_UNROLL, lat_body, 0)

    def fin(i, carry):
        r0 = pl.multiple_of(i * T, T)
        g = gl_ref[0, pl.ds(r0, T), :].astype(F32)
        for h in range(GLA_HEADS):
            sl = slice(h * GLA_DV, (h + 1) * GLA_DV)
            gh = g[:, sl]
            o = oacc_ref[0, pl.ds(r0, T), sl] + oacc_ref[1, pl.ds(r0, T), sl]
            y_ref[0, pl.ds(r0, T), sl] = (_rms(o, og_ref[...]) * (gh * jax.nn.sigmoid(gh))).astype(BF16)
        return carry

    lax.fori_loop(0, N // T, fin, 0)


def _gla(ul, uc, wa, ba, o_gain):
    B, N, _ = ul.shape
    Nc = uc.shape[1]
    blk = lambda n, w, base: pl.BlockSpec((1, n, w), lambda b: (b, 0, base // w))
    full = lambda shape: pl.BlockSpec(shape, lambda b: (0,) * len(shape))
    return pl.pallas_call(
        _gla_kernel,
        out_shape=jax.ShapeDtypeStruct((B, N, GLA_V_W), BF16),
        grid=(B,),
        in_specs=[blk(N, GLA_QK_W, OD_GQ), blk(N, GLA_QK_W, OD_GK), blk(N, GLA_V_W, OD_GV),
                  blk(N, GLA_V_W, OD_GG), blk(N, LANE, OD_GLR),
                  blk(Nc, GLA_QK_W, OD_GK), blk(Nc, GLA_V_W, OD_GV), blk(Nc, LANE, OD_GLR),
                  full((2, LANE, GLA_QK_W)), full((2, 1, GLA_QK_W)), full((1, GLA_DV))],
        out_specs=pl.BlockSpec((1, N, GLA_V_W), lambda b: (b, 0, 0)),
        scratch_shapes=[pltpu.VMEM((2, N, GLA_V_W), F32),
                        pltpu.VMEM((2, GLA_HEADS // 2, LANE, 2 * GLA_DV), F32)],
        compiler_params=_cparams(("parallel",), 48),
        name="gla",
    )(ul, ul, ul, ul, ul, uc, uc, uc, wa, ba, o_gain)


NA_W = NA_HEADS * NA_HD
NA_QROWS = 4
NA_KROWS = 12
NA_TQ = NA_QROWS * GRID_W
NA_TK = NA_KROWS * GRID_W
NA_NEG = -1e30
NA_BATCH_BLOCK = 4


def _na_kernel(q_ref, k0_ref, k1_ref, k2_ref, vt0_ref, vt1_ref, vt2_ref, kc_ref, vtc_ref, bias_ref, o_ref):
    lane_lo = lax.broadcasted_iota(jnp.int32, (NA_TQ, LANE), 1) < NA_HD
    k_refs = (k0_ref, k1_ref, k2_ref, kc_ref)
    nc = kc_ref.shape[1]
    for bb in range(q_ref.shape[0]):
        for h0 in range(0, NA_HEADS, ATTN_GROUP):
            heads = []
            for h in range(h0, h0 + ATTN_GROUP):
                sl = slice((h // 2) * LANE, (h // 2 + 1) * LANE)
                hs = slice(h * HEAD_SLOT, (h + 1) * HEAD_SLOT)
                qp = q_ref[bb, :, sl]
                qm = jnp.where(lane_lo if h % 2 == 0 else jnp.logical_not(lane_lo), qp, jnp.zeros_like(qp))
                biases = [bias_ref[0, h, t * NA_TQ:(t + 1) * NA_TQ, :] for t in range(3)] + [None]
                values_t = [r[bb, hs, :] for r in (vt0_ref, vt1_ref, vt2_ref)] + [vtc_ref[0, hs, bb * nc:(bb + 1) * nc]]
                heads.append((qm, [r[bb, :, sl] for r in k_refs], values_t, biases))
            outs = _attend_t(heads)
            for i in range(0, ATTN_GROUP, 2):
                hp = (h0 + i) // 2
                o_ref[bb, :, hp * LANE:(hp + 1) * LANE] = _pair_output(outs[i:i + 2], NA_HD).astype(BF16)


def _na_bias_tables(rpb, rows):
    H, ndi, ndj = rpb.shape
    rpb = rpb * LOG2_E
    W = GRID_W
    half = NA_WIN_COLS - 1
    zeros = jnp.zeros((H, ndi, W - 1 - half), rpb.dtype)
    vec = jnp.concatenate([rpb[:, :, half:], zeros, zeros, rpb[:, :, :half]], axis=-1)
    toep = jnp.tile(vec, (1, 1, W))[:, :, :W * (2 * W - 2)].reshape(H, ndi, W, 2 * W - 2)[..., :W]
    qc = np.arange(W)[:, None]
    kc = np.arange(W)[None, :]
    c0 = np.clip(qc - NA_WIN_COLS // 2, 0, W - NA_WIN_COLS)
    toep = jnp.where(jnp.asarray((kc >= c0) & (kc < c0 + NA_WIN_COLS)), toep, NA_NEG)
    toep = jnp.swapaxes(toep, 2, 3)
    nblk = rows // NA_QROWS
    plan = []
    for m in (0, 1, nblk - 1):
        kb = min(max(m - 1, 0), nblk - NA_KROWS // NA_QROWS)
        blocks = []
        for i in range(NA_KROWS):
            for a in range(NA_QROWS):
                kr, qr = kb * NA_QROWS + i, m * NA_QROWS + a
                r0 = min(max(qr - NA_WIN_ROWS // 2, 0), rows - NA_WIN_ROWS)
                blocks.append((i, a, kr - qr + NA_WIN_ROWS - 1 if r0 <= kr < r0 + NA_WIN_ROWS else None))
        plan.append(blocks)

    def expand_kernel(toep_ref, out_ref):
        for v, blocks in enumerate(plan):
            for i, a, di in blocks:
                blk = jnp.full((W, W), NA_NEG, F32) if di is None else toep_ref[0, di]
                out_ref[v, 0, i * W:(i + 1) * W, a * W:(a + 1) * W] = blk

    return pl.pallas_call(
        expand_kernel,
        out_shape=jax.ShapeDtypeStruct((len(plan), H, NA_TK, NA_TQ), F32),
        grid=(H,),
        in_specs=[pl.BlockSpec((1, ndi, W, W), lambda h: (h, 0, 0, 0))],
        out_specs=pl.BlockSpec((len(plan), 1, NA_TK, NA_TQ), lambda h: (0, h, 0, 0)),
        compiler_params=_cparams(("arbitrary",), 32),
        name="na_bias_expand",
    )(toep)


def _natten(ul, vt, uc, vtc, bias):
    B, N, _ = ul.shape
    Nc = uc.shape[1]
    nblk = N // NA_TQ
    kmax = nblk - NA_KROWS // NA_QROWS
    VS = NA_HEADS * HEAD_SLOT
    qcol, kcol = OD_NQ // NA_W, OD_NK // NA_W

    bb = np.gcd(B, NA_BATCH_BLOCK).item()

    def kspec(t):
        return pl.BlockSpec((bb, NA_TQ, NA_W), lambda m, b: (b, jnp.clip(m - 1, 0, kmax) + t, kcol))

    def vspec(t):
        return pl.BlockSpec((bb, VS, NA_TQ), lambda m, b: (b, 0, jnp.clip(m - 1, 0, kmax) + t))

    return pl.pallas_call(
        _na_kernel,
        out_shape=jax.ShapeDtypeStruct((B, N, NA_W), BF16),
        grid=(nblk, B // bb),
        in_specs=[pl.BlockSpec((bb, NA_TQ, NA_W), lambda m, b: (b, m, qcol)),
                  kspec(0), kspec(1), kspec(2), vspec(0), vspec(1), vspec(2),
                  pl.BlockSpec((bb, Nc, NA_W), lambda m, b: (b, 0, kcol)),
                  pl.BlockSpec((1, VS, bb * Nc), lambda m, b: (0, 0, b)),
                  pl.BlockSpec((1, NA_HEADS, NA_TK, NA_TQ),
                               lambda m, b: (jnp.where(m == 0, 0, jnp.where(m == nblk - 1, 2, 1)), 0, 0, 0))],
        out_specs=pl.BlockSpec((bb, NA_TQ, NA_W), lambda m, b: (b, m, 0)),
        compiler_params=_cparams(("arbitrary", "arbitrary"), 48),
        name="natten",
    )(ul, ul, ul, ul, vt, vt, vt, uc, vtc, bias)


def _place(w, layout, total):
    pieces, pos = [], 0
    for src, width, dst in sorted(layout, key=lambda item: item[2]):
        if dst > pos:
            pieces.append(jnp.zeros((w.shape[0], dst - pos), w.dtype))
        pieces.append(w[:, src:src + width])
        pos = dst + width
    if pos < total:
        pieces.append(jnp.zeros((w.shape[0], total - pos), w.dtype))
    return jnp.concatenate(pieces, axis=1)


def _ev_in_weight(w):
    lay = [(0, 256, EV_UQ), (256, 128, EV_UKV), (384, MLA_ROPE, EV_UKR + MLA_NOPE),
           (416, LRU_W, EV_UX), (416 + LRU_W, LRU_W, EV_UG)]
    return _place(w, lay, EV_PAD).astype(BF16)


def _od_in_weight(w):
    src = np.cumsum([0, 256, 256, 512, 512, 2 * GLA_LR, 512, 512, 512])
    dst = [OD_GQ, OD_GK, OD_GV, OD_GG, OD_GLR, OD_NQ, OD_NK, OD_NV]
    lay = [(int(src[i]), int(src[i + 1] - src[i]), dst[i]) for i in range(8)]
    return _place(w, lay, OD_PAD).astype(BF16)


def _pad_heads(w, heads, width):
    k = w.shape[0]
    return jnp.pad(w.reshape(k, heads, width), ((0, 0), (0, 0), (0, HEAD_SLOT - width))).reshape(k, heads * HEAD_SLOT)


def kernel(x, c, ctx, c_ctx, ada_w, ada_b, norm_mix, norm_mlp, w_out, mlp_w1, mlp_w2,
           ev_w_in, mla_q_norm, mla_w_uq, mla_kv_norm, mla_w_ukv, mla_q_gain, mla_k_gain,
           lru_conv_w, lru_conv_b, lru_w_a, lru_b_a, lru_w_x, lru_b_x, lru_lam,
           od_w_in, gla_w_a, gla_b_a, gla_o_gain, na_q_gain, na_k_gain, na_rpb):
    B, N, D = x.shape
    Nc = ctx.shape[1]
    depth = ada_w.shape[0]

    R = -(-(B + 1) // 8) * 8
    cc = jnp.concatenate([c, c_ctx[None], jnp.zeros((R - B - 1, D), c.dtype)], axis=0)
    mods = _modulation(cc, ada_w, ada_b).reshape(depth * R, 1, 6 * D)

    w1, w2, wo = mlp_w1.astype(BF16), mlp_w2.astype(BF16), w_out.astype(BF16)
    xl = x
    xc = ctx.reshape(1, B * Nc, D)
    for l in range(depth):
        last = l == depth - 1
        j = l // 2
        row_l, row_c = l * R, l * R + B
        if l % 2 == 0:
            w_in = _ev_in_weight(ev_w_in[j])
            wuq = _pad_heads(mla_w_uq[j], MLA_HEADS, MLA_QK)
            wukv = mla_w_ukv[j].reshape(MLA_KV_LORA, MLA_HEADS, MLA_NOPE + MLA_V)
            wuk = _pad_heads(wukv[:, :, :MLA_NOPE].reshape(MLA_KV_LORA, -1), MLA_HEADS, MLA_NOPE).astype(BF16)
            wuv = _pad_heads(wukv[:, :, MLA_NOPE:].reshape(MLA_KV_LORA, -1), MLA_HEADS, MLA_V).astype(BF16)
            proj = functools.partial(_ev_in_proj, g=norm_mix[l], mods=mods, w=w_in, qn=mla_q_norm[j].reshape(1, -1),
                                     kvn=mla_kv_norm[j].reshape(1, -1), wuq=wuq.astype(BF16),
                                     wuqr=_rope_partner_weight(wuq).astype(BF16), wuk=wuk, wuv=wuv)
            ul, ql, kl, vl = proj(xl, row0=row_l, tables=_rope_tables(N, mla_q_gain[j], mla_k_gain[j], True))
            uc, qc, kc, vc = proj(xc, row0=row_c, tables=_rope_tables(B * Nc, mla_q_gain[j], mla_k_gain[j], False))
            uc, qc, kc = (t.reshape(B, Nc, -1) for t in (uc, qc, kc))
            ya_l = _mla_attention(ql, kl, vl, kc, vc)
            ya_c = None if last else _mla_attention(qc, None, None, kc, vc)
            wg, bg = _lru_gate_weights(lru_w_a[j], lru_b_a[j], lru_w_x[j], lru_b_x[j])
            yb_l, yb_c = _rglru(ul, uc, lru_conv_w[j], lru_conv_b[j], wg, bg, lru_lam[j])
        else:
            w_in = _od_in_weight(od_w_in[j])
            na_gains = (jnp.tile(na_q_gain[j] * (NA_HD ** -0.5 * LOG2_E), NA_HEADS).reshape(1, NA_HEADS * NA_HD),
                        jnp.tile(na_k_gain[j], NA_HEADS).reshape(1, NA_HEADS * NA_HD))
            ul, nvt = _norm_mod_matmul(xl, norm_mix[l], mods, row_l, w_in, na_gains)
            uc, nvtc = _norm_mod_matmul(xc, norm_mix[l], mods, row_c, w_in, na_gains)
            uc = uc.reshape(B, Nc, OD_PAD)
            if not last:
                raise NotImplementedError("context outputs of the odd-layer mixers are only needed when depth > 2")
            wa = jnp.stack([jnp.pad(gla_w_a[j, d], ((d * GLA_LR, LANE - (d + 1) * GLA_LR), (0, 0))) for d in range(2)])
            ya_l = _gla(ul, uc, wa.astype(BF16), gla_b_a[j].reshape(2, 1, GLA_QK_W), gla_o_gain[j].reshape(1, GLA_DV))
            ya_c = None
            yb_l = _natten(ul, nvt, uc, nvtc, _na_bias_tables(na_rpb[j], N // GRID_W))
            yb_c = None
        xl = _out_proj_mlp(xl, ya_l, yb_l, wo, norm_mlp[l], mods, row_l, w1, w2, l)
        if not last:
            xc = _out_proj_mlp(xc, ya_c.reshape(1, B * Nc, -1), yb_c.reshape(1, B * Nc, -1), wo, norm_mlp[l], mods,
                               row_c, w1, w2, l)
    return xl
```
